```python
import math
import jax, jax.numpy as jnp
from jax import lax
import numpy as np

D_MODEL = 1024
BATCH = 8
SEQ = 2048
DEPTH = 1
DEC_BATCH = 128
DEC_SEQ = 1
PAST_LEN = 16384
PAGE_SIZE = 128

A_HEADS = 8
A_HEAD_DIM = 128
A_WIDTH = A_HEADS * A_HEAD_DIM
CHUNK = 128
B_HEADS = 16
B_HEAD_DIM = 64
B_WIDTH = B_HEADS * B_HEAD_DIM
B_GROUPS = 2
D_STATE = 128
CONV_W = 4
SSD_CHUNK = 128
CONV_DIM = B_WIDTH + 2 * B_GROUPS * D_STATE
MIX_WIDTH = A_WIDTH + B_WIDTH
IN_PROJ = 2 * A_WIDTH + B_WIDTH + CONV_DIM + B_HEADS
N_KEYS = 128
N_EXPERTS = N_KEYS * N_KEYS
R_HEADS = 8
D_QUERY = 256
D_HALF = D_QUERY // 2
TOPK = 16
PEER_TOKEN_BLOCK = 256
N_MOD = 6
EPS = 1e-6

kernel_name = "hybrid_chunkmlp_ssd_peer_adaln_step"


def rmsnorm(x, g):
    xf = x.astype(jnp.float32)
    y = xf * lax.rsqrt(jnp.mean(xf * xf, axis=-1, keepdims=True) + EPS)
    return (y * g.astype(jnp.float32)).astype(x.dtype)


def chunk_mlp(u, v, w_s, b_s):
    bn, L = u.shape[0], u.shape[1]
    pad = (-L) % CHUNK
    vp = jnp.pad(v, ((0, 0), (0, pad), (0, 0), (0, 0)))
    nc = (L + pad) // CHUNK
    vc = vp.reshape(bn, nc, CHUNK, A_HEADS, A_HEAD_DIM)
    causal = jnp.tril(jnp.ones((CHUNK, CHUNK), dtype=bool))
    w = jnp.where(causal[None], w_s, jnp.zeros_like(w_s))
    s = jnp.einsum('hij,bcjhd->bcihd', w, vc) + b_s.T[None, None, :, :, None]
    s = s.reshape(bn, nc * CHUNK, A_HEADS, A_HEAD_DIM)[:, :L]
    return u * s


def causal_conv(xbc, conv_state, conv_w, conv_b):
    L = xbc.shape[1]
    xp = jnp.concatenate([conv_state.astype(xbc.dtype), xbc], axis=1)
    out = conv_b + xp[:, 0:L] * conv_w[0]
    for k in range(1, CONV_W):
        out = out + xp[:, k:k + L] * conv_w[k]
    return jax.nn.silu(out), xp[:, -(CONV_W - 1):]


def ssd(x, dt, a, bm, cm, h0):
    bn, L = x.shape[0], x.shape[1]
    K = B_HEADS // B_GROUPS
    Q = SSD_CHUNK
    pad = (-L) % Q

    def padt(t):
        return jnp.pad(t.astype(jnp.float32), ((0, 0), (0, pad)) + ((0, 0),) * (t.ndim - 2))

    x, dt, bm, cm = padt(x), padt(dt), padt(bm), padt(cm)
    nc = (L + pad) // Q
    xc = x.reshape(bn, nc, Q, B_GROUPS, K, B_HEAD_DIM)
    dtc = dt.reshape(bn, nc, Q, B_GROUPS, K)
    bc = bm.reshape(bn, nc, Q, B_GROUPS, D_STATE)
    cc = cm.reshape(bn, nc, Q, B_GROUPS, D_STATE)
    acum = jnp.cumsum(dtc * a.reshape(B_GROUPS, K), axis=2)
    seg = acum[:, :, :, None] - acum[:, :, None, :]
    causal = jnp.tril(jnp.ones((Q, Q), dtype=bool))[:, :, None, None]
    lmat = jnp.exp(jnp.where(causal, seg, -jnp.inf))
    xdt = xc * dtc[..., None]
    cb = jnp.einsum('bcign,bcjgn->bcijg', cc, bc)
    y_diag = jnp.einsum('bcijgk,bcjgkp->bcigkp', cb[..., None] * lmat, xdt)
    decay_end = jnp.exp(acum[:, :, -1:] - acum)
    chunk_states = jnp.einsum('bcjgn,bcjgk,bcjgkp->bcgkpn', bc, decay_end, xdt)
    chunk_decay = jnp.exp(acum[:, :, -1])

    def step(h, inp):
        s, d = inp
        return h * d[..., None, None] + s, h

    h_init = h0.astype(jnp.float32).reshape(bn, B_GROUPS, K, B_HEAD_DIM, D_STATE)
    h_final, h_prev = lax.scan(step, h_init, (jnp.moveaxis(chunk_states, 1, 0), jnp.moveaxis(chunk_decay, 1, 0)))
    h_prev = jnp.moveaxis(h_prev, 0, 1)
    y_off = jnp.einsum('bcign,bcigk,bcgkpn->bcigkp', cc, jnp.exp(acum), h_prev)
    y = (y_diag + y_off).reshape(bn, nc * Q, B_HEADS, B_HEAD_DIM)[:, :L]
    return y, h_final.reshape(bn, B_HEADS, B_HEAD_DIM, D_STATE)


def peer(h, w_q, sub_keys, expert_u, expert_v):
    shape = h.shape
    t = h.reshape(-1, D_MODEL)
    T = t.shape[0]
    pad = (-T) % PEER_TOKEN_BLOCK
    blocks = jnp.pad(t, ((0, pad), (0, 0))).reshape(-1, PEER_TOKEN_BLOCK, D_MODEL)

    def one_block(xb):
        q = (xb @ w_q).reshape(PEER_TOKEN_BLOCK, R_HEADS, 2, D_HALF)
        s = jnp.einsum('thsd,hsnd->thsn', q, sub_keys).astype(jnp.float32)
        s_top, i_top = lax.top_k(s, TOPK)
        cand = s_top[:, :, 0, :, None] + s_top[:, :, 1, None, :]
        cand_idx = i_top[:, :, 0, :, None] * N_KEYS + i_top[:, :, 1, None, :]
        best, pos = lax.top_k(cand.reshape(PEER_TOKEN_BLOCK, R_HEADS, TOPK * TOPK), TOPK)
        idx = jnp.take_along_axis(cand_idx.reshape(PEER_TOKEN_BLOCK, R_HEADS, TOPK * TOPK), pos, axis=-1)
        g = jax.nn.softmax(best, axis=-1)
        act = jax.nn.gelu(jnp.einsum('td,thkd->thk', xb, expert_u[idx]).astype(jnp.float32), approximate=False)
        coef = (g * act).astype(xb.dtype)
        return jnp.einsum('thk,thkd->td', coef, expert_v[idx])

    out = lax.map(one_block, blocks)
    return out.reshape(-1, D_MODEL)[:T].reshape(shape)


def layer(x, c, ssm0, conv0, w_ada, b_ada, g_mix, w_in, g_v, w_s, b_s, conv_w, conv_b,
          dt_bias, a_log, d_skip, g_ssm, w_out, g_ffn, w_q, sub_keys, expert_u, expert_v):
    bn, L = x.shape[0], x.shape[1]
    mod = jax.nn.silu(c) @ w_ada + b_ada
    sh1, sc1, ga1, sh2, sc2, ga2 = [m[:, None, :] for m in jnp.split(mod, N_MOD, axis=-1)]

    h = rmsnorm(x, g_mix) * (1 + sc1) + sh1
    proj = h @ w_in
    u, v, z, xbc, dt_raw = jnp.split(
        proj, [A_WIDTH, 2 * A_WIDTH, 2 * A_WIDTH + B_WIDTH, 2 * A_WIDTH + B_WIDTH + CONV_DIM], axis=-1)

    u = jax.nn.gelu(u, approximate=False).reshape(bn, L, A_HEADS, A_HEAD_DIM)
    v = rmsnorm(jax.nn.gelu(v, approximate=False).reshape(bn, L, A_HEADS, A_HEAD_DIM),
                g_v.reshape(A_HEADS, A_HEAD_DIM))
    y_a = chunk_mlp(u, v, w_s, b_s).reshape(bn, L, A_WIDTH)

    xbc, conv_new = causal_conv(xbc, conv0, conv_w, conv_b)
    xs, bm, cm = jnp.split(xbc, [B_WIDTH, B_WIDTH + B_GROUPS * D_STATE], axis=-1)
    xs = xs.reshape(bn, L, B_HEADS, B_HEAD_DIM)
    dt = jax.nn.softplus(dt_raw.astype(jnp.float32) + dt_bias.astype(jnp.float32))
    a = -jnp.exp(a_log.astype(jnp.float32))
    y_s, ssm_new = ssd(xs, dt, a, bm.reshape(bn, L, B_GROUPS, D_STATE), cm.reshape(bn, L, B_GROUPS, D_STATE), ssm0)
    y_s = y_s + d_skip.astype(jnp.float32)[:, None] * xs.astype(jnp.float32)
    y_s = y_s * jax.nn.silu(z.astype(jnp.float32)).reshape(bn, L, B_HEADS, B_HEAD_DIM)
    y_b = rmsnorm(y_s.reshape(bn, L, B_GROUPS, B_WIDTH // B_GROUPS), g_ssm.reshape(B_GROUPS, -1))
    y_b = y_b.reshape(bn, L, B_WIDTH).astype(x.dtype)

    x = x + ga1 * (jnp.concatenate([y_a, y_b], axis=-1) @ w_out)

    h2 = rmsnorm(x, g_ffn) * (1 + sc2) + sh2
    x = x + ga2 * peer(h2, w_q, sub_keys, expert_u, expert_v)
    return x, ssm_new, conv_new, v.reshape(bn, L, A_WIDTH)


def trunk(x, c, ssm0, conv0, layer_w, w_ada_f, b_ada_f, g_final):
    ssms, convs, vs = [], [], []
    for l in range(DEPTH):
        x, s_new, c_new, v_rows = layer(x, c, ssm0[l], conv0[l], *[w[l] for w in layer_w])
        ssms.append(s_new)
        convs.append(c_new)
        vs.append(v_rows)
    shf, scf = jnp.split(jax.nn.silu(c) @ w_ada_f + b_ada_f, 2, axis=-1)
    y = rmsnorm(x, g_final) * (1 + scf[:, None, :]) + shf[:, None, :]
    return y, jnp.stack(ssms), jnp.stack(convs), jnp.stack(vs)


def setup_inputs(seed: int = 0) -> dict:
    key = jax.random.key(seed)
    ks = jax.random.split(key, 32)
    f32 = jnp.float32

    def nrm(k, shape, scale):
        return jax.random.normal(k, shape, f32) * scale

    def gain(k, shape):
        return 1.0 + 0.01 * jax.random.normal(k, shape, f32)

    dt0 = jnp.exp(jax.random.uniform(ks[20], (DEPTH, B_HEADS), f32, math.log(1e-3), math.log(1e-1)))
    return {
        "x_prompt": nrm(ks[0], (BATCH, SEQ, D_MODEL), 1.0),
        "x_sample": nrm(ks[1], (DEC_BATCH, DEC_SEQ, D_MODEL), 1.0),
        "c_prompt": nrm(ks[2], (BATCH, D_MODEL), 1.0),
        "c_sample": nrm(ks[3], (DEC_BATCH, D_MODEL), 1.0),
        "state_ssm": nrm(ks[4], (DEPTH, DEC_BATCH, B_HEADS, B_HEAD_DIM, D_STATE), 0.5),
        "state_conv": nrm(ks[5], (DEPTH, DEC_BATCH, CONV_W - 1, CONV_DIM), 1.0),
        "w_ada": nrm(ks[6], (DEPTH, D_MODEL, N_MOD * D_MODEL), D_MODEL ** -0.5),
        "b_ada": nrm(ks[7], (DEPTH, N_MOD * D_MODEL), 0.01),
        "g_mix": gain(ks[8], (DEPTH, D_MODEL)),
        "w_in": nrm(ks[9], (DEPTH, D_MODEL, IN_PROJ), D_MODEL ** -0.5),
        "g_v": gain(ks[10], (DEPTH, A_WIDTH)),
        "w_s": nrm(ks[11], (DEPTH, A_HEADS, CHUNK, CHUNK), CHUNK ** -0.5),
        "b_s": gain(ks[12], (DEPTH, A_HEADS, CHUNK)),
        "conv_w": nrm(ks[13], (DEPTH, CONV_W, CONV_DIM), 0.5),
        "conv_b": nrm(ks[14], (DEPTH, CONV_DIM), 0.01),
        "dt_bias": dt0 + jnp.log(-jnp.expm1(-dt0)),
        "a_log": jnp.log(jax.random.uniform(ks[15], (DEPTH, B_HEADS), f32, 1.0, 16.0)),
        "d_skip": gain(ks[16], (DEPTH, B_HEADS)),
        "g_ssm": gain(ks[17], (DEPTH, B_WIDTH)),
        "w_out": nrm(ks[18], (DEPTH, MIX_WIDTH, D_MODEL), MIX_WIDTH ** -0.5),
        "g_ffn": gain(ks[19], (DEPTH, D_MODEL)),
        "w_q": nrm(ks[21], (DEPTH, D_MODEL, R_HEADS * D_QUERY), D_MODEL ** -0.5),
        "sub_keys": nrm(ks[22], (DEPTH, R_HEADS, 2, N_KEYS, D_HALF), D_HALF ** -0.5),
        "expert_u": nrm(ks[23], (DEPTH, N_EXPERTS, D_MODEL), D_MODEL ** -0.5),
        "expert_v": nrm(ks[24], (DEPTH, N_EXPERTS, D_MODEL), 0.3),
        "w_ada_f": nrm(ks[25], (D_MODEL, 2 * D_MODEL), D_MODEL ** -0.5),
        "b_ada_f": nrm(ks[26], (2 * D_MODEL,), 0.01),
        "g_final": gain(ks[27], (D_MODEL,)),
    }


def reference(x_prompt, x_sample, c_prompt, c_sample, state_ssm, state_conv, w_ada, b_ada, g_mix, w_in,
              g_v, w_s, b_s, conv_w, conv_b, dt_bias, a_log, d_skip, g_ssm, w_out, g_ffn, w_q, sub_keys,
              expert_u, expert_v, w_ada_f, b_ada_f, g_final):
    layer_w = (w_ada, b_ada, g_mix, w_in, g_v, w_s, b_s, conv_w, conv_b, dt_bias, a_log, d_skip, g_ssm,
               w_out, g_ffn, w_q, sub_keys, expert_u, expert_v)
    bp = x_prompt.shape[0]
    ssm_zero = jnp.zeros((DEPTH, bp, B_HEADS, B_HEAD_DIM, D_STATE), jnp.float32)
    conv_zero = jnp.zeros((DEPTH, bp, CONV_W - 1, CONV_DIM), x_prompt.dtype)
    y_prompt, ssm_prompt, conv_prompt, _ = trunk(x_prompt, c_prompt, ssm_zero, conv_zero, layer_w,
                                                 w_ada_f, b_ada_f, g_final)
    y_sample, ssm_sample, conv_sample, chunk_v_sample = trunk(x_sample, c_sample, state_ssm, state_conv, layer_w,
                                                              w_ada_f, b_ada_f, g_final)
    return (y_prompt, y_sample, ssm_prompt, conv_prompt, ssm_sample, conv_sample, chunk_v_sample)
```

```python
import functools
import math

import jax
import jax.numpy as jnp
from jax import lax
from jax.experimental import pallas as pl
from jax.experimental.pallas import tpu as pltpu

F32 = jnp.float32
BF16 = jnp.bfloat16
NEG_INF = float("-inf")

EPS = 1e-6
A_HEADS = 8
A_HEAD_DIM = 128
CHUNK = 128
B_HEADS = 16
B_HEAD_DIM = 64
B_GROUPS = 2
D_STATE = 128
CONV_W = 4
N_KEYS = 128
R_HEADS = 8
TOPK = 16
N_MOD = 6

LANES = 128
SUBLANES = 8
VMEM_LIMIT = 56 * 1024 * 1024

MIX_ROWS = 256
PEER_TOKENS = 256
PEER_EXPERTS = 2048
S2_TOKENS = 8


def _dot(a, b):
    return jnp.dot(a, b, preferred_element_type=F32)


def _dot_nt(a, b):
    return lax.dot_general(a, b, (((1,), (1,)), ((), ())), preferred_element_type=F32)


def _dot_tn(a, b):
    return lax.dot_general(a, b, (((0,), (0,)), ((), ())), preferred_element_type=F32)


def _split2(x):
    hi = x.astype(BF16)
    lo = (x - hi.astype(F32)).astype(BF16)
    return hi, lo


def _split3(x):
    hi = x.astype(BF16)
    r = x - hi.astype(F32)
    mid = r.astype(BF16)
    lo = (r - mid.astype(F32)).astype(BF16)
    return hi, mid, lo


def _dot_x3(a, b):
    a1, a2 = _split2(a)
    b1, b2 = _split2(b)
    return _dot(a1, b1) + (_dot(a1, b2) + _dot(a2, b1))


def _silu(x):
    return x / (1.0 + jnp.exp(-x))


def _gelu(x):
    return 0.5 * x * (1.0 + lax.erf(x * (1.0 / math.sqrt(2.0))))


def _softplus(x):
    return jnp.maximum(x, 0.0) + jnp.log1p(jnp.exp(-jnp.abs(x)))


def _rmsnorm(x, g):
    return x * lax.rsqrt(jnp.mean(x * x, axis=-1, keepdims=True) + EPS) * g


def _full(shape):
    nd = len(shape)
    return pl.BlockSpec(shape, lambda *_: (0,) * nd)


def _params(sem):
    return pltpu.CompilerParams(dimension_semantics=sem, vmem_limit_bytes=VMEM_LIMIT)


def _ada_kernel(c_ref, w_ref, b_ref, o_ref):
    o_ref[...] = _dot_x3(_silu(c_ref[...]), w_ref[...]) + b_ref[...]


def _ada_call(c, w, b, tn=512):
    m, k = c.shape
    n = w.shape[1]
    return pl.pallas_call(
        _ada_kernel,
        grid=(n // tn,),
        in_specs=[_full((m, k)), pl.BlockSpec((k, tn), lambda j: (0, j)), pl.BlockSpec((1, tn), lambda j: (0, j))],
        out_specs=pl.BlockSpec((m, tn), lambda j: (0, j)),
        out_shape=jax.ShapeDtypeStruct((m, n), F32),
        compiler_params=_params(("parallel",)),
        name="ada",
    )(c, w, b.reshape(1, n))


def _gate_mlp_chunk(u, v, gv, ws_ref, bs_ref, causal):
    ys, vs = [], []
    for g in range(A_HEADS):
        sl = slice(g * A_HEAD_DIM, (g + 1) * A_HEAD_DIM)
        ug = _gelu(u[:, sl])
        vn = _rmsnorm(_gelu(v[:, sl]), gv[:, sl])
        w = jnp.where(causal, ws_ref[g], 0.0).astype(BF16)
        s = _dot(w, vn.astype(BF16)) + bs_ref[g]
        ys.append((ug * s).astype(BF16))
        vs.append(vn)
    return ys, vs


def _mix_kernel(x_ref, mod_ref, gmix_ref, wuv_ref, wz_ref, wxbc_ref, wdt_ref, wdtT_ref,
                dtb_ref, dtbT_ref, alog_ref, alogT_ref, exp_ref, gv_ref, ws_ref, bs_ref,
                cw_ref, cb_ref, dsk_ref, gssm_ref, wout_ref,
                o_ref, ssm_ref, ctail_ref,
                hst_ref, tail_ref, uv_ref, z_ref, xs_ref, dt_ref, dtT_ref, ycat_ref):
    t = pl.program_id(1)
    tm = x_ref.shape[0]
    n_chunks = tm // CHUNK

    @pl.when(t == 0)
    def _():
        hst_ref[...] = jnp.zeros_like(hst_ref)
        tail_ref[...] = jnp.zeros_like(tail_ref)

    x = x_ref[...]
    h = _rmsnorm(x, gmix_ref[...]) * (1.0 + mod_ref[1]) + mod_ref[0]
    hb = h.astype(BF16)
    uv_ref[...] = _dot(hb, wuv_ref[...])
    z_ref[...] = _dot(hb, wz_ref[...])
    dt_ref[...] = _softplus(_dot(hb, wdt_ref[...]) + dtb_ref[...])
    dtT_ref[...] = _softplus(_dot_nt(wdtT_ref[...], hb) + dtbT_ref[...])

    cur = _dot(hb, wxbc_ref[...])
    prev = tail_ref[...]
    row8 = lax.broadcasted_iota(jnp.int32, (SUBLANES, 1), 0)
    acc = cb_ref[...] + cur * cw_ref[CONV_W - 1:CONV_W, :]
    for k in range(1, CONV_W):
        r = pltpu.roll(cur, k, axis=0)
        head = jnp.where(row8 < k, pltpu.roll(prev, k, axis=0), r[0:SUBLANES])
        shifted = jnp.concatenate([head, r[SUBLANES:]], axis=0)
        acc = acc + shifted * cw_ref[CONV_W - 1 - k:CONV_W - k, :]
    xs_ref[...] = _silu(acc)
    new_tail = cur[tm - SUBLANES:tm]
    tail_ref[...] = new_tail
    ctail_ref[...] = new_tail

    ri = lax.broadcasted_iota(jnp.int32, (CHUNK, CHUNK), 0)
    ci = lax.broadcasted_iota(jnp.int32, (CHUNK, CHUNK), 1)
    causal = ri >= ci
    tril = jnp.where(causal, 1.0, 0.0).astype(BF16)
    triu = jnp.where(ri <= ci, 1.0, 0.0).astype(BF16)
    low_half = ri < B_HEAD_DIM
    low_lane = ci < B_HEAD_DIM
    a_col = -jnp.exp(alog_ref[...])
    a_row = -jnp.exp(alogT_ref[...])

    def chunk_body(c, carry):
        r0 = pl.multiple_of(c * CHUNK, CHUNK)
        rows = pl.ds(r0, CHUNK)

        ys, _ = _gate_mlp_chunk(uv_ref[rows, 0:1024], uv_ref[rows, 1024:2048], gv_ref[...], ws_ref, bs_ref, causal)
        for g in range(A_HEADS):
            ycat_ref[rows, g * A_HEAD_DIM:(g + 1) * A_HEAD_DIM] = ys[g]

        dt = dt_ref[rows, :]
        d_a = dt * a_col
        h1, h2, h3 = _split3(d_a)
        acum = _dot(tril, h1) + (_dot(tril, h2) + _dot(tril, h3))
        e1, e2, e3 = _split3(acum)
        acum_bc = _dot(e1, exp_ref[...]) + (_dot(e2, exp_ref[...]) + _dot(e3, exp_ref[...]))
        f1, f2 = _split2(dt)
        dt_bc = _dot(f1, exp_ref[...]) + _dot(f2, exp_ref[...])
        dt_t = dtT_ref[:, rows]
        g1, g2, g3 = _split3(dt_t * a_row)
        acum_t = _dot(g1, triu) + (_dot(g2, triu) + _dot(g3, triu))

        xs = xs_ref[rows, 0:1024]
        zz = z_ref[rows, :]
        y_pairs = []
        for g in range(B_GROUPS):
            bg = xs_ref[rows, 1024 + g * D_STATE:1024 + (g + 1) * D_STATE]
            cg = xs_ref[rows, 1280 + g * D_STATE:1280 + (g + 1) * D_STATE]
            cbm = _dot_nt(cg.astype(BF16), bg.astype(BF16))
            for pr in range(B_HEADS // B_GROUPS // 2):
                pair = g * 4 + pr
                xpair = xs[:, pair * 128:(pair + 1) * 128]
                hprev = hst_ref[pair]
                y_acc = None
                st_acc = None
                alasts = []
                for e in range(2):
                    hd = pair * 2 + e
                    ab = acum_bc[:, hd * 128:(hd + 1) * 128]
                    seg = ab - acum_t[hd:hd + 1, :]
                    lm = jnp.exp(jnp.where(causal, seg, NEG_INF))
                    gd = (cbm * lm * dt_t[hd:hd + 1, :]).astype(BF16)
                    sel_l = low_lane if e == 0 else jnp.logical_not(low_lane)
                    sel_r = low_half if e == 0 else jnp.logical_not(low_half)
                    xh = jnp.where(sel_l, xpair, 0.0).astype(BF16)
                    hp = jnp.where(sel_r, hprev, 0.0).astype(BF16)
                    cs = (cg * jnp.exp(ab)).astype(BF16)
                    alast = ab[CHUNK - 1:CHUNK, :]
                    bsc = (bg * (jnp.exp(alast - ab) * dt_bc[:, hd * 128:(hd + 1) * 128])).astype(BF16)
                    y_e = _dot(gd, xh) + _dot_nt(cs, hp)
                    st_e = _dot_tn(xh, bsc)
                    y_acc = y_e if y_acc is None else y_acc + y_e
                    st_acc = st_e if st_acc is None else st_acc + st_e
                    alasts.append(alast)
                decay = jnp.exp(jnp.where(low_half, alasts[0], alasts[1]))
                hst_ref[pair] = hprev * decay + st_acc
                y_pairs.append(y_acc)
        y = jnp.concatenate(y_pairs, axis=1) + dsk_ref[...] * xs
        y = y * _silu(zz)
        half = B_HEADS * B_HEAD_DIM // B_GROUPS
        for g in range(B_GROUPS):
            sl = slice(g * half, (g + 1) * half)
            ycat_ref[rows, 1024 + g * half:1024 + (g + 1) * half] = _rmsnorm(y[:, sl], gssm_ref[:, sl]).astype(BF16)
        return carry

    lax.fori_loop(0, n_chunks, chunk_body, 0)

    o_ref[...] = x + mod_ref[2] * _dot(ycat_ref[...], wout_ref[...])

    @pl.when(t == pl.num_programs(1) - 1)
    def _():
        ssm_ref[...] = hst_ref[...]


def _mix_call(x, mod1, p, tm=MIX_ROWS):
    b, s, d = x.shape
    n_pairs = B_HEADS // 2
    in_specs = [
        pl.BlockSpec((None, tm, d), lambda i, t: (i, t, 0)),
        pl.BlockSpec((None, 3, 1, d), lambda i, t: (i, 0, 0, 0)),
        _full((1, d)),
        _full(p["w_uv"].shape), _full(p["w_z"].shape), _full(p["w_xbc"].shape),
        _full(p["w_dt"].shape), _full(p["w_dtT"].shape),
        _full((1, LANES)), _full((LANES, 1)), _full((1, LANES)), _full((LANES, LANES)),
        _full(p["expand"].shape),
        _full((1, 1024)), _full((A_HEADS, CHUNK, CHUNK)), _full((A_HEADS, CHUNK, CHUNK)),
        _full((CONV_W, 1536)), _full((1, 1536)), _full((1, 1024)), _full((1, 1024)),
        _full(p["w_out"].shape),
    ]
    out_specs = [
        pl.BlockSpec((None, tm, d), lambda i, t: (i, t, 0)),
        pl.BlockSpec((None, n_pairs, 128, D_STATE), lambda i, t: (i, 0, 0, 0)),
        pl.BlockSpec((None, SUBLANES, 1536), lambda i, t: (i, 0, 0)),
    ]
    out_shape = [
        jax.ShapeDtypeStruct((b, s, d), F32),
        jax.ShapeDtypeStruct((b, n_pairs, 128, D_STATE), F32),
        jax.ShapeDtypeStruct((b, SUBLANES, 1536), F32),
    ]
    scratch = [
        pltpu.VMEM((n_pairs, 128, D_STATE), F32),
        pltpu.VMEM((SUBLANES, 1536), F32),
        pltpu.VMEM((tm, 2048), F32),
        pltpu.VMEM((tm, 1024), F32),
        pltpu.VMEM((tm, 1536), F32),
        pltpu.VMEM((tm, LANES), F32),
        pltpu.VMEM((LANES, tm), F32),
        pltpu.VMEM((tm, 2048), BF16),
    ]
    return pl.pallas_call(
        _mix_kernel,
        grid=(b, s // tm),
        in_specs=in_specs, out_specs=out_specs, out_shape=out_shape, scratch_shapes=scratch,
        compiler_params=_params(("parallel", "arbitrary")),
        name="mix_prompt",
    )(x, mod1, p["g_mix"], p["w_uv"], p["w_z"], p["w_xbc"], p["w_dt"], p["w_dtT"],
      p["dt_bias_row"], p["dt_bias_col"], p["a_log_row"], p["a_log_sq"], p["expand"],
      p["g_v"], p["w_s"], p["b_s_rep"], p["conv_w"], p["conv_b"], p["d_skip_exp"], p["g_ssm"], p["w_out"])


def _s1_kernel(x_ref, mod_ref, gmix_ref, wuv_ref, wz_ref, wxbc_ref, wdt_ref, dtb_ref, alog_ref, exp64_ref,
               gv_ref, ws0_ref, bs0_ref, cw_ref, cb_ref, sconv_ref,
               v_ref, cnew_ref, ya_ref, z_ref, xs_ref, bc_ref, xdtT_ref, dec_ref):
    x = x_ref[...]
    h = _rmsnorm(x, gmix_ref[...]) * (1.0 + mod_ref[1]) + mod_ref[0]
    hb = h.astype(BF16)
    uv = _dot(hb, wuv_ref[...])
    gv = gv_ref[...]
    for g in range(A_HEADS):
        sl = slice(g * A_HEAD_DIM, (g + 1) * A_HEAD_DIM)
        ug = _gelu(uv[:, sl])
        vn = _rmsnorm(_gelu(uv[:, 1024 + g * A_HEAD_DIM:1024 + (g + 1) * A_HEAD_DIM]), gv[:, sl])
        v_ref[:, sl] = vn
        s = ws0_ref[:, sl] * vn + bs0_ref[:, sl]
        ya_ref[:, sl] = (ug * s).astype(BF16)
    z_ref[...] = _dot(hb, wz_ref[...])
    raw = _dot(hb, wxbc_ref[...])
    cd = raw.shape[1]
    acc = cb_ref[...] + raw * cw_ref[CONV_W - 1:CONV_W, :]
    for k in range(CONV_W - 1):
        acc = acc + sconv_ref[:, k * cd:(k + 1) * cd] * cw_ref[k:k + 1, :]
    cnew_ref[:, 0:(CONV_W - 2) * cd] = sconv_ref[:, cd:(CONV_W - 1) * cd]
    cnew_ref[:, (CONV_W - 2) * cd:(CONV_W - 1) * cd] = raw
    xbc = _silu(acc)
    xs = xbc[:, 0:1024]
    xs_ref[...] = xs
    bc_ref[...] = xbc[:, 1024:1536]
    dt = _softplus(_dot(hb, wdt_ref[...]) + dtb_ref[...])
    dec_ref[...] = jnp.exp(dt * (-jnp.exp(alog_ref[...])))
    f1, f2, f3 = _split3(dt)
    dt64 = _dot(f1, exp64_ref[...]) + (_dot(f2, exp64_ref[...]) + _dot(f3, exp64_ref[...]))
    xdtT_ref[...] = (xs * dt64).T.astype(BF16)


def _s1_call(x, mod1, sconv, p):
    n, d = x.shape
    outs = [
        jax.ShapeDtypeStruct((n, 1024), F32),
        jax.ShapeDtypeStruct((n, 3 * 1536), F32),
        jax.ShapeDtypeStruct((n, 1024), BF16),
        jax.ShapeDtypeStruct((n, 1024), F32),
        jax.ShapeDtypeStruct((n, 1024), F32),
        jax.ShapeDtypeStruct((n, 512), F32),
        jax.ShapeDtypeStruct((1024, n), BF16),
        jax.ShapeDtypeStruct((n, LANES), F32),
    ]
    args = (x, mod1, p["g_mix"], p["w_uv"], p["w_z"], p["w_xbc"], p["w_dt"], p["dt_bias_row"], p["a_log_row"],
            p["expand64"], p["g_v"], p["w_s00"], p["b_s0"], p["conv_w"], p["conv_b"], sconv)
    return pl.pallas_call(
        _s1_kernel,
        grid=(1,),
        in_specs=[_full(a.shape) for a in args],
        out_specs=[_full(o.shape) for o in outs],
        out_shape=outs,
        compiler_params=_params(("arbitrary",)),
        name="mix_sample_in",
    )(*args)


def _s2_kernel(dec_ref, st_ref, xdtT_ref, bc_ref, cblk_ref, o_ref, y_ref):
    i = pl.program_id(0)
    bt = st_ref.shape[0]
    n = bc_ref.shape[0]
    half = B_HEADS * B_HEAD_DIM // B_GROUPS
    rowi = lax.broadcasted_iota(jnp.int32, (n, 1), 0)
    for bb in range(bt):
        b = i * bt + bb
        for g in range(B_GROUPS):
            bm = bc_ref[:, g * D_STATE:(g + 1) * D_STATE]
            rb = jnp.where(rowi == b, bm, 0.0).astype(BF16)
            outer = _dot(xdtT_ref[g * half:(g + 1) * half, :], rb)
            for k in range(B_HEADS // B_GROUPS):
                hd = g * (B_HEADS // B_GROUPS) + k
                sl = slice(k * B_HEAD_DIM, (k + 1) * B_HEAD_DIM)
                o_ref[bb, g, sl, :] = st_ref[bb, g, sl, :] * dec_ref[b, hd] + outer[sl, :]
            crow = cblk_ref[bb:bb + 1, 256 + g * D_STATE:256 + (g + 1) * D_STATE]
            c8 = jnp.broadcast_to(crow, (SUBLANES, D_STATE)).astype(BF16)
            yr = _dot_nt(c8, o_ref[bb, g].astype(BF16))
            y_ref[bb:bb + 1, g * half:(g + 1) * half] = yr[0:1]


def _s2_call(dec, state, xdtT, bc, bt=S2_TOKENS):
    n = state.shape[0]
    half = B_HEADS * B_HEAD_DIM // B_GROUPS
    st = state.reshape(n, B_GROUPS, half, D_STATE)
    grid_spec = pltpu.PrefetchScalarGridSpec(
        num_scalar_prefetch=1,
        grid=(n // bt,),
        in_specs=[
            pl.BlockSpec((bt, B_GROUPS, half, D_STATE), lambda i, d: (i, 0, 0, 0)),
            pl.BlockSpec(xdtT.shape, lambda i, d: (0, 0)),
            pl.BlockSpec(bc.shape, lambda i, d: (0, 0)),
            pl.BlockSpec((bt, bc.shape[1]), lambda i, d: (i, 0)),
        ],
        out_specs=[
            pl.BlockSpec((bt, B_GROUPS, half, D_STATE), lambda i, d: (i, 0, 0, 0)),
            pl.BlockSpec((bt, 1024), lambda i, d: (i, 0)),
        ],
    )
    new_state, y = pl.pallas_call(
        _s2_kernel,
        grid_spec=grid_spec,
        out_shape=[jax.ShapeDtypeStruct(st.shape, F32), jax.ShapeDtypeStruct((n, 1024), F32)],
        compiler_params=_params(("arbitrary",)),
        name="mix_sample_state",
    )(dec, st, xdtT, bc, bc)
    return new_state.reshape(state.shape), y


def _s3_kernel(x_ref, mod_ref, y_ref, xs_ref, z_ref, ya_ref, dsk_ref, gssm_ref, wout_ref, o_ref):
    y = (y_ref[...] + dsk_ref[...] * xs_ref[...]) * _silu(z_ref[...])
    half = B_HEADS * B_HEAD_DIM // B_GROUPS
    parts = [ya_ref[...]]
    for g in range(B_GROUPS):
        sl = slice(g * half, (g + 1) * half)
        parts.append(_rmsnorm(y[:, sl], gssm_ref[:, sl]).astype(BF16))
    cat = jnp.concatenate(parts, axis=1)
    o_ref[...] = x_ref[...] + mod_ref[2] * _dot(cat, wout_ref[...])


def _s3_call(x, mod1, y, xs, z, ya, p):
    args = (x, mod1, y, xs, z, ya, p["d_skip_exp"], p["g_ssm"], p["w_out"])
    return pl.pallas_call(
        _s3_kernel,
        grid=(1,),
        in_specs=[_full(a.shape) for a in args],
        out_specs=_full(x.shape),
        out_shape=jax.ShapeDtypeStruct(x.shape, F32),
        compiler_params=_params(("arbitrary",)),
        name="mix_sample_out",
    )(*args)


def _cand_blocks(a, b):
    row8 = lax.broadcasted_iota(jnp.int32, (SUBLANES, 1), 0)
    a8 = a[0:SUBLANES]
    blocks = []
    for j in range(SUBLANES):
        blk = a8 + b[j:j + 1]
        cnt = min(SUBLANES, TOPK // (j + 1))
        if cnt < SUBLANES:
            blk = jnp.where(row8 < cnt, blk, NEG_INF)
        blocks.append(blk)
    blocks.append(a[0:1] + b[SUBLANES:TOPK])
    blocks.append(a[SUBLANES:TOPK] + b[0:1])
    return jnp.concatenate(blocks, axis=0)


def _kth_largest(v, k):
    m = None
    for _ in range(k):
        m = jnp.max(v, axis=0, keepdims=True)
        v = jnp.where(v == m, NEG_INF, v)
    return m


def _peer_kernel(x_ref, mod_ref, gffn_ref, gfin_ref, wqT_ref, keys_ref, u_ref, vt_ref, y_ref,
                 h2_ref, q_ref, s1_ref, s2_ref, thr_ref, top_ref, act_ref, coef_ref, outT_ref):
    c = pl.program_id(1)
    n_blk = u_ref.shape[0] // N_KEYS

    @pl.when(c == 0)
    def _():
        x = x_ref[...]
        h = _rmsnorm(x, gffn_ref[...]) * (1.0 + mod_ref[1]) + mod_ref[0]
        hb = h.astype(BF16)
        h2_ref[...] = hb
        q_ref[...] = _dot_nt(wqT_ref[...], hb)

        def head_body(hh, carry):
            for s in range(2):
                r0 = pl.multiple_of((2 * hh + s) * N_KEYS, N_KEYS)
                qs = q_ref[pl.ds(r0, N_KEYS), :].astype(BF16)
                sc = _dot(keys_ref[2 * hh + s], qs)
                if s == 0:
                    s1_ref[hh] = sc
                else:
                    s2_ref[hh] = sc
                v = sc
                for k in range(TOPK):
                    m = jnp.max(v, axis=0, keepdims=True)
                    top_ref[s, k:k + 1, :] = m
                    v = jnp.where(v == m, NEG_INF, v)
            a = top_ref[0]
            b = top_ref[1]
            best = a[0:1] + b[0:1]
            cand = _cand_blocks(a, b)
            thr0 = _kth_largest(cand, TOPK)
            zsum = jnp.sum(jnp.where(cand >= thr0, jnp.exp(cand - best), 0.0), axis=0, keepdims=True)
            shift = best + jnp.log(zsum)
            s1_ref[hh] = s1_ref[hh] - shift
            thr_ref[hh] = _kth_largest(_cand_blocks(a - shift, b), TOPK)
            return carry

        lax.fori_loop(0, R_HEADS, head_body, 0)

    act_ref[...] = _dot_nt(u_ref[...], h2_ref[...])
    blk0 = c * n_blk

    def grp_body(gi, carry):
        i1 = pl.multiple_of(blk0 + gi * SUBLANES, SUBLANES)
        s1_rows = [s1_ref[hh, pl.ds(i1, SUBLANES), :] for hh in range(R_HEADS)]
        for r in range(SUBLANES):
            acc = None
            for hh in range(R_HEADS):
                val = s1_rows[hh][r:r + 1, :] + s2_ref[hh]
                gate = jnp.where(val >= thr_ref[hh], jnp.exp(val), 0.0)
                acc = gate if acc is None else acc + gate
            r0 = pl.multiple_of((gi * SUBLANES + r) * N_KEYS, N_KEYS)
            coef_ref[pl.ds(r0, N_KEYS), :] = (acc * _gelu(act_ref[pl.ds(r0, N_KEYS), :])).astype(BF16)
        return carry

    lax.fori_loop(0, n_blk // SUBLANES, grp_body, 0)
    contrib = _dot(vt_ref[...], coef_ref[...])

    @pl.when(c == 0)
    def _():
        outT_ref[...] = contrib

    @pl.when(c > 0)
    def _():
        outT_ref[...] += contrib

    @pl.when(c == pl.num_programs(1) - 1)
    def _():
        x2 = x_ref[...] + mod_ref[2] * outT_ref[...].T
        y_ref[...] = _rmsnorm(x2, gfin_ref[...]) * (1.0 + mod_ref[4]) + mod_ref[3]


def _peer_call(x, mod5, tiles_per_row, p, tl, ec=PEER_EXPERTS):
    t, d = x.shape
    n_exp = p["expert_u"].shape[0]
    r = mod5.shape[2]
    scratch = [
        pltpu.VMEM((tl, d), BF16),
        pltpu.VMEM((R_HEADS * 2 * N_KEYS, tl), F32),
        pltpu.VMEM((R_HEADS, N_KEYS, tl), F32),
        pltpu.VMEM((R_HEADS, N_KEYS, tl), F32),
        pltpu.VMEM((R_HEADS, 1, tl), F32),
        pltpu.VMEM((2, TOPK, tl), F32),
        pltpu.VMEM((ec, tl), F32),
        pltpu.VMEM((ec, tl), BF16),
        pltpu.VMEM((d, tl), F32),
    ]
    return pl.pallas_call(
        _peer_kernel,
        grid=(t // tl, n_exp // ec),
        in_specs=[
            pl.BlockSpec((tl, d), lambda i, c: (i, 0)),
            pl.BlockSpec((None, 5, r, d), lambda i, c: (i // tiles_per_row, 0, 0, 0)),
            _full((1, d)), _full((1, d)),
            _full(p["w_qT"].shape), _full(p["keys"].shape),
            pl.BlockSpec((ec, d), lambda i, c: (c, 0)),
            pl.BlockSpec((d, ec), lambda i, c: (0, c)),
        ],
        out_specs=pl.BlockSpec((tl, d), lambda i, c: (i, 0)),
        out_shape=jax.ShapeDtypeStruct((t, d), F32),
        scratch_shapes=scratch,
        compiler_params=_params(("parallel", "arbitrary")),
        name="peer",
    )(x, mod5, p["g_ffn"], p["g_final"], p["w_qT"], p["keys"], p["expert_u"], p["expert_vT"])


def _prep(w_in, g_mix, g_v, w_s, b_s, conv_w, conv_b, dt_bias, a_log, d_skip, g_ssm, w_out, g_ffn, w_q,
          sub_keys, expert_u, expert_v, g_final):
    d = w_in.shape[0]
    o_z = 2 * 1024
    o_x = o_z + 1024
    o_dt = o_x + 1536
    w_dt = jnp.zeros((d, LANES), F32).at[:, :B_HEADS].set(w_in[:, o_dt:o_dt + B_HEADS])
    pad16 = lambda v: jnp.zeros((LANES,), F32).at[:B_HEADS].set(v)
    heads = jnp.arange(LANES)[:, None]
    p = {
        "g_mix": g_mix.reshape(1, d),
        "w_uv": w_in[:, 0:o_z].astype(BF16),
        "w_z": w_in[:, o_z:o_x].astype(BF16),
        "w_xbc": w_in[:, o_x:o_dt].astype(BF16),
        "w_dt": w_dt.astype(BF16),
        "w_dtT": w_dt.T.astype(BF16),
        "dt_bias_row": pad16(dt_bias).reshape(1, LANES),
        "dt_bias_col": pad16(dt_bias).reshape(LANES, 1),
        "a_log_row": pad16(a_log).reshape(1, LANES),
        "a_log_sq": jnp.broadcast_to(pad16(a_log).reshape(LANES, 1), (LANES, LANES)),
        "expand": (heads == jnp.arange(B_HEADS * LANES)[None, :] // LANES).astype(BF16),
        "expand64": (heads == jnp.arange(B_HEADS * B_HEAD_DIM)[None, :] // B_HEAD_DIM).astype(BF16),
        "g_v": g_v.reshape(1, -1),
        "w_s": w_s,
        "b_s_rep": jnp.broadcast_to(b_s[:, :, None], (A_HEADS, CHUNK, CHUNK)),
        "w_s00": jnp.repeat(w_s[:, 0, 0], A_HEAD_DIM).reshape(1, -1),
        "b_s0": jnp.repeat(b_s[:, 0], A_HEAD_DIM).reshape(1, -1),
        "conv_w": conv_w,
        "conv_b": conv_b.reshape(1, -1),
        "d_skip_exp": jnp.repeat(d_skip, B_HEAD_DIM).reshape(1, -1),
        "g_ssm": g_ssm.reshape(1, -1),
        "w_out": w_out.astype(BF16),
        "g_ffn": g_ffn.reshape(1, d),
        "g_final": g_final.reshape(1, d),
        "w_qT": w_q.T.astype(BF16),
        "keys": sub_keys.reshape(R_HEADS * 2, N_KEYS, -1).astype(BF16),
        "expert_u": expert_u.astype(BF16),
        "expert_vT": expert_v.T.astype(BF16),
    }
    return p


def kernel(x_prompt, x_sample, c_prompt, c_sample, state_ssm, state_conv, w_ada, b_ada, g_mix, w_in, g_v, w_s, b_s,
           conv_w, conv_b, dt_bias, a_log, d_skip, g_ssm, w_out, g_ffn, w_q, sub_keys, expert_u, expert_v,
           w_ada_f, b_ada_f, g_final):
    assert w_ada.shape[0] == 1, "single-layer trunk"
    bp, seq, d = x_prompt.shape
    ns = x_sample.shape[0]
    p = _prep(w_in[0], g_mix[0], g_v[0], w_s[0], b_s[0], conv_w[0], conv_b[0], dt_bias[0], a_log[0], d_skip[0],
              g_ssm[0], w_out[0], g_ffn[0], w_q[0], sub_keys[0], expert_u[0], expert_v[0], g_final)

    c_all = jnp.concatenate([c_prompt, c_sample], axis=0)
    mod = _ada_call(c_all, w_ada[0], b_ada[0]).reshape(bp + ns, N_MOD, d)
    modf = _ada_call(c_all, w_ada_f, b_ada_f).reshape(bp + ns, 2, d)
    mod5 = jnp.concatenate([mod[:, 3:6], modf], axis=1)

    x1_p, ssm_p, ctail_p = _mix_call(x_prompt, mod[:bp, 0:3].reshape(bp, 3, 1, d), p)
    y_p = _peer_call(x1_p.reshape(bp * seq, d), mod5[:bp].reshape(bp, 5, 1, d), seq // PEER_TOKENS, p, PEER_TOKENS)

    xs_in = x_sample.reshape(ns, d)
    mod1_s = jnp.transpose(mod[bp:, 0:3], (1, 0, 2))
    sconv = state_conv[0].reshape(ns, -1)
    v_s, cnew_s, ya_s, z_s, xc_s, bc_s, xdtT_s, dec_s = _s1_call(xs_in, mod1_s, sconv, p)
    ssm_s, yssd_s = _s2_call(dec_s[:, :B_HEADS], state_ssm[0], xdtT_s, bc_s)
    x1_s = _s3_call(xs_in, mod1_s, yssd_s, xc_s, z_s, ya_s, p)
    mod5_s = jnp.transpose(mod5[bp:], (1, 0, 2)).reshape(1, 5, ns, d)
    y_s = _peer_call(x1_s, mod5_s, 1, p, ns)

    return (
        y_p.reshape(bp, seq, d),
        y_s.reshape(ns, 1, d),
        ssm_p.reshape(1, bp, B_HEADS, B_HEAD_DIM, D_STATE),
        ctail_p[:, SUBLANES - (CONV_W - 1):, :].reshape(1, bp, CONV_W - 1, -1),
        ssm_s.reshape(1, ns, B_HEADS, B_HEAD_DIM, D_STATE),
        cnew_s.reshape(1, ns, CONV_W - 1, -1),
        v_s.reshape(1, ns, 1, -1),
    )
```

```python
import functools
import math

import jax
import jax.numpy as jnp
from jax import lax
from jax.experimental import pallas as pl
from jax.experimental.pallas import tpu as pltpu

F32 = jnp.float32
BF16 = jnp.bfloat16
NEG_INF = float("-inf")

EPS = 1e-6
A_HEADS = 8
A_HEAD_DIM = 128
CHUNK = 128
B_HEADS = 16
B_HEAD_DIM = 64
B_GROUPS = 2
D_STATE = 128
CONV_W = 4
N_KEYS = 128
R_HEADS = 8
TOPK = 16
N_MOD = 6

LANES = 128
SUBLANES = 8
VMEM_LIMIT = 56 * 1024 * 1024

MIX_ROWS = 256
PEER_TOKENS = 256
PEER_EXPERTS = 2048
LOG2E = 1.4426950408889634
S2_TOKENS = 8


def _dot(a, b):
    return jnp.dot(a, b, preferred_element_type=F32)


def _dot_nt(a, b):
    return lax.dot_general(a, b, (((1,), (1,)), ((), ())), preferred_element_type=F32)


def _dot_tn(a, b):
    return lax.dot_general(a, b, (((0,), (0,)), ((), ())), preferred_element_type=F32)


def _split2(x):
    hi = x.astype(BF16)
    lo = (x - hi.astype(F32)).astype(BF16)
    return hi, lo


def _split3(x):
    hi = x.astype(BF16)
    r = x - hi.astype(F32)
    mid = r.astype(BF16)
    lo = (r - mid.astype(F32)).astype(BF16)
    return hi, mid, lo


def _dot_x3(a, b):
    a1, a2 = _split2(a)
    b1, b2 = _split2(b)
    return _dot(a1, b1) + (_dot(a1, b2) + _dot(a2, b1))


def _silu(x):
    return x / (1.0 + jnp.exp(-x))


def _gelu(x):
    return 0.5 * x * (1.0 + lax.erf(x * (1.0 / math.sqrt(2.0))))


def _softplus(x):
    return jnp.maximum(x, 0.0) + jnp.log1p(jnp.exp(-jnp.abs(x)))


def _rmsnorm(x, g):
    return x * lax.rsqrt(jnp.mean(x * x, axis=-1, keepdims=True) + EPS) * g


def _full(shape):
    nd = len(shape)
    return pl.BlockSpec(shape, lambda *_: (0,) * nd)


def _params(sem):
    return pltpu.CompilerParams(dimension_semantics=sem, vmem_limit_bytes=VMEM_LIMIT)


def _ada_kernel(c_ref, w_ref, b_ref, o_ref):
    o_ref[...] = _dot_x3(_silu(c_ref[...]), w_ref[...]) + b_ref[...]


def _ada_call(c, w, b, tn=512):
    m, k = c.shape
    n = w.shape[1]
    return pl.pallas_call(
        _ada_kernel,
        grid=(n // tn,),
        in_specs=[_full((m, k)), pl.BlockSpec((k, tn), lambda j: (0, j)), pl.BlockSpec((1, tn), lambda j: (0, j))],
        out_specs=pl.BlockSpec((m, tn), lambda j: (0, j)),
        out_shape=jax.ShapeDtypeStruct((m, n), F32),
        compiler_params=_params(("parallel",)),
        name="ada",
    )(c, w, b.reshape(1, n))


def _gate_mlp_chunk(u, v, gv, ws_ref, bs_ref, causal):
    ys, vs = [], []
    for g in range(A_HEADS):
        sl = slice(g * A_HEAD_DIM, (g + 1) * A_HEAD_DIM)
        ug = _gelu(u[:, sl])
        vn = _rmsnorm(_gelu(v[:, sl]), gv[:, sl])
        w = jnp.where(causal, ws_ref[g], 0.0).astype(BF16)
        s = _dot(w, vn.astype(BF16)) + bs_ref[g]
        ys.append((ug * s).astype(BF16))
        vs.append(vn)
    return ys, vs


def _mix_kernel(x_ref, mod_ref, gmix_ref, wuv_ref, wz_ref, wxbc_ref, wdt_ref, wdtT_ref,
                dtb_ref, dtbT_ref, alog_ref, alogT_ref, exp_ref, gv_ref, ws_ref, bs_ref,
                cw_ref, cb_ref, dsk_ref, gssm_ref, wout_ref,
                o_ref, ssm_ref, ctail_ref,
                hst_ref, tail_ref, uv_ref, z_ref, xs_ref, dt_ref, dtT_ref, ycat_ref):
    t = pl.program_id(1)
    tm = x_ref.shape[0]
    n_chunks = tm // CHUNK

    @pl.when(t == 0)
    def _():
        hst_ref[...] = jnp.zeros_like(hst_ref)
        tail_ref[...] = jnp.zeros_like(tail_ref)

    x = x_ref[...]
    h = _rmsnorm(x, gmix_ref[...]) * (1.0 + mod_ref[1]) + mod_ref[0]
    hb = h.astype(BF16)
    uv_ref[...] = _dot(hb, wuv_ref[...])
    z_ref[...] = _dot(hb, wz_ref[...])
    dt_ref[...] = _softplus(_dot(hb, wdt_ref[...]) + dtb_ref[...])
    dtT_ref[...] = _softplus(_dot_nt(wdtT_ref[...], hb) + dtbT_ref[...])

    cur = _dot(hb, wxbc_ref[...])
    prev = tail_ref[...]
    row8 = lax.broadcasted_iota(jnp.int32, (SUBLANES, 1), 0)
    acc = cb_ref[...] + cur * cw_ref[CONV_W - 1:CONV_W, :]
    for k in range(1, CONV_W):
        r = pltpu.roll(cur, k, axis=0)
        head = jnp.where(row8 < k, pltpu.roll(prev, k, axis=0), r[0:SUBLANES])
        shifted = jnp.concatenate([head, r[SUBLANES:]], axis=0)
        acc = acc + shifted * cw_ref[CONV_W - 1 - k:CONV_W - k, :]
    xs_ref[...] = _silu(acc)
    new_tail = cur[tm - SUBLANES:tm]
    tail_ref[...] = new_tail
    ctail_ref[...] = new_tail

    ri = lax.broadcasted_iota(jnp.int32, (CHUNK, CHUNK), 0)
    ci = lax.broadcasted_iota(jnp.int32, (CHUNK, CHUNK), 1)
    causal = ri >= ci
    tril = jnp.where(causal, 1.0, 0.0).astype(BF16)
    triu = jnp.where(ri <= ci, 1.0, 0.0).astype(BF16)
    low_half = ri < B_HEAD_DIM
    low_lane = ci < B_HEAD_DIM
    a_col = -jnp.exp(alog_ref[...])
    a_row = -jnp.exp(alogT_ref[...])

    def chunk_body(c, carry):
        r0 = pl.multiple_of(c * CHUNK, CHUNK)
        rows = pl.ds(r0, CHUNK)

        ys, _ = _gate_mlp_chunk(uv_ref[rows, 0:1024], uv_ref[rows, 1024:2048], gv_ref[...], ws_ref, bs_ref, causal)
        for g in range(A_HEADS):
            ycat_ref[rows, g * A_HEAD_DIM:(g + 1) * A_HEAD_DIM] = ys[g]

        dt = dt_ref[rows, :]
        d_a = dt * a_col
        h1, h2, h3 = _split3(d_a)
        acum = _dot(tril, h1) + (_dot(tril, h2) + _dot(tril, h3))
        e1, e2, e3 = _split3(acum)
        acum_bc = _dot(e1, exp_ref[...]) + (_dot(e2, exp_ref[...]) + _dot(e3, exp_ref[...]))
        f1, f2 = _split2(dt)
        dt_bc = _dot(f1, exp_ref[...]) + _dot(f2, exp_ref[...])
        dt_t = dtT_ref[:, rows]
        g1, g2, g3 = _split3(dt_t * a_row)
        acum_t = _dot(g1, triu) + (_dot(g2, triu) + _dot(g3, triu))

        xs = xs_ref[rows, 0:1024]
        zz = z_ref[rows, :]
        y_pairs = []
        for g in range(B_GROUPS):
            bg = xs_ref[rows, 1024 + g * D_STATE:1024 + (g + 1) * D_STATE]
            cg = xs_ref[rows, 1280 + g * D_STATE:1280 + (g + 1) * D_STATE]
            cbm = _dot_nt(cg.astype(BF16), bg.astype(BF16))
            for pr in range(B_HEADS // B_GROUPS // 2):
                pair = g * 4 + pr
                xpair = xs[:, pair * 128:(pair + 1) * 128]
                hprev = hst_ref[pair]
                y_acc = None
                st_acc = None
                alasts = []
                for e in range(2):
                    hd = pair * 2 + e
                    ab = acum_bc[:, hd * 128:(hd + 1) * 128]
                    seg = ab - acum_t[hd:hd + 1, :]
                    lm = jnp.exp(jnp.where(causal, seg, NEG_INF))
                    gd = (cbm * lm * dt_t[hd:hd + 1, :]).astype(BF16)
                    sel_l = low_lane if e == 0 else jnp.logical_not(low_lane)
                    sel_r = low_half if e == 0 else jnp.logical_not(low_half)
                    xh = jnp.where(sel_l, xpair, 0.0).astype(BF16)
                    hp = jnp.where(sel_r, hprev, 0.0).astype(BF16)
                    cs = (cg * jnp.exp(ab)).astype(BF16)
                    alast = ab[CHUNK - 1:CHUNK, :]
                    bsc = (bg * (jnp.exp(alast - ab) * dt_bc[:, hd * 128:(hd + 1) * 128])).astype(BF16)
                    y_e = _dot(gd, xh) + _dot_nt(cs, hp)
                    st_e = _dot_tn(xh, bsc)
                    y_acc = y_e if y_acc is None else y_acc + y_e
                    st_acc = st_e if st_acc is None else st_acc + st_e
                    alasts.append(alast)
                decay = jnp.exp(jnp.where(low_half, alasts[0], alasts[1]))
                hst_ref[pair] = hprev * decay + st_acc
                y_pairs.append(y_acc)
        y = jnp.concatenate(y_pairs, axis=1) + dsk_ref[...] * xs
        y = y * _silu(zz)
        half = B_HEADS * B_HEAD_DIM // B_GROUPS
        for g in range(B_GROUPS):
            sl = slice(g * half, (g + 1) * half)
            ycat_ref[rows, 1024 + g * half:1024 + (g + 1) * half] = _rmsnorm(y[:, sl], gssm_ref[:, sl]).astype(BF16)
        return carry

    lax.fori_loop(0, n_chunks, chunk_body, 0)

    o_ref[...] = x + mod_ref[2] * _dot(ycat_ref[...], wout_ref[...])

    @pl.when(t == pl.num_programs(1) - 1)
    def _():
        ssm_ref[...] = hst_ref[...]


def _mix_call(x, mod1, p, tm=MIX_ROWS):
    b, s, d = x.shape
    n_pairs = B_HEADS // 2
    in_specs = [
        pl.BlockSpec((None, tm, d), lambda i, t: (i, t, 0)),
        pl.BlockSpec((None, 3, 1, d), lambda i, t: (i, 0, 0, 0)),
        _full((1, d)),
        _full(p["w_uv"].shape), _full(p["w_z"].shape), _full(p["w_xbc"].shape),
        _full(p["w_dt"].shape), _full(p["w_dtT"].shape),
        _full((1, LANES)), _full((LANES, 1)), _full((1, LANES)), _full((LANES, LANES)),
        _full(p["expand"].shape),
        _full((1, 1024)), _full((A_HEADS, CHUNK, CHUNK)), _full((A_HEADS, CHUNK, CHUNK)),
        _full((CONV_W, 1536)), _full((1, 1536)), _full((1, 1024)), _full((1, 1024)),
        _full(p["w_out"].shape),
    ]
    out_specs = [
        pl.BlockSpec((None, tm, d), lambda i, t: (i, t, 0)),
        pl.BlockSpec((None, n_pairs, 128, D_STATE), lambda i, t: (i, 0, 0, 0)),
        pl.BlockSpec((None, SUBLANES, 1536), lambda i, t: (i, 0, 0)),
    ]
    out_shape = [
        jax.ShapeDtypeStruct((b, s, d), F32),
        jax.ShapeDtypeStruct((b, n_pairs, 128, D_STATE), F32),
        jax.ShapeDtypeStruct((b, SUBLANES, 1536), F32),
    ]
    scratch = [
        pltpu.VMEM((n_pairs, 128, D_STATE), F32),
        pltpu.VMEM((SUBLANES, 1536), F32),
        pltpu.VMEM((tm, 2048), F32),
        pltpu.VMEM((tm, 1024), F32),
        pltpu.VMEM((tm, 1536), F32),
        pltpu.VMEM((tm, LANES), F32),
        pltpu.VMEM((LANES, tm), F32),
        pltpu.VMEM((tm, 2048), BF16),
    ]
    return pl.pallas_call(
        _mix_kernel,
        grid=(b, s // tm),
        in_specs=in_specs, out_specs=out_specs, out_shape=out_shape, scratch_shapes=scratch,
        compiler_params=_params(("parallel", "arbitrary")),
        name="mix_prompt",
    )(x, mod1, p["g_mix"], p["w_uv"], p["w_z"], p["w_xbc"], p["w_dt"], p["w_dtT"],
      p["dt_bias_row"], p["dt_bias_col"], p["a_log_row"], p["a_log_sq"], p["expand"],
      p["g_v"], p["w_s"], p["b_s_rep"], p["conv_w"], p["conv_b"], p["d_skip_exp"], p["g_ssm"], p["w_out"])


def _s1_kernel(x_ref, mod_ref, gmix_ref, wuv_ref, wz_ref, wxbc_ref, wdt_ref, dtb_ref, alog_ref, exp64_ref,
               gv_ref, ws0_ref, bs0_ref, cw_ref, cb_ref, sconv_ref,
               v_ref, cnew_ref, ya_ref, z_ref, xs_ref, bc_ref, xdtT_ref, dec_ref):
    x = x_ref[...]
    h = _rmsnorm(x, gmix_ref[...]) * (1.0 + mod_ref[1]) + mod_ref[0]
    hb = h.astype(BF16)
    uv = _dot(hb, wuv_ref[...])
    gv = gv_ref[...]
    for g in range(A_HEADS):
        sl = slice(g * A_HEAD_DIM, (g + 1) * A_HEAD_DIM)
        ug = _gelu(uv[:, sl])
        vn = _rmsnorm(_gelu(uv[:, 1024 + g * A_HEAD_DIM:1024 + (g + 1) * A_HEAD_DIM]), gv[:, sl])
        v_ref[:, sl] = vn
        s = ws0_ref[:, sl] * vn + bs0_ref[:, sl]
        ya_ref[:, sl] = (ug * s).astype(BF16)
    z_ref[...] = _dot(hb, wz_ref[...])
    raw = _dot(hb, wxbc_ref[...])
    cd = raw.shape[1]
    acc = cb_ref[...] + raw * cw_ref[CONV_W - 1:CONV_W, :]
    for k in range(CONV_W - 1):
        acc = acc + sconv_ref[:, k * cd:(k + 1) * cd] * cw_ref[k:k + 1, :]
    cnew_ref[:, 0:(CONV_W - 2) * cd] = sconv_ref[:, cd:(CONV_W - 1) * cd]
    cnew_ref[:, (CONV_W - 2) * cd:(CONV_W - 1) * cd] = raw
    xbc = _silu(acc)
    xs = xbc[:, 0:1024]
    xs_ref[...] = xs
    bc_ref[...] = xbc[:, 1024:1536]
    dt = _softplus(_dot(hb, wdt_ref[...]) + dtb_ref[...])
    dec_ref[...] = jnp.exp(dt * (-jnp.exp(alog_ref[...])))
    f1, f2, f3 = _split3(dt)
    dt64 = _dot(f1, exp64_ref[...]) + (_dot(f2, exp64_ref[...]) + _dot(f3, exp64_ref[...]))
    xdtT_ref[...] = (xs * dt64).T.astype(BF16)


def _s1_call(x, mod1, sconv, p):
    n, d = x.shape
    outs = [
        jax.ShapeDtypeStruct((n, 1024), F32),
        jax.ShapeDtypeStruct((n, 3 * 1536), F32),
        jax.ShapeDtypeStruct((n, 1024), BF16),
        jax.ShapeDtypeStruct((n, 1024), F32),
        jax.ShapeDtypeStruct((n, 1024), F32),
        jax.ShapeDtypeStruct((n, 512), F32),
        jax.ShapeDtypeStruct((1024, n), BF16),
        jax.ShapeDtypeStruct((n, LANES), F32),
    ]
    args = (x, mod1, p["g_mix"], p["w_uv"], p["w_z"], p["w_xbc"], p["w_dt"], p["dt_bias_row"], p["a_log_row"],
            p["expand64"], p["g_v"], p["w_s00"], p["b_s0"], p["conv_w"], p["conv_b"], sconv)
    return pl.pallas_call(
        _s1_kernel,
        grid=(1,),
        in_specs=[_full(a.shape) for a in args],
        out_specs=[_full(o.shape) for o in outs],
        out_shape=outs,
        compiler_params=_params(("arbitrary",)),
        name="mix_sample_in",
    )(*args)


def _s2_kernel(dec_ref, st_ref, xdtT_ref, bc_ref, cblk_ref, o_ref, y_ref):
    i = pl.program_id(0)
    bt = st_ref.shape[0]
    n = bc_ref.shape[0]
    half = B_HEADS * B_HEAD_DIM // B_GROUPS
    rowi = lax.broadcasted_iota(jnp.int32, (n, 1), 0)
    for bb in range(bt):
        b = i * bt + bb
        for g in range(B_GROUPS):
            bm = bc_ref[:, g * D_STATE:(g + 1) * D_STATE]
            rb = jnp.where(rowi == b, bm, 0.0).astype(BF16)
            outer = _dot(xdtT_ref[g * half:(g + 1) * half, :], rb)
            for k in range(B_HEADS // B_GROUPS):
                hd = g * (B_HEADS // B_GROUPS) + k
                sl = slice(k * B_HEAD_DIM, (k + 1) * B_HEAD_DIM)
                o_ref[bb, g, sl, :] = st_ref[bb, g, sl, :] * dec_ref[b, hd] + outer[sl, :]
            crow = cblk_ref[bb:bb + 1, 256 + g * D_STATE:256 + (g + 1) * D_STATE]
            c8 = jnp.broadcast_to(crow, (SUBLANES, D_STATE)).astype(BF16)
            yr = _dot_nt(c8, o_ref[bb, g].astype(BF16))
            y_ref[bb:bb + 1, g * half:(g + 1) * half] = yr[0:1]


def _s2_call(dec, state, xdtT, bc, bt=S2_TOKENS):
    n = state.shape[0]
    half = B_HEADS * B_HEAD_DIM // B_GROUPS
    st = state.reshape(n, B_GROUPS, half, D_STATE)
    grid_spec = pltpu.PrefetchScalarGridSpec(
        num_scalar_prefetch=1,
        grid=(n // bt,),
        in_specs=[
            pl.BlockSpec((bt, B_GROUPS, half, D_STATE), lambda i, d: (i, 0, 0, 0)),
            pl.BlockSpec(xdtT.shape, lambda i, d: (0, 0)),
            pl.BlockSpec(bc.shape, lambda i, d: (0, 0)),
            pl.BlockSpec((bt, bc.shape[1]), lambda i, d: (i, 0)),
        ],
        out_specs=[
            pl.BlockSpec((bt, B_GROUPS, half, D_STATE), lambda i, d: (i, 0, 0, 0)),
            pl.BlockSpec((bt, 1024), lambda i, d: (i, 0)),
        ],
    )
    new_state, y = pl.pallas_call(
        _s2_kernel,
        grid_spec=grid_spec,
        out_shape=[jax.ShapeDtypeStruct(st.shape, F32), jax.ShapeDtypeStruct((n, 1024), F32)],
        compiler_params=_params(("arbitrary",)),
        name="mix_sample_state",
    )(dec, st, xdtT, bc, bc)
    return new_state.reshape(state.shape), y


def _s3_kernel(x_ref, mod_ref, y_ref, xs_ref, z_ref, ya_ref, dsk_ref, gssm_ref, wout_ref, o_ref):
    y = (y_ref[...] + dsk_ref[...] * xs_ref[...]) * _silu(z_ref[...])
    half = B_HEADS * B_HEAD_DIM // B_GROUPS
    parts = [ya_ref[...]]
    for g in range(B_GROUPS):
        sl = slice(g * half, (g + 1) * half)
        parts.append(_rmsnorm(y[:, sl], gssm_ref[:, sl]).astype(BF16))
    cat = jnp.concatenate(parts, axis=1)
    o_ref[...] = x_ref[...] + mod_ref[2] * _dot(cat, wout_ref[...])


def _s3_call(x, mod1, y, xs, z, ya, p):
    args = (x, mod1, y, xs, z, ya, p["d_skip_exp"], p["g_ssm"], p["w_out"])
    return pl.pallas_call(
        _s3_kernel,
        grid=(1,),
        in_specs=[_full(a.shape) for a in args],
        out_specs=_full(x.shape),
        out_shape=jax.ShapeDtypeStruct(x.shape, F32),
        compiler_params=_params(("arbitrary",)),
        name="mix_sample_out",
    )(*args)


def _cand_blocks(a, b):
    row8 = lax.broadcasted_iota(jnp.int32, (SUBLANES, 1), 0)
    a8 = a[0:SUBLANES]
    blocks = []
    for j in range(SUBLANES):
        blk = a8 + b[j:j + 1]
        cnt = min(SUBLANES, TOPK // (j + 1))
        if cnt < SUBLANES:
            blk = jnp.where(row8 < cnt, blk, NEG_INF)
        blocks.append(blk)
    blocks.append(a[0:1] + b[SUBLANES:TOPK])
    blocks.append(a[SUBLANES:TOPK] + b[0:1])
    return blocks


def _merge_exchange_pairs(n):
    pairs = []
    t = max(1, math.ceil(math.log2(n)))
    p = 1 << (t - 1)
    while p > 0:
        q, r, d = 1 << (t - 1), 0, p
        while d > 0:
            pairs.extend((i, i + d) for i in range(n - d) if (i & p) == r)
            d, q, r = q - p, q >> 1, p
        p >>= 1
    return pairs


def _sort_blocks_desc(blocks):
    blocks = list(blocks)
    for i, j in _merge_exchange_pairs(len(blocks)):
        hi = jnp.maximum(blocks[i], blocks[j])
        blocks[j] = jnp.minimum(blocks[i], blocks[j])
        blocks[i] = hi
    return blocks


def _pop_largest(blocks, k):
    blocks = list(blocks)
    nb = len(blocks)
    rows = []
    for t in range(k):
        m = jnp.max(blocks[0], axis=0, keepdims=True)
        rows.append(m)
        need = min(k - t - 1, nb)
        if need == 0:
            break
        sel = blocks[0] == m
        for j in range(need):
            nxt = blocks[j + 1] if j + 1 < nb else NEG_INF
            blocks[j] = jnp.where(sel, nxt, blocks[j])
    return rows


def _peer_kernel(x_ref, mod_ref, gffn_ref, gfin_ref, wqT_ref, keys_ref, u_ref, vt_ref, y_ref,
                 h2_ref, q_ref, s1_ref, s2_ref, thr_ref, top_ref, act_ref, coef_ref, outT_ref):
    c = pl.program_id(1)
    n_blk = u_ref.shape[0] // N_KEYS

    @pl.when(c == 0)
    def _():
        x = x_ref[...]
        h = _rmsnorm(x, gffn_ref[...]) * (1.0 + mod_ref[1]) + mod_ref[0]
        ht = h.T.astype(BF16)
        h2_ref[...] = ht
        q_ref[...] = _dot(wqT_ref[...], ht)
        outT_ref[...] = jnp.zeros_like(outT_ref)

        def head_body(hh, carry):
            for s in range(2):
                r0 = pl.multiple_of((2 * hh + s) * N_KEYS, N_KEYS)
                qs = q_ref[pl.ds(r0, N_KEYS), :].astype(BF16)
                sc = _dot(keys_ref[2 * hh + s], qs) * LOG2E
                if s == 0:
                    s1_ref[hh] = sc
                else:
                    s2_ref[hh] = sc
                srt = _sort_blocks_desc([sc[j * SUBLANES:(j + 1) * SUBLANES] for j in range(N_KEYS // SUBLANES)])
                for k, m in enumerate(_pop_largest(srt, TOPK)):
                    top_ref[s, k:k + 1, :] = m
            a = top_ref[0]
            b = top_ref[1]
            best = a[0:1] + b[0:1]
            zsum = None
            for m in _pop_largest(_sort_blocks_desc(_cand_blocks(a, b)), TOPK):
                e = jnp.exp2(m - best)
                zsum = e if zsum is None else zsum + e
            shift = best + jnp.log2(zsum)
            s1_ref[hh] = s1_ref[hh] - shift
            thr_ref[hh] = _pop_largest(_sort_blocks_desc(_cand_blocks(a - shift, b)), TOPK)[-1]
            return carry

        lax.fori_loop(0, R_HEADS, head_body, 0)

    eh = u_ref.shape[0] // 2
    act_ref[0:eh, :] = _dot(u_ref[0:eh, :], h2_ref[...])
    act_ref[eh:, :] = _dot(u_ref[eh:, :], h2_ref[...])
    i1 = pl.multiple_of(c * n_blk, n_blk)
    s1_rows = [s1_ref[hh, pl.ds(i1, n_blk), :] for hh in range(R_HEADS)]
    for r in range(n_blk):
        acc = None
        for hh in range(R_HEADS):
            val = s1_rows[hh][r:r + 1, :] + s2_ref[hh]
            gate = jnp.where(val >= thr_ref[hh], jnp.exp2(val), 0.0)
            acc = gate if acc is None else acc + gate
        blk = slice(r * N_KEYS, (r + 1) * N_KEYS)
        coef_ref[blk, :] = (acc * _gelu(act_ref[blk, :])).astype(BF16)
    dh = vt_ref.shape[0] // 2
    outT_ref[0:dh, :] += _dot(vt_ref[0:dh, :], coef_ref[...])
    outT_ref[dh:, :] += _dot(vt_ref[dh:, :], coef_ref[...])

    @pl.when(c == pl.num_programs(1) - 1)
    def _():
        x2 = x_ref[...] + mod_ref[2] * outT_ref[...].T
        y_ref[...] = _rmsnorm(x2, gfin_ref[...]) * (1.0 + mod_ref[4]) + mod_ref[3]


def _peer_call(x, mod5, tiles_per_row, p, tl, ec=PEER_EXPERTS):
    t, d = x.shape
    n_exp = p["expert_u"].shape[0]
    r = mod5.shape[2]
    scratch = [
        pltpu.VMEM((d, tl), BF16),
        pltpu.VMEM((R_HEADS * 2 * N_KEYS, tl), F32),
        pltpu.VMEM((R_HEADS, N_KEYS, tl), F32),
        pltpu.VMEM((R_HEADS, N_KEYS, tl), F32),
        pltpu.VMEM((R_HEADS, 1, tl), F32),
        pltpu.VMEM((2, TOPK, tl), F32),
        pltpu.VMEM((ec, tl), F32),
        pltpu.VMEM((ec, tl), BF16),
        pltpu.VMEM((d, tl), F32),
    ]
    return pl.pallas_call(
        _peer_kernel,
        grid=(t // tl, n_exp // ec),
        in_specs=[
            pl.BlockSpec((tl, d), lambda i, c: (i, 0)),
            pl.BlockSpec((None, 5, r, d), lambda i, c: (i // tiles_per_row, 0, 0, 0)),
            _full((1, d)), _full((1, d)),
            _full(p["w_qT"].shape), _full(p["keys"].shape),
            pl.BlockSpec((ec, d), lambda i, c: (c, 0)),
            pl.BlockSpec((d, ec), lambda i, c: (0, c)),
        ],
        out_specs=pl.BlockSpec((tl, d), lambda i, c: (i, 0)),
        out_shape=jax.ShapeDtypeStruct((t, d), F32),
        scratch_shapes=scratch,
        compiler_params=_params(("parallel", "arbitrary")),
        name="peer",
    )(x, mod5, p["g_ffn"], p["g_final"], p["w_qT"], p["keys"], p["expert_u"], p["expert_vT"])


def _prep(w_in, g_mix, g_v, w_s, b_s, conv_w, conv_b, dt_bias, a_log, d_skip, g_ssm, w_out, g_ffn, w_q,
          sub_keys, expert_u, expert_v, g_final):
    d = w_in.shape[0]
    o_z = 2 * 1024
    o_x = o_z + 1024
    o_dt = o_x + 1536
    w_dt = jnp.zeros((d, LANES), F32).at[:, :B_HEADS].set(w_in[:, o_dt:o_dt + B_HEADS])
    pad16 = lambda v: jnp.zeros((LANES,), F32).at[:B_HEADS].set(v)
    heads = jnp.arange(LANES)[:, None]
    p = {
        "g_mix": g_mix.reshape(1, d),
        "w_uv": w_in[:, 0:o_z].astype(BF16),
        "w_z": w_in[:, o_z:o_x].astype(BF16),
        "w_xbc": w_in[:, o_x:o_dt].astype(BF16),
        "w_dt": w_dt.astype(BF16),
        "w_dtT": w_dt.T.astype(BF16),
        "dt_bias_row": pad16(dt_bias).reshape(1, LANES),
        "dt_bias_col": pad16(dt_bias).reshape(LANES, 1),
        "a_log_row": pad16(a_log).reshape(1, LANES),
        "a_log_sq": jnp.broadcast_to(pad16(a_log).reshape(LANES, 1), (LANES, LANES)),
        "expand": (heads == jnp.arange(B_HEADS * LANES)[None, :] // LANES).astype(BF16),
        "expand64": (heads == jnp.arange(B_HEADS * B_HEAD_DIM)[None, :] // B_HEAD_DIM).astype(BF16),
        "g_v": g_v.reshape(1, -1),
        "w_s": w_s,
        "b_s_rep": jnp.broadcast_to(b_s[:, :, None], (A_HEADS, CHUNK, CHUNK)),
        "w_s00": jnp.repeat(w_s[:, 0, 0], A_HEAD_DIM).reshape(1, -1),
        "b_s0": jnp.repeat(b_s[:, 0], A_HEAD_DIM).reshape(1, -1),
        "conv_w": conv_w,
        "conv_b": conv_b.reshape(1, -1),
        "d_skip_exp": jnp.repeat(d_skip, B_HEAD_DIM).reshape(1, -1),
        "g_ssm": g_ssm.reshape(1, -1),
        "w_out": w_out.astype(BF16),
        "g_ffn": g_ffn.reshape(1, d),
        "g_final": g_final.reshape(1, d),
        "w_qT": w_q.T.astype(BF16),
        "keys": sub_keys.reshape(R_HEADS * 2, N_KEYS, -1).astype(BF16),
        "expert_u": expert_u.astype(BF16),
        "expert_vT": expert_v.T.astype(BF16),
    }
    return p


def kernel(x_prompt, x_sample, c_prompt, c_sample, state_ssm, state_conv, w_ada, b_ada, g_mix, w_in, g_v, w_s, b_s,
           conv_w, conv_b, dt_bias, a_log, d_skip, g_ssm, w_out, g_ffn, w_q, sub_keys, expert_u, expert_v,
           w_ada_f, b_ada_f, g_final):
    assert w_ada.shape[0] == 1, "single-layer trunk"
    bp, seq, d = x_prompt.shape
    ns = x_sample.shape[0]
    p = _prep(w_in[0], g_mix[0], g_v[0], w_s[0], b_s[0], conv_w[0], conv_b[0], dt_bias[0], a_log[0], d_skip[0],
              g_ssm[0], w_out[0], g_ffn[0], w_q[0], sub_keys[0], expert_u[0], expert_v[0], g_final)

    c_all = jnp.concatenate([c_prompt, c_sample], axis=0)
    mod = _ada_call(c_all, w_ada[0], b_ada[0]).reshape(bp + ns, N_MOD, d)
    modf = _ada_call(c_all, w_ada_f, b_ada_f).reshape(bp + ns, 2, d)
    mod5 = jnp.concatenate([mod[:, 3:6], modf], axis=1)

    x1_p, ssm_p, ctail_p = _mix_call(x_prompt, mod[:bp, 0:3].reshape(bp, 3, 1, d), p)
    y_p = _peer_call(x1_p.reshape(bp * seq, d), mod5[:bp].reshape(bp, 5, 1, d), seq // PEER_TOKENS, p, PEER_TOKENS)

    xs_in = x_sample.reshape(ns, d)
    mod1_s = jnp.transpose(mod[bp:, 0:3], (1, 0, 2))
    sconv = state_conv[0].reshape(ns, -1)
    v_s, cnew_s, ya_s, z_s, xc_s, bc_s, xdtT_s, dec_s = _s1_call(xs_in, mod1_s, sconv, p)
    ssm_s, yssd_s = _s2_call(dec_s[:, :B_HEADS], state_ssm[0], xdtT_s, bc_s)
    x1_s = _s3_call(xs_in, mod1_s, yssd_s, xc_s, z_s, ya_s, p)
    mod5_s = jnp.transpose(mod5[bp:], (1, 0, 2)).reshape(1, 5, ns, d)
    y_s = _peer_call(x1_s, mod5_s, 1, p, ns)

    return (
        y_p.reshape(bp, seq, d),
        y_s.reshape(ns, 1, d),
        ssm_p.reshape(1, bp, B_HEADS, B_HEAD_DIM, D_STATE),
        ctail_p[:, SUBLANES - (CONV_W - 1):, :].reshape(1, bp, CONV_W - 1, -1),
        ssm_s.reshape(1, ns, B_HEADS, B_HEAD_DIM, D_STATE),
        cnew_s.reshape(1, ns, CONV_W - 1, -1),
        v_s.reshape(1, ns, 1, -1),
    )
```

```python
import functools
import math

import jax
import jax.numpy as jnp
from jax import lax
from jax.experimental import pallas as pl
from jax.experimental.pallas import tpu as pltpu

F32 = jnp.float32
BF16 = jnp.bfloat16
NEG_INF = float("-inf")

EPS = 1e-6
A_HEADS = 8
A_HEAD_DIM = 128
CHUNK = 128
B_HEADS = 16
B_HEAD_DIM = 64
B_GROUPS = 2
D_STATE = 128
CONV_W = 4
N_KEYS = 128
R_HEADS = 8
TOPK = 16
N_MOD = 6

LANES = 128
SUBLANES = 8
VMEM_LIMIT = 56 * 1024 * 1024

MIX_ROWS = 512
PEER_TOKENS = 256
PEER_EXPERTS = 2048
LOG2E = 1.4426950408889634
S2_TOKENS = 8


def _dot(a, b):
    return jnp.dot(a, b, preferred_element_type=F32)


def _dot_nt(a, b):
    return lax.dot_general(a, b, (((1,), (1,)), ((), ())), preferred_element_type=F32)


def _dot_tn(a, b):
    return lax.dot_general(a, b, (((0,), (0,)), ((), ())), preferred_element_type=F32)


def _split2(x):
    hi = x.astype(BF16)
    lo = (x - hi.astype(F32)).astype(BF16)
    return hi, lo


def _split3(x):
    hi = x.astype(BF16)
    r = x - hi.astype(F32)
    mid = r.astype(BF16)
    lo = (r - mid.astype(F32)).astype(BF16)
    return hi, mid, lo


def _dot_x3(a, b):
    a1, a2 = _split2(a)
    b1, b2 = _split2(b)
    return _dot(a1, b1) + (_dot(a1, b2) + _dot(a2, b1))


def _silu(x):
    return x / (1.0 + jnp.exp(-x))


def _gelu(x):
    return 0.5 * x * (1.0 + lax.erf(x * (1.0 / math.sqrt(2.0))))


def _softplus(x):
    return jnp.maximum(x, 0.0) + jnp.log1p(jnp.exp(-jnp.abs(x)))


def _rmsnorm(x, g):
    return x * lax.rsqrt(jnp.mean(x * x, axis=-1, keepdims=True) + EPS) * g


def _full(shape):
    nd = len(shape)
    return pl.BlockSpec(shape, lambda *_: (0,) * nd, pipeline_mode=pl.Buffered(1))


def _dot2(a, w_ref):
    h = w_ref.shape[1] // 2
    return jnp.concatenate([_dot(a, w_ref[:, :h]), _dot(a, w_ref[:, h:])], axis=1)


def _params(sem):
    return pltpu.CompilerParams(dimension_semantics=sem, vmem_limit_bytes=VMEM_LIMIT)


def _ada_kernel(c_ref, w_ref, b_ref, o_ref):
    o_ref[...] = _dot_x3(_silu(c_ref[...]), w_ref[...]) + b_ref[...]


def _ada_call(c, w, b, tn=512):
    m, k = c.shape
    n = w.shape[1]
    return pl.pallas_call(
        _ada_kernel,
        grid=(n // tn,),
        in_specs=[_full((m, k)), pl.BlockSpec((k, tn), lambda j: (0, j)), pl.BlockSpec((1, tn), lambda j: (0, j))],
        out_specs=pl.BlockSpec((m, tn), lambda j: (0, j)),
        out_shape=jax.ShapeDtypeStruct((m, n), F32),
        compiler_params=_params(("parallel",)),
        name="ada",
    )(c, w, b.reshape(1, n))


def _gate_mlp_chunk(u, v, gv, ws_ref, bs_ref, causal):
    ys, vs = [], []
    for g in range(A_HEADS):
        sl = slice(g * A_HEAD_DIM, (g + 1) * A_HEAD_DIM)
        ug = _gelu(u[:, sl])
        vn = _rmsnorm(_gelu(v[:, sl]), gv[:, sl])
        w = jnp.where(causal, ws_ref[g], 0.0).astype(BF16)
        s = _dot(w, vn.astype(BF16)) + bs_ref[g]
        ys.append((ug * s).astype(BF16))
        vs.append(vn)
    return ys, vs


def _mix_kernel(x_ref, mod_ref, gmix_ref, wuv_ref, wz_ref, wxbc_ref, wdt_ref, wdtT_ref,
                dtb_ref, dtbT_ref, alog_ref, alogT_ref, exp_ref, gv_ref, ws_ref, bs_ref,
                cw_ref, cb_ref, dsk_ref, gssm_ref, wout_ref,
                o_ref, ssm_ref, ctail_ref,
                hst_ref, tail_ref, uv_ref, z_ref, xs_ref, dt_ref, dtT_ref, ycat_ref):
    t = pl.program_id(1)
    tm = x_ref.shape[0]
    n_chunks = tm // CHUNK

    @pl.when(t == 0)
    def _():
        hst_ref[...] = jnp.zeros_like(hst_ref)
        tail_ref[...] = jnp.zeros_like(tail_ref)

    x = x_ref[...]
    h = _rmsnorm(x, gmix_ref[...]) * (1.0 + mod_ref[1]) + mod_ref[0]
    hb = h.astype(BF16)
    uv_ref[...] = _dot2(hb, wuv_ref)
    z_ref[...] = _dot2(hb, wz_ref)
    dt_ref[...] = _softplus(_dot(hb, wdt_ref[...]) + dtb_ref[...])
    dtT_ref[...] = _softplus(_dot_nt(wdtT_ref[...], hb) + dtbT_ref[...])

    cur = _dot2(hb, wxbc_ref)
    prev = tail_ref[...]
    row8 = lax.broadcasted_iota(jnp.int32, (SUBLANES, 1), 0)
    acc = cb_ref[...] + cur * cw_ref[CONV_W - 1:CONV_W, :]
    for k in range(1, CONV_W):
        r = pltpu.roll(cur, k, axis=0)
        head = jnp.where(row8 < k, pltpu.roll(prev, k, axis=0), r[0:SUBLANES])
        shifted = jnp.concatenate([head, r[SUBLANES:]], axis=0)
        acc = acc + shifted * cw_ref[CONV_W - 1 - k:CONV_W - k, :]
    xs_ref[...] = _silu(acc)
    new_tail = cur[tm - SUBLANES:tm]
    tail_ref[...] = new_tail
    ctail_ref[...] = new_tail

    ri = lax.broadcasted_iota(jnp.int32, (CHUNK, CHUNK), 0)
    ci = lax.broadcasted_iota(jnp.int32, (CHUNK, CHUNK), 1)
    causal = ri >= ci
    tril = jnp.where(causal, 1.0, 0.0).astype(BF16)
    triu = jnp.where(ri <= ci, 1.0, 0.0).astype(BF16)
    low_half = ri < B_HEAD_DIM
    low_lane = ci < B_HEAD_DIM
    a_col = -jnp.exp(alog_ref[...])
    a_row = -jnp.exp(alogT_ref[...])

    def chunk_body(c, carry):
        r0 = pl.multiple_of(c * CHUNK, CHUNK)
        rows = pl.ds(r0, CHUNK)

        ys, _ = _gate_mlp_chunk(uv_ref[rows, 0:1024], uv_ref[rows, 1024:2048], gv_ref[...], ws_ref, bs_ref, causal)
        for g in range(A_HEADS):
            ycat_ref[rows, g * A_HEAD_DIM:(g + 1) * A_HEAD_DIM] = ys[g]

        dt = dt_ref[rows, :]
        d_a = dt * a_col
        h1, h2, h3 = _split3(d_a)
        acum = _dot(tril, h1) + (_dot(tril, h2) + _dot(tril, h3))
        e1, e2, e3 = _split3(acum)
        acum_bc = _dot(e1, exp_ref[...]) + (_dot(e2, exp_ref[...]) + _dot(e3, exp_ref[...]))
        f1, f2 = _split2(dt)
        dt_bc = _dot(f1, exp_ref[...]) + _dot(f2, exp_ref[...])
        dt_t = dtT_ref[:, rows]
        g1, g2, g3 = _split3(dt_t * a_row)
        acum_t = _dot(g1, triu) + (_dot(g2, triu) + _dot(g3, triu))

        xs = xs_ref[rows, 0:1024]
        zz = z_ref[rows, :]
        y_pairs = []
        for g in range(B_GROUPS):
            bg = xs_ref[rows, 1024 + g * D_STATE:1024 + (g + 1) * D_STATE]
            cg = xs_ref[rows, 1280 + g * D_STATE:1280 + (g + 1) * D_STATE]
            cbm = _dot_nt(cg.astype(BF16), bg.astype(BF16))
            for pr in range(B_HEADS // B_GROUPS // 2):
                pair = g * 4 + pr
                xpair = xs[:, pair * 128:(pair + 1) * 128]
                hprev = hst_ref[pair]
                y_acc = None
                st_acc = None
                alasts = []
                for e in range(2):
                    hd = pair * 2 + e
                    ab = acum_bc[:, hd * 128:(hd + 1) * 128]
                    seg = ab - acum_t[hd:hd + 1, :]
                    lm = jnp.exp(jnp.where(causal, seg, NEG_INF))
                    gd = (cbm * lm * dt_t[hd:hd + 1, :]).astype(BF16)
                    sel_l = low_lane if e == 0 else jnp.logical_not(low_lane)
                    sel_r = low_half if e == 0 else jnp.logical_not(low_half)
                    xh = jnp.where(sel_l, xpair, 0.0).astype(BF16)
                    hp = jnp.where(sel_r, hprev, 0.0).astype(BF16)
                    cs = (cg * jnp.exp(ab)).astype(BF16)
                    alast = ab[CHUNK - 1:CHUNK, :]
                    bsc = (bg * (jnp.exp(alast - ab) * dt_bc[:, hd * 128:(hd + 1) * 128])).astype(BF16)
                    y_e = _dot(gd, xh) + _dot_nt(cs, hp)
                    st_e = _dot_tn(xh, bsc)
                    y_acc = y_e if y_acc is None else y_acc + y_e
                    st_acc = st_e if st_acc is None else st_acc + st_e
                    alasts.append(alast)
                decay = jnp.exp(jnp.where(low_half, alasts[0], alasts[1]))
                hst_ref[pair] = hprev * decay + st_acc
                y_pairs.append(y_acc)
        y = jnp.concatenate(y_pairs, axis=1) + dsk_ref[...] * xs
        y = y * _silu(zz)
        half = B_HEADS * B_HEAD_DIM // B_GROUPS
        for g in range(B_GROUPS):
            sl = slice(g * half, (g + 1) * half)
            ycat_ref[rows, 1024 + g * half:1024 + (g + 1) * half] = _rmsnorm(y[:, sl], gssm_ref[:, sl]).astype(BF16)
        return carry

    lax.fori_loop(0, n_chunks, chunk_body, 0)

    o_ref[...] = x + mod_ref[2] * _dot2(ycat_ref[...], wout_ref)

    @pl.when(t == pl.num_programs(1) - 1)
    def _():
        ssm_ref[...] = hst_ref[...]


def _mix_call(x, mod1, p, tm=MIX_ROWS):
    b, s, d = x.shape
    n_pairs = B_HEADS // 2
    in_specs = [
        pl.BlockSpec((None, tm, d), lambda i, t: (i, t, 0)),
        pl.BlockSpec((None, 3, 1, d), lambda i, t: (i, 0, 0, 0)),
        _full((1, d)),
        _full(p["w_uv"].shape), _full(p["w_z"].shape), _full(p["w_xbc"].shape),
        _full(p["w_dt"].shape), _full(p["w_dtT"].shape),
        _full((1, LANES)), _full((LANES, 1)), _full((1, LANES)), _full((LANES, LANES)),
        _full(p["expand"].shape),
        _full((1, 1024)), _full((A_HEADS, CHUNK, CHUNK)), _full((A_HEADS, CHUNK, CHUNK)),
        _full((CONV_W, 1536)), _full((1, 1536)), _full((1, 1024)), _full((1, 1024)),
        _full(p["w_out"].shape),
    ]
    out_specs = [
        pl.BlockSpec((None, tm, d), lambda i, t: (i, t, 0)),
        pl.BlockSpec((None, n_pairs, 128, D_STATE), lambda i, t: (i, 0, 0, 0)),
        pl.BlockSpec((None, SUBLANES, 1536), lambda i, t: (i, 0, 0)),
    ]
    out_shape = [
        jax.ShapeDtypeStruct((b, s, d), F32),
        jax.ShapeDtypeStruct((b, n_pairs, 128, D_STATE), F32),
        jax.ShapeDtypeStruct((b, SUBLANES, 1536), F32),
    ]
    scratch = [
        pltpu.VMEM((n_pairs, 128, D_STATE), F32),
        pltpu.VMEM((SUBLANES, 1536), F32),
        pltpu.VMEM((tm, 2048), F32),
        pltpu.VMEM((tm, 1024), F32),
        pltpu.VMEM((tm, 1536), F32),
        pltpu.VMEM((tm, LANES), F32),
        pltpu.VMEM((LANES, tm), F32),
        pltpu.VMEM((tm, 2048), BF16),
    ]
    return pl.pallas_call(
        _mix_kernel,
        grid=(b, s // tm),
        in_specs=in_specs, out_specs=out_specs, out_shape=out_shape, scratch_shapes=scratch,
        compiler_params=_params(("parallel", "arbitrary")),
        name="mix_prompt",
    )(x, mod1, p["g_mix"], p["w_uv"], p["w_z"], p["w_xbc"], p["w_dt"], p["w_dtT"],
      p["dt_bias_row"], p["dt_bias_col"], p["a_log_row"], p["a_log_sq"], p["expand"],
      p["g_v"], p["w_s"], p["b_s_rep"], p["conv_w"], p["conv_b"], p["d_skip_exp"], p["g_ssm"], p["w_out"])


def _s1_kernel(x_ref, mod_ref, gmix_ref, wuv_ref, wz_ref, wxbc_ref, wdt_ref, dtb_ref, alog_ref, exp64_ref,
               gv_ref, ws0_ref, bs0_ref, cw_ref, cb_ref, sconv_ref,
               v_ref, cnew_ref, ya_ref, z_ref, xs_ref, bc_ref, xdtT_ref, dec_ref):
    x = x_ref[...]
    h = _rmsnorm(x, gmix_ref[...]) * (1.0 + mod_ref[1]) + mod_ref[0]
    hb = h.astype(BF16)
    uv = _dot(hb, wuv_ref[...])
    gv = gv_ref[...]
    for g in range(A_HEADS):
        sl = slice(g * A_HEAD_DIM, (g + 1) * A_HEAD_DIM)
        ug = _gelu(uv[:, sl])
        vn = _rmsnorm(_gelu(uv[:, 1024 + g * A_HEAD_DIM:1024 + (g + 1) * A_HEAD_DIM]), gv[:, sl])
        v_ref[:, sl] = vn
        s = ws0_ref[:, sl] * vn + bs0_ref[:, sl]
        ya_ref[:, sl] = (ug * s).astype(BF16)
    z_ref[...] = _dot(hb, wz_ref[...])
    raw = _dot(hb, wxbc_ref[...])
    cd = raw.shape[1]
    acc = cb_ref[...] + raw * cw_ref[CONV_W - 1:CONV_W, :]
    for k in range(CONV_W - 1):
        acc = acc + sconv_ref[:, k * cd:(k + 1) * cd] * cw_ref[k:k + 1, :]
    cnew_ref[:, 0:(CONV_W - 2) * cd] = sconv_ref[:, cd:(CONV_W - 1) * cd]
    cnew_ref[:, (CONV_W - 2) * cd:(CONV_W - 1) * cd] = raw
    xbc = _silu(acc)
    xs = xbc[:, 0:1024]
    xs_ref[...] = xs
    bc_ref[...] = xbc[:, 1024:1536]
    dt = _softplus(_dot(hb, wdt_ref[...]) + dtb_ref[...])
    dec_ref[...] = jnp.exp(dt * (-jnp.exp(alog_ref[...])))
    f1, f2, f3 = _split3(dt)
    dt64 = _dot(f1, exp64_ref[...]) + (_dot(f2, exp64_ref[...]) + _dot(f3, exp64_ref[...]))
    xdtT_ref[...] = (xs * dt64).T.astype(BF16)


def _s1_call(x, mod1, sconv, p):
    n, d = x.shape
    outs = [
        jax.ShapeDtypeStruct((n, 1024), F32),
        jax.ShapeDtypeStruct((n, 3 * 1536), F32),
        jax.ShapeDtypeStruct((n, 1024), BF16),
        jax.ShapeDtypeStruct((n, 1024), F32),
        jax.ShapeDtypeStruct((n, 1024), F32),
        jax.ShapeDtypeStruct((n, 512), F32),
        jax.ShapeDtypeStruct((1024, n), BF16),
        jax.ShapeDtypeStruct((n, LANES), F32),
    ]
    args = (x, mod1, p["g_mix"], p["w_uv"], p["w_z"], p["w_xbc"], p["w_dt"], p["dt_bias_row"], p["a_log_row"],
            p["expand64"], p["g_v"], p["w_s00"], p["b_s0"], p["conv_w"], p["conv_b"], sconv)
    return pl.pallas_call(
        _s1_kernel,
        grid=(1,),
        in_specs=[_full(a.shape) for a in args],
        out_specs=[_full(o.shape) for o in outs],
        out_shape=outs,
        compiler_params=_params(("arbitrary",)),
        name="mix_sample_in",
    )(*args)


def _s2_kernel(dec_ref, st_ref, xdtT_ref, bc_ref, cblk_ref, o_ref, y_ref):
    i = pl.program_id(0)
    bt = st_ref.shape[0]
    n = bc_ref.shape[0]
    half = B_HEADS * B_HEAD_DIM // B_GROUPS
    rowi = lax.broadcasted_iota(jnp.int32, (n, 1), 0)
    for bb in range(bt):
        b = i * bt + bb
        for g in range(B_GROUPS):
            bm = bc_ref[:, g * D_STATE:(g + 1) * D_STATE]
            rb = jnp.where(rowi == b, bm, 0.0).astype(BF16)
            outer = _dot(xdtT_ref[g * half:(g + 1) * half, :], rb)
            for k in range(B_HEADS // B_GROUPS):
                hd = g * (B_HEADS // B_GROUPS) + k
                sl = slice(k * B_HEAD_DIM, (k + 1) * B_HEAD_DIM)
                o_ref[bb, g, sl, :] = st_ref[bb, g, sl, :] * dec_ref[b, hd] + outer[sl, :]
            crow = cblk_ref[bb:bb + 1, 256 + g * D_STATE:256 + (g + 1) * D_STATE]
            c8 = jnp.broadcast_to(crow, (SUBLANES, D_STATE)).astype(BF16)
            yr = _dot_nt(c8, o_ref[bb, g].astype(BF16))
            y_ref[bb:bb + 1, g * half:(g + 1) * half] = yr[0:1]


def _s2_call(dec, state, xdtT, bc, bt=S2_TOKENS):
    n = state.shape[0]
    half = B_HEADS * B_HEAD_DIM // B_GROUPS
    st = state.reshape(n, B_GROUPS, half, D_STATE)
    grid_spec = pltpu.PrefetchScalarGridSpec(
        num_scalar_prefetch=1,
        grid=(n // bt,),
        in_specs=[
            pl.BlockSpec((bt, B_GROUPS, half, D_STATE), lambda i, d: (i, 0, 0, 0)),
            pl.BlockSpec(xdtT.shape, lambda i, d: (0, 0)),
            pl.BlockSpec(bc.shape, lambda i, d: (0, 0)),
            pl.BlockSpec((bt, bc.shape[1]), lambda i, d: (i, 0)),
        ],
        out_specs=[
            pl.BlockSpec((bt, B_GROUPS, half, D_STATE), lambda i, d: (i, 0, 0, 0)),
            pl.BlockSpec((bt, 1024), lambda i, d: (i, 0)),
        ],
    )
    new_state, y = pl.pallas_call(
        _s2_kernel,
        grid_spec=grid_spec,
        out_shape=[jax.ShapeDtypeStruct(st.shape, F32), jax.ShapeDtypeStruct((n, 1024), F32)],
        compiler_params=_params(("arbitrary",)),
        name="mix_sample_state",
    )(dec, st, xdtT, bc, bc)
    return new_state.reshape(state.shape), y


def _s3_kernel(x_ref, mod_ref, y_ref, xs_ref, z_ref, ya_ref, dsk_ref, gssm_ref, wout_ref, o_ref):
    y = (y_ref[...] + dsk_ref[...] * xs_ref[...]) * _silu(z_ref[...])
    half = B_HEADS * B_HEAD_DIM // B_GROUPS
    parts = [ya_ref[...]]
    for g in range(B_GROUPS):
        sl = slice(g * half, (g + 1) * half)
        parts.append(_rmsnorm(y[:, sl], gssm_ref[:, sl]).astype(BF16))
    cat = jnp.concatenate(parts, axis=1)
    o_ref[...] = x_ref[...] + mod_ref[2] * _dot(cat, wout_ref[...])


def _s3_call(x, mod1, y, xs, z, ya, p):
    args = (x, mod1, y, xs, z, ya, p["d_skip_exp"], p["g_ssm"], p["w_out"])
    return pl.pallas_call(
        _s3_kernel,
        grid=(1,),
        in_specs=[_full(a.shape) for a in args],
        out_specs=_full(x.shape),
        out_shape=jax.ShapeDtypeStruct(x.shape, F32),
        compiler_params=_params(("arbitrary",)),
        name="mix_sample_out",
    )(*args)


def _cand_blocks(a, b):
    row8 = lax.broadcasted_iota(jnp.int32, (SUBLANES, 1), 0)
    a8 = a[0:SUBLANES]
    blocks = []
    for j in range(SUBLANES):
        blk = a8 + b[j:j + 1]
        cnt = min(SUBLANES, TOPK // (j + 1))
        if cnt < SUBLANES:
            blk = jnp.where(row8 < cnt, blk, NEG_INF)
        blocks.append(blk)
    blocks.append(a[0:1] + b[SUBLANES:TOPK])
    blocks.append(a[SUBLANES:TOPK] + b[0:1])
    return blocks


def _merge_exchange_pairs(n):
    pairs = []
    t = max(1, math.ceil(math.log2(n)))
    p = 1 << (t - 1)
    while p > 0:
        q, r, d = 1 << (t - 1), 0, p
        while d > 0:
            pairs.extend((i, i + d) for i in range(n - d) if (i & p) == r)
            d, q, r = q - p, q >> 1, p
        p >>= 1
    return pairs


def _sort_blocks_desc(blocks):
    blocks = list(blocks)
    for i, j in _merge_exchange_pairs(len(blocks)):
        hi = jnp.maximum(blocks[i], blocks[j])
        blocks[j] = jnp.minimum(blocks[i], blocks[j])
        blocks[i] = hi
    return blocks


def _pop_largest(blocks, k):
    blocks = list(blocks)
    nb = len(blocks)
    rows = []
    for t in range(k):
        m = jnp.max(blocks[0], axis=0, keepdims=True)
        rows.append(m)
        need = min(k - t - 1, nb)
        if need == 0:
            break
        sel = blocks[0] == m
        for j in range(need):
            nxt = blocks[j + 1] if j + 1 < nb else NEG_INF
            blocks[j] = jnp.where(sel, nxt, blocks[j])
    return rows


def _peer_kernel(x_ref, mod_ref, gffn_ref, gfin_ref, wqT_ref, keys_ref, u_ref, vt_ref, y_ref,
                 h2_ref, q_ref, s1_ref, s2_ref, thr_ref, tops_ref, act_a, act_b, coef_ref, outT_ref,
                 *, n_chunks):
    c = pl.program_id(1)
    n_blk = u_ref.shape[0] // N_KEYS

    @pl.when(c == 0)
    def _():
        x = x_ref[...]
        h = _rmsnorm(x, gffn_ref[...]) * (1.0 + mod_ref[1]) + mod_ref[0]
        ht = h.T.astype(BF16)
        h2_ref[...] = ht
        qh = wqT_ref.shape[0] // 2
        q_ref[0:qh, :] = _dot(wqT_ref[0:qh, :], ht)
        q_ref[qh:, :] = _dot(wqT_ref[qh:, :], ht)
        outT_ref[...] = jnp.zeros_like(outT_ref)

        def one_head(hh, top_ref):
            for s in range(2):
                r0 = pl.multiple_of((2 * hh + s) * N_KEYS, N_KEYS)
                qs = q_ref[pl.ds(r0, N_KEYS), :].astype(BF16)
                sc = _dot(keys_ref[2 * hh + s], qs) * LOG2E
                if s == 0:
                    s1_ref[hh] = sc
                else:
                    s2_ref[hh] = sc
                srt = _sort_blocks_desc([sc[j * SUBLANES:(j + 1) * SUBLANES] for j in range(N_KEYS // SUBLANES)])
                for k, m in enumerate(_pop_largest(srt, TOPK)):
                    top_ref[s, k:k + 1, :] = m
            a = top_ref[0]
            b = top_ref[1]
            best = a[0:1] + b[0:1]
            zsum = None
            for m in _pop_largest(_sort_blocks_desc(_cand_blocks(a, b)), TOPK):
                e = jnp.exp2(m - best)
                zsum = e if zsum is None else zsum + e
            shift = best + jnp.log2(zsum)
            s1_ref[hh] = s1_ref[hh] - shift
            thr_ref[hh] = _pop_largest(_sort_blocks_desc(_cand_blocks(a - shift, b)), TOPK)[-1]

        def head_body(hp, carry):
            for e in range(2):
                one_head(2 * hp + e, tops_ref.at[e])
            return carry

        lax.fori_loop(0, R_HEADS // 2, head_body, 0)

    eh = u_ref.shape[0] // 2
    dh = vt_ref.shape[0] // 2

    def first_matmul(act_w):
        act_w[0:eh, :] = _dot(u_ref[0:eh, :], h2_ref[...])
        act_w[eh:, :] = _dot(u_ref[eh:, :], h2_ref[...])

    def gate_and_second_matmul(act_r):
        i1 = pl.multiple_of((c - 1) * n_blk, n_blk)
        s1_rows = [s1_ref[hh, pl.ds(i1, n_blk), :] for hh in range(R_HEADS)]
        for r in range(n_blk):
            acc = None
            for hh in range(R_HEADS):
                val = s1_rows[hh][r:r + 1, :] + s2_ref[hh]
                gate = jnp.where(val >= thr_ref[hh], jnp.exp2(val), 0.0)
                acc = gate if acc is None else acc + gate
            blk = slice(r * N_KEYS, (r + 1) * N_KEYS)
            coef_ref[blk, :] = (acc * _gelu(act_r[blk, :])).astype(BF16)
        outT_ref[0:dh, :] += _dot(vt_ref[0:dh, :], coef_ref[...])
        outT_ref[dh:, :] += _dot(vt_ref[dh:, :], coef_ref[...])

    in_middle = jnp.logical_and(c >= 1, c < n_chunks)

    @pl.when(c == 0)
    def _():
        first_matmul(act_a)

    @pl.when(jnp.logical_and(in_middle, c % 2 == 1))
    def _():
        first_matmul(act_b)
        gate_and_second_matmul(act_a)

    @pl.when(jnp.logical_and(in_middle, c % 2 == 0))
    def _():
        first_matmul(act_a)
        gate_and_second_matmul(act_b)

    @pl.when(c == n_chunks)
    def _():
        gate_and_second_matmul(act_b if n_chunks % 2 == 0 else act_a)
        x2 = x_ref[...] + mod_ref[2] * outT_ref[...].T
        y_ref[...] = _rmsnorm(x2, gfin_ref[...]) * (1.0 + mod_ref[4]) + mod_ref[3]


def _peer_call(x, mod5, tiles_per_row, p, tl, ec=PEER_EXPERTS):
    t, d = x.shape
    n_exp = p["expert_u"].shape[0]
    r = mod5.shape[2]
    scratch = [
        pltpu.VMEM((d, tl), BF16),
        pltpu.VMEM((R_HEADS * 2 * N_KEYS, tl), F32),
        pltpu.VMEM((R_HEADS, N_KEYS, tl), F32),
        pltpu.VMEM((R_HEADS, N_KEYS, tl), F32),
        pltpu.VMEM((R_HEADS, 1, tl), F32),
        pltpu.VMEM((2, 2, TOPK, tl), F32),
        pltpu.VMEM((ec, tl), F32),
        pltpu.VMEM((ec, tl), F32),
        pltpu.VMEM((ec, tl), BF16),
        pltpu.VMEM((d, tl), F32),
    ]
    n_chunks = n_exp // ec
    return pl.pallas_call(
        functools.partial(_peer_kernel, n_chunks=n_chunks),
        grid=(t // tl, n_chunks + 1),
        in_specs=[
            pl.BlockSpec((tl, d), lambda i, c: (i, 0)),
            pl.BlockSpec((None, 5, r, d), lambda i, c: (i // tiles_per_row, 0, 0, 0)),
            _full((1, d)), _full((1, d)),
            _full(p["w_qT"].shape), _full(p["keys"].shape),
            pl.BlockSpec((ec, d), lambda i, c: (jnp.minimum(c, n_chunks - 1), 0)),
            pl.BlockSpec((d, ec), lambda i, c: (0, jnp.maximum(c - 1, 0))),
        ],
        out_specs=pl.BlockSpec((tl, d), lambda i, c: (i, 0)),
        out_shape=jax.ShapeDtypeStruct((t, d), F32),
        scratch_shapes=scratch,
        compiler_params=_params(("parallel", "arbitrary")),
        name="peer",
    )(x, mod5, p["g_ffn"], p["g_final"], p["w_qT"], p["keys"], p["expert_u"], p["expert_vT"])


def _prep(w_in, g_mix, g_v, w_s, b_s, conv_w, conv_b, dt_bias, a_log, d_skip, g_ssm, w_out, g_ffn, w_q,
          sub_keys, expert_u, expert_v, g_final):
    d = w_in.shape[0]
    o_z = 2 * 1024
    o_x = o_z + 1024
    o_dt = o_x + 1536
    w_dt = jnp.zeros((d, LANES), F32).at[:, :B_HEADS].set(w_in[:, o_dt:o_dt + B_HEADS])
    pad16 = lambda v: jnp.zeros((LANES,), F32).at[:B_HEADS].set(v)
    heads = jnp.arange(LANES)[:, None]
    p = {
        "g_mix": g_mix.reshape(1, d),
        "w_uv": w_in[:, 0:o_z].astype(BF16),
        "w_z": w_in[:, o_z:o_x].astype(BF16),
        "w_xbc": w_in[:, o_x:o_dt].astype(BF16),
        "w_dt": w_dt.astype(BF16),
        "w_dtT": w_dt.T.astype(BF16),
        "dt_bias_row": pad16(dt_bias).reshape(1, LANES),
        "dt_bias_col": pad16(dt_bias).reshape(LANES, 1),
        "a_log_row": pad16(a_log).reshape(1, LANES),
        "a_log_sq": jnp.broadcast_to(pad16(a_log).reshape(LANES, 1), (LANES, LANES)),
        "expand": (heads == jnp.arange(B_HEADS * LANES)[None, :] // LANES).astype(BF16),
        "expand64": (heads == jnp.arange(B_HEADS * B_HEAD_DIM)[None, :] // B_HEAD_DIM).astype(BF16),
        "g_v": g_v.reshape(1, -1),
        "w_s": w_s,
        "b_s_rep": jnp.broadcast_to(b_s[:, :, None], (A_HEADS, CHUNK, CHUNK)),
        "w_s00": jnp.repeat(w_s[:, 0, 0], A_HEAD_DIM).reshape(1, -1),
        "b_s0": jnp.repeat(b_s[:, 0], A_HEAD_DIM).reshape(1, -1),
        "conv_w": conv_w,
        "conv_b": conv_b.reshape(1, -1),
        "d_skip_exp": jnp.repeat(d_skip, B_HEAD_DIM).reshape(1, -1),
        "g_ssm": g_ssm.reshape(1, -1),
        "w_out": w_out.astype(BF16),
        "g_ffn": g_ffn.reshape(1, d),
        "g_final": g_final.reshape(1, d),
        "w_qT": w_q.T.astype(BF16),
        "keys": sub_keys.reshape(R_HEADS * 2, N_KEYS, -1).astype(BF16),
        "expert_u": expert_u.astype(BF16),
        "expert_vT": expert_v.T.astype(BF16),
    }
    return p


def kernel(x_prompt, x_sample, c_prompt, c_sample, state_ssm, state_conv, w_ada, b_ada, g_mix, w_in, g_v, w_s, b_s,
           conv_w, conv_b, dt_bias, a_log, d_skip, g_ssm, w_out, g_ffn, w_q, sub_keys, expert_u, expert_v,
           w_ada_f, b_ada_f, g_final):
    assert w_ada.shape[0] == 1, "single-layer trunk"
    bp, seq, d = x_prompt.shape
    ns = x_sample.shape[0]
    p = _prep(w_in[0], g_mix[0], g_v[0], w_s[0], b_s[0], conv_w[0], conv_b[0], dt_bias[0], a_log[0], d_skip[0],
              g_ssm[0], w_out[0], g_ffn[0], w_q[0], sub_keys[0], expert_u[0], expert_v[0], g_final)

    c_all = jnp.concatenate([c_prompt, c_sample], axis=0)
    mod = _ada_call(c_all, w_ada[0], b_ada[0]).reshape(bp + ns, N_MOD, d)
    modf = _ada_call(c_all, w_ada_f, b_ada_f).reshape(bp + ns, 2, d)
    mod5 = jnp.concatenate([mod[:, 3:6], modf], axis=1)

    x1_p, ssm_p, ctail_p = _mix_call(x_prompt, mod[:bp, 0:3].reshape(bp, 3, 1, d), p)
    y_p = _peer_call(x1_p.reshape(bp * seq, d), mod5[:bp].reshape(bp, 5, 1, d), seq // PEER_TOKENS, p, PEER_TOKENS)

    xs_in = x_sample.reshape(ns, d)
    mod1_s = jnp.transpose(mod[bp:, 0:3], (1, 0, 2))
    sconv = state_conv[0].reshape(ns, -1)
    v_s, cnew_s, ya_s, z_s, xc_s, bc_s, xdtT_s, dec_s = _s1_call(xs_in, mod1_s, sconv, p)
    ssm_s, yssd_s = _s2_call(dec_s[:, :B_HEADS], state_ssm[0], xdtT_s, bc_s)
    x1_s = _s3_call(xs_in, mod1_s, yssd_s, xc_s, z_s, ya_s, p)
    mod5_s = jnp.transpose(mod5[bp:], (1, 0, 2)).reshape(1, 5, ns, d)
    y_s = _peer_call(x1_s, mod5_s, 1, p, ns)

    return (
        y_p.reshape(bp, seq, d),
        y_s.reshape(ns, 1, d),
        ssm_p.reshape(1, bp, B_HEADS, B_HEAD_DIM, D_STATE),
        ctail_p[:, SUBLANES - (CONV_W - 1):, :].reshape(1, bp, CONV_W - 1, -1),
        ssm_s.reshape(1, ns, B_HEADS, B_HEAD_DIM, D_STATE),
        cnew_s.reshape(1, ns, CONV_W - 1, -1),
        v_s.reshape(1, ns, 1, -1),
    )
```

```python
import functools
import math

import jax
import jax.numpy as jnp
from jax import lax
from jax.experimental import pallas as pl
from jax.experimental.pallas import tpu as pltpu

F32 = jnp.float32
BF16 = jnp.bfloat16
NEG_INF = float("-inf")

EPS = 1e-6
A_HEADS = 8
A_HEAD_DIM = 128
CHUNK = 128
B_HEADS = 16
B_HEAD_DIM = 64
B_GROUPS = 2
D_STATE = 128
CONV_W = 4
N_KEYS = 128
R_HEADS = 8
TOPK = 16
N_MOD = 6

LANES = 128
SUBLANES = 8
VMEM_LIMIT = 56 * 1024 * 1024

MIX_ROWS = 512
PEER_TOKENS = 256
PEER_EXPERTS = 2048
PEER_SELECT_STRIP = 256
LOG2E = 1.4426950408889634
S2_TOKENS = 8


def _dot(a, b):
    return jnp.dot(a, b, preferred_element_type=F32)


def _dot_nt(a, b):
    return lax.dot_general(a, b, (((1,), (1,)), ((), ())), preferred_element_type=F32)


def _dot_tn(a, b):
    return lax.dot_general(a, b, (((0,), (0,)), ((), ())), preferred_element_type=F32)


def _split2(x):
    hi = x.astype(BF16)
    lo = (x - hi.astype(F32)).astype(BF16)
    return hi, lo


def _split3(x):
    hi = x.astype(BF16)
    r = x - hi.astype(F32)
    mid = r.astype(BF16)
    lo = (r - mid.astype(F32)).astype(BF16)
    return hi, mid, lo


def _dot_x3(a, b):
    a1, a2 = _split2(a)
    b1, b2 = _split2(b)
    return _dot(a1, b1) + (_dot(a1, b2) + _dot(a2, b1))


def _silu(x):
    return x / (1.0 + jnp.exp(-x))


def _gelu(x):
    return 0.5 * x * (1.0 + lax.erf(x * (1.0 / math.sqrt(2.0))))


def _softplus(x):
    return jnp.maximum(x, 0.0) + jnp.log1p(jnp.exp(-jnp.abs(x)))


def _rmsnorm(x, g):
    return x * lax.rsqrt(jnp.mean(x * x, axis=-1, keepdims=True) + EPS) * g


def _full(shape):
    nd = len(shape)
    return pl.BlockSpec(shape, lambda *_: (0,) * nd, pipeline_mode=pl.Buffered(1))


def _dot2(a, w_ref):
    h = w_ref.shape[1] // 2
    return jnp.concatenate([_dot(a, w_ref[:, :h]), _dot(a, w_ref[:, h:])], axis=1)


def _params(sem):
    return pltpu.CompilerParams(dimension_semantics=sem, vmem_limit_bytes=VMEM_LIMIT)


def _ada_kernel(c_ref, w_ref, b_ref, o_ref):
    o_ref[...] = _dot_x3(_silu(c_ref[...]), w_ref[...]) + b_ref[...]


def _ada_call(c, w, b, tn=512):
    m, k = c.shape
    n = w.shape[1]
    return pl.pallas_call(
        _ada_kernel,
        grid=(n // tn,),
        in_specs=[_full((m, k)), pl.BlockSpec((k, tn), lambda j: (0, j)), pl.BlockSpec((1, tn), lambda j: (0, j))],
        out_specs=pl.BlockSpec((m, tn), lambda j: (0, j)),
        out_shape=jax.ShapeDtypeStruct((m, n), F32),
        compiler_params=_params(("parallel",)),
        name="ada",
    )(c, w, b.reshape(1, n))


def _gate_mlp_chunk(u, v, gv, ws_ref, bs_ref, causal):
    ys, vs = [], []
    for g in range(A_HEADS):
        sl = slice(g * A_HEAD_DIM, (g + 1) * A_HEAD_DIM)
        ug = _gelu(u[:, sl])
        vn = _rmsnorm(_gelu(v[:, sl]), gv[:, sl])
        w = jnp.where(causal, ws_ref[g], 0.0).astype(BF16)
        s = _dot(w, vn.astype(BF16)) + bs_ref[g]
        ys.append((ug * s).astype(BF16))
        vs.append(vn)
    return ys, vs


def _mix_kernel(x_ref, mod_ref, gmix_ref, wuv_ref, wz_ref, wxbc_ref, wdt_ref, wdtT_ref,
                dtb_ref, dtbT_ref, alog_ref, alogT_ref, exp_ref, gv_ref, ws_ref, bs_ref,
                cw_ref, cb_ref, dsk_ref, gssm_ref, wout_ref,
                o_ref, ssm_ref, ctail_ref,
                hst_ref, tail_ref, uv_ref, z_ref, xs_ref, dt_ref, dtT_ref, ycat_ref):
    t = pl.program_id(1)
    tm = x_ref.shape[0]
    n_chunks = tm // CHUNK

    @pl.when(t == 0)
    def _():
        hst_ref[...] = jnp.zeros_like(hst_ref)
        tail_ref[...] = jnp.zeros_like(tail_ref)

    x = x_ref[...]
    h = _rmsnorm(x, gmix_ref[...]) * (1.0 + mod_ref[1]) + mod_ref[0]
    hb = h.astype(BF16)
    uv_ref[...] = _dot2(hb, wuv_ref)
    z_ref[...] = _dot2(hb, wz_ref)
    dt_ref[...] = _softplus(_dot(hb, wdt_ref[...]) + dtb_ref[...])
    dtT_ref[...] = _softplus(_dot_nt(wdtT_ref[...], hb) + dtbT_ref[...])

    cur = _dot2(hb, wxbc_ref)
    prev = tail_ref[...]
    row8 = lax.broadcasted_iota(jnp.int32, (SUBLANES, 1), 0)
    acc = cb_ref[...] + cur * cw_ref[CONV_W - 1:CONV_W, :]
    for k in range(1, CONV_W):
        r = pltpu.roll(cur, k, axis=0)
        head = jnp.where(row8 < k, pltpu.roll(prev, k, axis=0), r[0:SUBLANES])
        shifted = jnp.concatenate([head, r[SUBLANES:]], axis=0)
        acc = acc + shifted * cw_ref[CONV_W - 1 - k:CONV_W - k, :]
    xs_ref[...] = _silu(acc)
    new_tail = cur[tm - SUBLANES:tm]
    tail_ref[...] = new_tail
    ctail_ref[...] = new_tail

    ri = lax.broadcasted_iota(jnp.int32, (CHUNK, CHUNK), 0)
    ci = lax.broadcasted_iota(jnp.int32, (CHUNK, CHUNK), 1)
    causal = ri >= ci
    tril = jnp.where(causal, 1.0, 0.0).astype(BF16)
    triu = jnp.where(ri <= ci, 1.0, 0.0).astype(BF16)
    low_half = ri < B_HEAD_DIM
    low_lane = ci < B_HEAD_DIM
    a_col = -jnp.exp(alog_ref[...])
    a_row = -jnp.exp(alogT_ref[...])

    def chunk_body(c, carry):
        r0 = pl.multiple_of(c * CHUNK, CHUNK)
        rows = pl.ds(r0, CHUNK)

        ys, _ = _gate_mlp_chunk(uv_ref[rows, 0:1024], uv_ref[rows, 1024:2048], gv_ref[...], ws_ref, bs_ref, causal)
        for g in range(A_HEADS):
            ycat_ref[rows, g * A_HEAD_DIM:(g + 1) * A_HEAD_DIM] = ys[g]

        dt = dt_ref[rows, :]
        d_a = dt * a_col
        h1, h2, h3 = _split3(d_a)
        acum = _dot(tril, h1) + (_dot(tril, h2) + _dot(tril, h3))
        e1, e2, e3 = _split3(acum)
        acum_bc = _dot(e1, exp_ref[...]) + (_dot(e2, exp_ref[...]) + _dot(e3, exp_ref[...]))
        f1, f2 = _split2(dt)
        dt_bc = _dot(f1, exp_ref[...]) + _dot(f2, exp_ref[...])
        dt_t = dtT_ref[:, rows]
        g1, g2, g3 = _split3(dt_t * a_row)
        acum_t = _dot(g1, triu) + (_dot(g2, triu) + _dot(g3, triu))

        xs = xs_ref[rows, 0:1024]
        zz = z_ref[rows, :]
        y_pairs = []
        for g in range(B_GROUPS):
            bg = xs_ref[rows, 1024 + g * D_STATE:1024 + (g + 1) * D_STATE]
            cg = xs_ref[rows, 1280 + g * D_STATE:1280 + (g + 1) * D_STATE]
            cbm = _dot_nt(cg.astype(BF16), bg.astype(BF16))
            for pr in range(B_HEADS // B_GROUPS // 2):
                pair = g * 4 + pr
                xpair = xs[:, pair * 128:(pair + 1) * 128]
                hprev = hst_ref[pair]
                y_acc = None
                st_acc = None
                alasts = []
                for e in range(2):
                    hd = pair * 2 + e
                    ab = acum_bc[:, hd * 128:(hd + 1) * 128]
                    seg = ab - acum_t[hd:hd + 1, :]
                    lm = jnp.exp(jnp.where(causal, seg, NEG_INF))
                    gd = (cbm * lm * dt_t[hd:hd + 1, :]).astype(BF16)
                    sel_l = low_lane if e == 0 else jnp.logical_not(low_lane)
                    sel_r = low_half if e == 0 else jnp.logical_not(low_half)
                    xh = jnp.where(sel_l, xpair, 0.0).astype(BF16)
                    hp = jnp.where(sel_r, hprev, 0.0).astype(BF16)
                    cs = (cg * jnp.exp(ab)).astype(BF16)
                    alast = ab[CHUNK - 1:CHUNK, :]
                    bsc = (bg * (jnp.exp(alast - ab) * dt_bc[:, hd * 128:(hd + 1) * 128])).astype(BF16)
                    y_e = _dot(gd, xh) + _dot_nt(cs, hp)
                    st_e = _dot_tn(xh, bsc)
                    y_acc = y_e if y_acc is None else y_acc + y_e
                    st_acc = st_e if st_acc is None else st_acc + st_e
                    alasts.append(alast)
                decay = jnp.exp(jnp.where(low_half, alasts[0], alasts[1]))
                hst_ref[pair] = hprev * decay + st_acc
                y_pairs.append(y_acc)
        y = jnp.concatenate(y_pairs, axis=1) + dsk_ref[...] * xs
        y = y * _silu(zz)
        half = B_HEADS * B_HEAD_DIM // B_GROUPS
        for g in range(B_GROUPS):
            sl = slice(g * half, (g + 1) * half)
            ycat_ref[rows, 1024 + g * half:1024 + (g + 1) * half] = _rmsnorm(y[:, sl], gssm_ref[:, sl]).astype(BF16)
        return carry

    lax.fori_loop(0, n_chunks, chunk_body, 0)

    o_ref[...] = x + mod_ref[2] * _dot2(ycat_ref[...], wout_ref)

    @pl.when(t == pl.num_programs(1) - 1)
    def _():
        ssm_ref[...] = hst_ref[...]


def _mix_call(x, mod1, p, tm=MIX_ROWS):
    b, s, d = x.shape
    n_pairs = B_HEADS // 2
    in_specs = [
        pl.BlockSpec((None, tm, d), lambda i, t: (i, t, 0)),
        pl.BlockSpec((None, 3, 1, d), lambda i, t: (i, 0, 0, 0)),
        _full((1, d)),
        _full(p["w_uv"].shape), _full(p["w_z"].shape), _full(p["w_xbc"].shape),
        _full(p["w_dt"].shape), _full(p["w_dtT"].shape),
        _full((1, LANES)), _full((LANES, 1)), _full((1, LANES)), _full((LANES, LANES)),
        _full(p["expand"].shape),
        _full((1, 1024)), _full((A_HEADS, CHUNK, CHUNK)), _full((A_HEADS, CHUNK, CHUNK)),
        _full((CONV_W, 1536)), _full((1, 1536)), _full((1, 1024)), _full((1, 1024)),
        _full(p["w_out"].shape),
    ]
    out_specs = [
        pl.BlockSpec((None, tm, d), lambda i, t: (i, t, 0)),
        pl.BlockSpec((None, n_pairs, 128, D_STATE), lambda i, t: (i, 0, 0, 0)),
        pl.BlockSpec((None, SUBLANES, 1536), lambda i, t: (i, 0, 0)),
    ]
    out_shape = [
        jax.ShapeDtypeStruct((b, s, d), F32),
        jax.ShapeDtypeStruct((b, n_pairs, 128, D_STATE), F32),
        jax.ShapeDtypeStruct((b, SUBLANES, 1536), F32),
    ]
    scratch = [
        pltpu.VMEM((n_pairs, 128, D_STATE), F32),
        pltpu.VMEM((SUBLANES, 1536), F32),
        pltpu.VMEM((tm, 2048), F32),
        pltpu.VMEM((tm, 1024), F32),
        pltpu.VMEM((tm, 1536), F32),
        pltpu.VMEM((tm, LANES), F32),
        pltpu.VMEM((LANES, tm), F32),
        pltpu.VMEM((tm, 2048), BF16),
    ]
    return pl.pallas_call(
        _mix_kernel,
        grid=(b, s // tm),
        in_specs=in_specs, out_specs=out_specs, out_shape=out_shape, scratch_shapes=scratch,
        compiler_params=_params(("parallel", "arbitrary")),
        name="mix_prompt",
    )(x, mod1, p["g_mix"], p["w_uv"], p["w_z"], p["w_xbc"], p["w_dt"], p["w_dtT"],
      p["dt_bias_row"], p["dt_bias_col"], p["a_log_row"], p["a_log_sq"], p["expand"],
      p["g_v"], p["w_s"], p["b_s_rep"], p["conv_w"], p["conv_b"], p["d_skip_exp"], p["g_ssm"], p["w_out"])


def _s1_kernel(x_ref, mod_ref, gmix_ref, wuv_ref, wz_ref, wxbc_ref, wdt_ref, dtb_ref, alog_ref, exp64_ref,
               gv_ref, ws0_ref, bs0_ref, cw_ref, cb_ref, sconv_ref,
               v_ref, cnew_ref, ya_ref, z_ref, xs_ref, bc_ref, xdtT_ref, dec_ref):
    x = x_ref[...]
    h = _rmsnorm(x, gmix_ref[...]) * (1.0 + mod_ref[1]) + mod_ref[0]
    hb = h.astype(BF16)
    uv = _dot(hb, wuv_ref[...])
    gv = gv_ref[...]
    for g in range(A_HEADS):
        sl = slice(g * A_HEAD_DIM, (g + 1) * A_HEAD_DIM)
        ug = _gelu(uv[:, sl])
        vn = _rmsnorm(_gelu(uv[:, 1024 + g * A_HEAD_DIM:1024 + (g + 1) * A_HEAD_DIM]), gv[:, sl])
        v_ref[:, sl] = vn
        s = ws0_ref[:, sl] * vn + bs0_ref[:, sl]
        ya_ref[:, sl] = (ug * s).astype(BF16)
    z_ref[...] = _dot(hb, wz_ref[...])
    raw = _dot(hb, wxbc_ref[...])
    cd = raw.shape[1]
    acc = cb_ref[...] + raw * cw_ref[CONV_W - 1:CONV_W, :]
    for k in range(CONV_W - 1):
        acc = acc + sconv_ref[:, k * cd:(k + 1) * cd] * cw_ref[k:k + 1, :]
    cnew_ref[:, 0:(CONV_W - 2) * cd] = sconv_ref[:, cd:(CONV_W - 1) * cd]
    cnew_ref[:, (CONV_W - 2) * cd:(CONV_W - 1) * cd] = raw
    xbc = _silu(acc)
    xs = xbc[:, 0:1024]
    xs_ref[...] = xs
    bc_ref[...] = xbc[:, 1024:1536]
    dt = _softplus(_dot(hb, wdt_ref[...]) + dtb_ref[...])
    dec_ref[...] = jnp.exp(dt * (-jnp.exp(alog_ref[...])))
    f1, f2, f3 = _split3(dt)
    dt64 = _dot(f1, exp64_ref[...]) + (_dot(f2, exp64_ref[...]) + _dot(f3, exp64_ref[...]))
    xdtT_ref[...] = (xs * dt64).T.astype(BF16)


def _s1_call(x, mod1, sconv, p):
    n, d = x.shape
    outs = [
        jax.ShapeDtypeStruct((n, 1024), F32),
        jax.ShapeDtypeStruct((n, 3 * 1536), F32),
        jax.ShapeDtypeStruct((n, 1024), BF16),
        jax.ShapeDtypeStruct((n, 1024), F32),
        jax.ShapeDtypeStruct((n, 1024), F32),
        jax.ShapeDtypeStruct((n, 512), F32),
        jax.ShapeDtypeStruct((1024, n), BF16),
        jax.ShapeDtypeStruct((n, LANES), F32),
    ]
    args = (x, mod1, p["g_mix"], p["w_uv"], p["w_z"], p["w_xbc"], p["w_dt"], p["dt_bias_row"], p["a_log_row"],
            p["expand64"], p["g_v"], p["w_s00"], p["b_s0"], p["conv_w"], p["conv_b"], sconv)
    return pl.pallas_call(
        _s1_kernel,
        grid=(1,),
        in_specs=[_full(a.shape) for a in args],
        out_specs=[_full(o.shape) for o in outs],
        out_shape=outs,
        compiler_params=_params(("arbitrary",)),
        name="mix_sample_in",
    )(*args)


def _s2_kernel(dec_ref, st_ref, xdtT_ref, bc_ref, cblk_ref, o_ref, y_ref):
    i = pl.program_id(0)
    bt = st_ref.shape[0]
    n = bc_ref.shape[0]
    half = B_HEADS * B_HEAD_DIM // B_GROUPS
    rowi = lax.broadcasted_iota(jnp.int32, (n, 1), 0)
    for bb in range(bt):
        b = i * bt + bb
        for g in range(B_GROUPS):
            bm = bc_ref[:, g * D_STATE:(g + 1) * D_STATE]
            rb = jnp.where(rowi == b, bm, 0.0).astype(BF16)
            outer = _dot(xdtT_ref[g * half:(g + 1) * half, :], rb)
            for k in range(B_HEADS // B_GROUPS):
                hd = g * (B_HEADS // B_GROUPS) + k
                sl = slice(k * B_HEAD_DIM, (k + 1) * B_HEAD_DIM)
                o_ref[bb, g, sl, :] = st_ref[bb, g, sl, :] * dec_ref[b, hd] + outer[sl, :]
            crow = cblk_ref[bb:bb + 1, 256 + g * D_STATE:256 + (g + 1) * D_STATE]
            c8 = jnp.broadcast_to(crow, (SUBLANES, D_STATE)).astype(BF16)
            yr = _dot_nt(c8, o_ref[bb, g].astype(BF16))
            y_ref[bb:bb + 1, g * half:(g + 1) * half] = yr[0:1]


def _s2_call(dec, state, xdtT, bc, bt=S2_TOKENS):
    n = state.shape[0]
    half = B_HEADS * B_HEAD_DIM // B_GROUPS
    st = state.reshape(n, B_GROUPS, half, D_STATE)
    grid_spec = pltpu.PrefetchScalarGridSpec(
        num_scalar_prefetch=1,
        grid=(n // bt,),
        in_specs=[
            pl.BlockSpec((bt, B_GROUPS, half, D_STATE), lambda i, d: (i, 0, 0, 0)),
            pl.BlockSpec(xdtT.shape, lambda i, d: (0, 0)),
            pl.BlockSpec(bc.shape, lambda i, d: (0, 0)),
            pl.BlockSpec((bt, bc.shape[1]), lambda i, d: (i, 0)),
        ],
        out_specs=[
            pl.BlockSpec((bt, B_GROUPS, half, D_STATE), lambda i, d: (i, 0, 0, 0)),
            pl.BlockSpec((bt, 1024), lambda i, d: (i, 0)),
        ],
    )
    new_state, y = pl.pallas_call(
        _s2_kernel,
        grid_spec=grid_spec,
        out_shape=[jax.ShapeDtypeStruct(st.shape, F32), jax.ShapeDtypeStruct((n, 1024), F32)],
        compiler_params=_params(("arbitrary",)),
        name="mix_sample_state",
    )(dec, st, xdtT, bc, bc)
    return new_state.reshape(state.shape), y


def _s3_kernel(x_ref, mod_ref, y_ref, xs_ref, z_ref, ya_ref, dsk_ref, gssm_ref, wout_ref, o_ref):
    y = (y_ref[...] + dsk_ref[...] * xs_ref[...]) * _silu(z_ref[...])
    half = B_HEADS * B_HEAD_DIM // B_GROUPS
    parts = [ya_ref[...]]
    for g in range(B_GROUPS):
        sl = slice(g * half, (g + 1) * half)
        parts.append(_rmsnorm(y[:, sl], gssm_ref[:, sl]).astype(BF16))
    cat = jnp.concatenate(parts, axis=1)
    o_ref[...] = x_ref[...] + mod_ref[2] * _dot(cat, wout_ref[...])


def _s3_call(x, mod1, y, xs, z, ya, p):
    args = (x, mod1, y, xs, z, ya, p["d_skip_exp"], p["g_ssm"], p["w_out"])
    return pl.pallas_call(
        _s3_kernel,
        grid=(1,),
        in_specs=[_full(a.shape) for a in args],
        out_specs=_full(x.shape),
        out_shape=jax.ShapeDtypeStruct(x.shape, F32),
        compiler_params=_params(("arbitrary",)),
        name="mix_sample_out",
    )(*args)


def _cand_blocks(a, b):
    row8 = lax.broadcasted_iota(jnp.int32, (SUBLANES, 1), 0)
    a8 = a[0:SUBLANES]
    blocks = []
    for j in range(SUBLANES):
        blk = a8 + b[j:j + 1]
        cnt = min(SUBLANES, TOPK // (j + 1))
        if cnt < SUBLANES:
            blk = jnp.where(row8 < cnt, blk, NEG_INF)
        blocks.append(blk)
    blocks.append(a[0:1] + b[SUBLANES:TOPK])
    blocks.append(a[SUBLANES:TOPK] + b[0:1])
    return blocks


def _merge_exchange_pairs(n):
    pairs = []
    t = max(1, math.ceil(math.log2(n)))
    p = 1 << (t - 1)
    while p > 0:
        q, r, d = 1 << (t - 1), 0, p
        while d > 0:
            pairs.extend((i, i + d) for i in range(n - d) if (i & p) == r)
            d, q, r = q - p, q >> 1, p
        p >>= 1
    return pairs


def _sort_blocks_desc(blocks):
    blocks = list(blocks)
    for i, j in _merge_exchange_pairs(len(blocks)):
        hi = jnp.maximum(blocks[i], blocks[j])
        blocks[j] = jnp.minimum(blocks[i], blocks[j])
        blocks[i] = hi
    return blocks


def _pop_largest(blocks, k):
    blocks = list(blocks)
    nb = len(blocks)
    rows = []
    for t in range(k):
        m = jnp.max(blocks[0], axis=0, keepdims=True)
        rows.append(m)
        need = min(k - t - 1, nb)
        if need == 0:
            break
        sel = blocks[0] == m
        for j in range(need):
            nxt = blocks[j + 1] if j + 1 < nb else NEG_INF
            blocks[j] = jnp.where(sel, nxt, blocks[j])
    return rows


def _peer_kernel(x_ref, mod_ref, gffn_ref, gfin_ref, wqT_ref, keys_ref, u_ref, vt_ref, y_ref,
                 h2_ref, q_ref, s1_ref, s2_ref, thr_ref, tops_ref, act_ref, coef_ref, outT_ref,
                 *, n_chunks):
    c = pl.program_id(1)
    n_blk = u_ref.shape[0] // N_KEYS

    @pl.when(c == 0)
    def _():
        x = x_ref[...]
        h = _rmsnorm(x, gffn_ref[...]) * (1.0 + mod_ref[1]) + mod_ref[0]
        ht = h.T.astype(BF16)
        h2_ref[...] = ht
        qh = wqT_ref.shape[0] // 2
        q_ref[0:qh, :] = _dot(wqT_ref[0:qh, :], ht)
        q_ref[qh:, :] = _dot(wqT_ref[qh:, :], ht)
        outT_ref[...] = jnp.zeros_like(outT_ref)

        def one_head(hh, top_ref):
            scores = []
            for s in range(2):
                r0 = pl.multiple_of((2 * hh + s) * N_KEYS, N_KEYS)
                qs = q_ref[pl.ds(r0, N_KEYS), :].astype(BF16)
                sc = _dot(keys_ref[2 * hh + s], qs) * LOG2E
                if s == 0:
                    s1_ref[hh] = sc
                else:
                    s2_ref[hh] = sc
                scores.append(sc)
            n_tok = x_ref.shape[0]
            strip = min(n_tok, PEER_SELECT_STRIP)
            for c0 in range(0, n_tok, strip):
                cols = slice(c0, c0 + strip)
                for s in range(2):
                    srt = _sort_blocks_desc([scores[s][j * SUBLANES:(j + 1) * SUBLANES, cols]
                                             for j in range(N_KEYS // SUBLANES)])
                    for k, m in enumerate(_pop_largest(srt, TOPK)):
                        top_ref[s, k:k + 1, cols] = m
                a = top_ref[0, :, cols]
                b = top_ref[1, :, cols]
                best = a[0:1] + b[0:1]
                zsum = None
                for m in _pop_largest(_sort_blocks_desc(_cand_blocks(a, b)), TOPK):
                    e = jnp.exp2(m - best)
                    zsum = e if zsum is None else zsum + e
                shift = best + jnp.log2(zsum)
                s1_ref[hh, :, cols] = s1_ref[hh, :, cols] - shift
                thr_ref[hh, :, cols] = _pop_largest(_sort_blocks_desc(_cand_blocks(a - shift, b)), TOPK)[-1]

        def head_body(hp, carry):
            for e in range(2):
                one_head(2 * hp + e, tops_ref.at[e])
            return carry

        lax.fori_loop(0, R_HEADS // 2, head_body, 0)

    eh = u_ref.shape[0] // 2
    dh = vt_ref.shape[0] // 2

    @pl.when(c < n_chunks)
    def _():
        act_w = act_ref.at[c % 2]
        act_w[0:eh, :] = _dot(u_ref[0:eh, :], h2_ref[...])
        act_w[eh:, :] = _dot(u_ref[eh:, :], h2_ref[...])

    @pl.when(c >= 1)
    def _():
        act_r = act_ref.at[(c - 1) % 2]
        i1 = pl.multiple_of((c - 1) * n_blk, n_blk)
        s1_rows = [s1_ref[hh, pl.ds(i1, n_blk), :] for hh in range(R_HEADS)]
        for r in range(n_blk):
            acc = None
            for hh in range(R_HEADS):
                val = s1_rows[hh][r:r + 1, :] + s2_ref[hh]
                gate = jnp.where(val >= thr_ref[hh], jnp.exp2(val), 0.0)
                acc = gate if acc is None else acc + gate
            blk = slice(r * N_KEYS, (r + 1) * N_KEYS)
            coef_ref[blk, :] = (acc * _gelu(act_r[blk, :])).astype(BF16)
        outT_ref[0:dh, :] += _dot(vt_ref[0:dh, :], coef_ref[...])
        outT_ref[dh:, :] += _dot(vt_ref[dh:, :], coef_ref[...])

    @pl.when(c == n_chunks)
    def _():
        x2 = x_ref[...] + mod_ref[2] * outT_ref[...].T
        y_ref[...] = _rmsnorm(x2, gfin_ref[...]) * (1.0 + mod_ref[4]) + mod_ref[3]


def _peer_call(x, mod5, tiles_per_row, p, tl, ec=PEER_EXPERTS):
    t, d = x.shape
    n_exp = p["expert_u"].shape[0]
    r = mod5.shape[2]
    scratch = [
        pltpu.VMEM((d, tl), BF16),
        pltpu.VMEM((R_HEADS * 2 * N_KEYS, tl), F32),
        pltpu.VMEM((R_HEADS, N_KEYS, tl), F32),
        pltpu.VMEM((R_HEADS, N_KEYS, tl), F32),
        pltpu.VMEM((R_HEADS, 1, tl), F32),
        pltpu.VMEM((2, 2, TOPK, tl), F32),
        pltpu.VMEM((2, ec, tl), F32),
        pltpu.VMEM((ec, tl), BF16),
        pltpu.VMEM((d, tl), F32),
    ]
    n_chunks = n_exp // ec
    return pl.pallas_call(
        functools.partial(_peer_kernel, n_chunks=n_chunks),
        grid=(t // tl, n_chunks + 1),
        in_specs=[
            pl.BlockSpec((tl, d), lambda i, c: (i, 0)),
            pl.BlockSpec((None, 5, r, d), lambda i, c: (i // tiles_per_row, 0, 0, 0)),
            _full((1, d)), _full((1, d)),
            _full(p["w_qT"].shape), _full(p["keys"].shape),
            pl.BlockSpec((ec, d), lambda i, c: (jnp.minimum(c, n_chunks - 1), 0)),
            pl.BlockSpec((d, ec), lambda i, c: (0, jnp.maximum(c - 1, 0))),
        ],
        out_specs=pl.BlockSpec((tl, d), lambda i, c: (i, 0)),
        out_shape=jax.ShapeDtypeStruct((t, d), F32),
        scratch_shapes=scratch,
        compiler_params=_params(("parallel", "arbitrary")),
        name="peer",
    )(x, mod5, p["g_ffn"], p["g_final"], p["w_qT"], p["keys"], p["expert_u"], p["expert_vT"])


def _prep(w_in, g_mix, g_v, w_s, b_s, conv_w, conv_b, dt_bias, a_log, d_skip, g_ssm, w_out, g_ffn, w_q,
          sub_keys, expert_u, expert_v, g_final):
    d = w_in.shape[0]
    o_z = 2 * 1024
    o_x = o_z + 1024
    o_dt = o_x + 1536
    w_dt = jnp.zeros((d, LANES), F32).at[:, :B_HEADS].set(w_in[:, o_dt:o_dt + B_HEADS])
    pad16 = lambda v: jnp.zeros((LANES,), F32).at[:B_HEADS].set(v)
    heads = jnp.arange(LANES)[:, None]
    p = {
        "g_mix": g_mix.reshape(1, d),
        "w_uv": w_in[:, 0:o_z].astype(BF16),
        "w_z": w_in[:, o_z:o_x].astype(BF16),
        "w_xbc": w_in[:, o_x:o_dt].astype(BF16),
        "w_dt": w_dt.astype(BF16),
        "w_dtT": w_dt.T.astype(BF16),
        "dt_bias_row": pad16(dt_bias).reshape(1, LANES),
        "dt_bias_col": pad16(dt_bias).reshape(LANES, 1),
        "a_log_row": pad16(a_log).reshape(1, LANES),
        "a_log_sq": jnp.broadcast_to(pad16(a_log).reshape(LANES, 1), (LANES, LANES)),
        "expand": (heads == jnp.arange(B_HEADS * LANES)[None, :] // LANES).astype(BF16),
        "expand64": (heads == jnp.arange(B_HEADS * B_HEAD_DIM)[None, :] // B_HEAD_DIM).astype(BF16),
        "g_v": g_v.reshape(1, -1),
        "w_s": w_s,
        "b_s_rep": jnp.broadcast_to(b_s[:, :, None], (A_HEADS, CHUNK, CHUNK)),
        "w_s00": jnp.repeat(w_s[:, 0, 0], A_HEAD_DIM).reshape(1, -1),
        "b_s0": jnp.repeat(b_s[:, 0], A_HEAD_DIM).reshape(1, -1),
        "conv_w": conv_w,
        "conv_b": conv_b.reshape(1, -1),
        "d_skip_exp": jnp.repeat(d_skip, B_HEAD_DIM).reshape(1, -1),
        "g_ssm": g_ssm.reshape(1, -1),
        "w_out": w_out.astype(BF16),
        "g_ffn": g_ffn.reshape(1, d),
        "g_final": g_final.reshape(1, d),
        "w_qT": w_q.T.astype(BF16),
        "keys": sub_keys.reshape(R_HEADS * 2, N_KEYS, -1).astype(BF16),
        "expert_u": expert_u.astype(BF16),
        "expert_vT": expert_v.T.astype(BF16),
    }
    return p


def kernel(x_prompt, x_sample, c_prompt, c_sample, state_ssm, state_conv, w_ada, b_ada, g_mix, w_in, g_v, w_s, b_s,
           conv_w, conv_b, dt_bias, a_log, d_skip, g_ssm, w_out, g_ffn, w_q, sub_keys, expert_u, expert_v,
           w_ada_f, b_ada_f, g_final):
    assert w_ada.shape[0] == 1, "single-layer trunk"
    bp, seq, d = x_prompt.shape
    ns = x_sample.shape[0]
    p = _prep(w_in[0], g_mix[0], g_v[0], w_s[0], b_s[0], conv_w[0], conv_b[0], dt_bias[0], a_log[0], d_skip[0],
              g_ssm[0], w_out[0], g_ffn[0], w_q[0], sub_keys[0], expert_u[0], expert_v[0], g_final)

    c_all = jnp.concatenate([c_prompt, c_sample], axis=0)
    mod = _ada_call(c_all, w_ada[0], b_ada[0]).reshape(bp + ns, N_MOD, d)
    modf = _ada_call(c_all, w_ada_f, b_ada_f).reshape(bp + ns, 2, d)
    mod5 = jnp.concatenate([mod[:, 3:6], modf], axis=1)

    x1_p, ssm_p, ctail_p = _mix_call(x_prompt, mod[:bp, 0:3].reshape(bp, 3, 1, d), p)
    y_p = _peer_call(x1_p.reshape(bp * seq, d), mod5[:bp].reshape(bp, 5, 1, d), seq // PEER_TOKENS, p, PEER_TOKENS)

    xs_in = x_sample.reshape(ns, d)
    mod1_s = jnp.transpose(mod[bp:, 0:3], (1, 0, 2))
    sconv = state_conv[0].reshape(ns, -1)
    v_s, cnew_s, ya_s, z_s, xc_s, bc_s, xdtT_s, dec_s = _s1_call(xs_in, mod1_s, sconv, p)
    ssm_s, yssd_s = _s2_call(dec_s[:, :B_HEADS], state_ssm[0], xdtT_s, bc_s)
    x1_s = _s3_call(xs_in, mod1_s, yssd_s, xc_s, z_s, ya_s, p)
    mod5_s = jnp.transpose(mod5[bp:], (1, 0, 2)).reshape(1, 5, ns, d)
    y_s = _peer_call(x1_s, mod5_s, 1, p, ns)

    return (
        y_p.reshape(bp, seq, d),
        y_s.reshape(ns, 1, d),
        ssm_p.reshape(1, bp, B_HEADS, B_HEAD_DIM, D_STATE),
        ctail_p[:, SUBLANES - (CONV_W - 1):, :].reshape(1, bp, CONV_W - 1, -1),
        ssm_s.reshape(1, ns, B_HEADS, B_HEAD_DIM, D_STATE),
        cnew_s.reshape(1, ns, CONV_W - 1, -1),
        v_s.reshape(1, ns, 1, -1),
    )
```

```python
import functools
import math

import jax
import jax.numpy as jnp
from jax import lax
from jax.experimental import pallas as pl
from jax.experimental.pallas import tpu as pltpu

F32 = jnp.float32
BF16 = jnp.bfloat16
NEG_INF = float("-inf")

EPS = 1e-6
A_HEADS = 8
A_HEAD_DIM = 128
CHUNK = 128
B_HEADS = 16
B_HEAD_DIM = 64
B_GROUPS = 2
D_STATE = 128
CONV_W = 4
N_KEYS = 128
R_HEADS = 8
TOPK = 16
N_MOD = 6

LANES = 128
SUBLANES = 8
VMEM_LIMIT = 56 * 1024 * 1024

MIX_ROWS = 512
PEER_TOKENS = 256
PEER_EXPERTS = 2048
PEER_SELECT_STRIP = 256
LOG2E = 1.4426950408889634
S2_TOKENS = 8


def _dot(a, b):
    return jnp.dot(a, b, preferred_element_type=F32)


def _dot_nt(a, b):
    return lax.dot_general(a, b, (((1,), (1,)), ((), ())), preferred_element_type=F32)


def _dot_tn(a, b):
    return lax.dot_general(a, b, (((0,), (0,)), ((), ())), preferred_element_type=F32)


def _split2(x):
    hi = x.astype(BF16)
    lo = (x - hi.astype(F32)).astype(BF16)
    return hi, lo


def _split3(x):
    hi = x.astype(BF16)
    r = x - hi.astype(F32)
    mid = r.astype(BF16)
    lo = (r - mid.astype(F32)).astype(BF16)
    return hi, mid, lo


def _dot_x3(a, b):
    a1, a2 = _split2(a)
    b1, b2 = _split2(b)
    return _dot(a1, b1) + (_dot(a1, b2) + _dot(a2, b1))


def _silu(x):
    return x / (1.0 + jnp.exp(-x))


def _gelu(x):
    return 0.5 * x * (1.0 + lax.erf(x * (1.0 / math.sqrt(2.0))))


def _softplus(x):
    return jnp.maximum(x, 0.0) + jnp.log1p(jnp.exp(-jnp.abs(x)))


def _rmsnorm(x, g):
    return x * lax.rsqrt(jnp.mean(x * x, axis=-1, keepdims=True) + EPS) * g


def _full(shape):
    nd = len(shape)
    return pl.BlockSpec(shape, lambda *_: (0,) * nd, pipeline_mode=pl.Buffered(1))


def _dot2(a, w_ref):
    h = w_ref.shape[1] // 2
    return jnp.concatenate([_dot(a, w_ref[:, :h]), _dot(a, w_ref[:, h:])], axis=1)


def _params(sem):
    return pltpu.CompilerParams(dimension_semantics=sem, vmem_limit_bytes=VMEM_LIMIT)


def _ada_kernel(c_ref, w_ref, b_ref, o_ref):
    o_ref[...] = _dot_x3(_silu(c_ref[...]), w_ref[...]) + b_ref[...]


def _ada_call(c, w, b, tn=512):
    m, k = c.shape
    n = w.shape[1]
    return pl.pallas_call(
        _ada_kernel,
        grid=(n // tn,),
        in_specs=[_full((m, k)), pl.BlockSpec((k, tn), lambda j: (0, j)), pl.BlockSpec((1, tn), lambda j: (0, j))],
        out_specs=pl.BlockSpec((m, tn), lambda j: (0, j)),
        out_shape=jax.ShapeDtypeStruct((m, n), F32),
        compiler_params=_params(("parallel",)),
        name="ada",
    )(c, w, b.reshape(1, n))


def _gate_mlp_chunk(u, v, gv, ws_ref, bs_ref, causal):
    ys, vs = [], []
    for g in range(A_HEADS):
        sl = slice(g * A_HEAD_DIM, (g + 1) * A_HEAD_DIM)
        ug = _gelu(u[:, sl])
        vn = _rmsnorm(_gelu(v[:, sl]), gv[:, sl])
        w = jnp.where(causal, ws_ref[g], 0.0).astype(BF16)
        s = _dot(w, vn.astype(BF16)) + bs_ref[g]
        ys.append((ug * s).astype(BF16))
        vs.append(vn)
    return ys, vs


def _mix_kernel(x_ref, mod_ref, gmix_ref, wuv_ref, wz_ref, wxbc_ref, wdt_ref, wdtT_ref,
                dtb_ref, dtbT_ref, alog_ref, alogT_ref, exp_ref, gv_ref, ws_ref, bs_ref,
                cw_ref, cb_ref, dsk_ref, gssm_ref, wout_ref,
                o_ref, ssm_ref, ctail_ref,
                hst_ref, tail_ref, uv_ref, z_ref, xs_ref, dt_ref, dtT_ref, ycat_ref):
    t = pl.program_id(1)
    tm = x_ref.shape[0]
    n_chunks = tm // CHUNK

    @pl.when(t == 0)
    def _():
        hst_ref[...] = jnp.zeros_like(hst_ref)
        tail_ref[...] = jnp.zeros_like(tail_ref)

    x = x_ref[...]
    h = _rmsnorm(x, gmix_ref[...]) * (1.0 + mod_ref[1]) + mod_ref[0]
    hb = h.astype(BF16)
    uv_ref[...] = _dot2(hb, wuv_ref)
    z_ref[...] = _dot2(hb, wz_ref)
    dt_ref[...] = _softplus(_dot(hb, wdt_ref[...]) + dtb_ref[...])
    dtT_ref[...] = _softplus(_dot_nt(wdtT_ref[...], hb) + dtbT_ref[...])

    cur = _dot2(hb, wxbc_ref)
    prev = tail_ref[...]
    row8 = lax.broadcasted_iota(jnp.int32, (SUBLANES, 1), 0)
    acc = cb_ref[...] + cur * cw_ref[CONV_W - 1:CONV_W, :]
    for k in range(1, CONV_W):
        r = pltpu.roll(cur, k, axis=0)
        head = jnp.where(row8 < k, pltpu.roll(prev, k, axis=0), r[0:SUBLANES])
        shifted = jnp.concatenate([head, r[SUBLANES:]], axis=0)
        acc = acc + shifted * cw_ref[CONV_W - 1 - k:CONV_W - k, :]
    xs_ref[...] = _silu(acc)
    new_tail = cur[tm - SUBLANES:tm]
    tail_ref[...] = new_tail
    ctail_ref[...] = new_tail

    ri = lax.broadcasted_iota(jnp.int32, (CHUNK, CHUNK), 0)
    ci = lax.broadcasted_iota(jnp.int32, (CHUNK, CHUNK), 1)
    causal = ri >= ci
    tril = jnp.where(causal, 1.0, 0.0).astype(BF16)
    triu = jnp.where(ri <= ci, 1.0, 0.0).astype(BF16)
    low_half = ri < B_HEAD_DIM
    low_lane = ci < B_HEAD_DIM
    a_col = -jnp.exp(alog_ref[...])
    a_row = -jnp.exp(alogT_ref[...])

    def chunk_body(c, carry):
        r0 = pl.multiple_of(c * CHUNK, CHUNK)
        rows = pl.ds(r0, CHUNK)

        ys, _ = _gate_mlp_chunk(uv_ref[rows, 0:1024], uv_ref[rows, 1024:2048], gv_ref[...], ws_ref, bs_ref, causal)
        for g in range(A_HEADS):
            ycat_ref[rows, g * A_HEAD_DIM:(g + 1) * A_HEAD_DIM] = ys[g]

        dt = dt_ref[rows, :]
        d_a = dt * a_col
        h1, h2, h3 = _split3(d_a)
        acum = _dot(tril, h1) + (_dot(tril, h2) + _dot(tril, h3))
        e1, e2, e3 = _split3(acum)
        acum_bc = _dot(e1, exp_ref[...]) + (_dot(e2, exp_ref[...]) + _dot(e3, exp_ref[...]))
        f1, f2 = _split2(dt)
        dt_bc = _dot(f1, exp_ref[...]) + _dot(f2, exp_ref[...])
        dt_t = dtT_ref[:, rows]
        g1, g2, g3 = _split3(dt_t * a_row)
        acum_t = _dot(g1, triu) + (_dot(g2, triu) + _dot(g3, triu))

        xs = xs_ref[rows, 0:1024]
        zz = z_ref[rows, :]
        y_pairs = []
        for g in range(B_GROUPS):
            bg = xs_ref[rows, 1024 + g * D_STATE:1024 + (g + 1) * D_STATE]
            cg = xs_ref[rows, 1280 + g * D_STATE:1280 + (g + 1) * D_STATE]
            cbm = _dot_nt(cg.astype(BF16), bg.astype(BF16))
            for pr in range(B_HEADS // B_GROUPS // 2):
                pair = g * 4 + pr
                xpair = xs[:, pair * 128:(pair + 1) * 128]
                hprev = hst_ref[pair]
                y_acc = None
                st_acc = None
                alasts = []
                for e in range(2):
                    hd = pair * 2 + e
                    ab = acum_bc[:, hd * 128:(hd + 1) * 128]
                    seg = ab - acum_t[hd:hd + 1, :]
                    lm = jnp.exp(jnp.where(causal, seg, NEG_INF))
                    gd = (cbm * lm * dt_t[hd:hd + 1, :]).astype(BF16)
                    sel_l = low_lane if e == 0 else jnp.logical_not(low_lane)
                    sel_r = low_half if e == 0 else jnp.logical_not(low_half)
                    xh = jnp.where(sel_l, xpair, 0.0).astype(BF16)
                    hp = jnp.where(sel_r, hprev, 0.0).astype(BF16)
                    cs = (cg * jnp.exp(ab)).astype(BF16)
                    alast = ab[CHUNK - 1:CHUNK, :]
                    bsc = (bg * (jnp.exp(alast - ab) * dt_bc[:, hd * 128:(hd + 1) * 128])).astype(BF16)
                    y_e = _dot(gd, xh) + _dot_nt(cs, hp)
                    st_e = _dot_tn(xh, bsc)
                    y_acc = y_e if y_acc is None else y_acc + y_e
                    st_acc = st_e if st_acc is None else st_acc + st_e
                    alasts.append(alast)
                decay = jnp.exp(jnp.where(low_half, alasts[0], alasts[1]))
                hst_ref[pair] = hprev * decay + st_acc
                y_pairs.append(y_acc)
        y = jnp.concatenate(y_pairs, axis=1) + dsk_ref[...] * xs
        y = y * _silu(zz)
        half = B_HEADS * B_HEAD_DIM // B_GROUPS
        for g in range(B_GROUPS):
            sl = slice(g * half, (g + 1) * half)
            ycat_ref[rows, 1024 + g * half:1024 + (g + 1) * half] = _rmsnorm(y[:, sl], gssm_ref[:, sl]).astype(BF16)
        return carry

    lax.fori_loop(0, n_chunks, chunk_body, 0)

    o_ref[...] = x + mod_ref[2] * _dot2(ycat_ref[...], wout_ref)

    @pl.when(t == pl.num_programs(1) - 1)
    def _():
        ssm_ref[...] = hst_ref[...]


def _mix_call(x, mod1, p, tm=MIX_ROWS):
    b, s, d = x.shape
    n_pairs = B_HEADS // 2
    in_specs = [
        pl.BlockSpec((None, tm, d), lambda i, t: (i, t, 0)),
        pl.BlockSpec((None, 3, 1, d), lambda i, t: (i, 0, 0, 0)),
        _full((1, d)),
        _full(p["w_uv"].shape), _full(p["w_z"].shape), _full(p["w_xbc"].shape),
        _full(p["w_dt"].shape), _full(p["w_dtT"].shape),
        _full((1, LANES)), _full((LANES, 1)), _full((1, LANES)), _full((LANES, LANES)),
        _full(p["expand"].shape),
        _full((1, 1024)), _full((A_HEADS, CHUNK, CHUNK)), _full((A_HEADS, CHUNK, CHUNK)),
        _full((CONV_W, 1536)), _full((1, 1536)), _full((1, 1024)), _full((1, 1024)),
        _full(p["w_out"].shape),
    ]
    out_specs = [
        pl.BlockSpec((None, tm, d), lambda i, t: (i, t, 0)),
        pl.BlockSpec((None, n_pairs, 128, D_STATE), lambda i, t: (i, 0, 0, 0)),
        pl.BlockSpec((None, SUBLANES, 1536), lambda i, t: (i, 0, 0)),
    ]
    out_shape = [
        jax.ShapeDtypeStruct((b, s, d), F32),
        jax.ShapeDtypeStruct((b, n_pairs, 128, D_STATE), F32),
        jax.ShapeDtypeStruct((b, SUBLANES, 1536), F32),
    ]
    scratch = [
        pltpu.VMEM((n_pairs, 128, D_STATE), F32),
        pltpu.VMEM((SUBLANES, 1536), F32),
        pltpu.VMEM((tm, 2048), F32),
        pltpu.VMEM((tm, 1024), F32),
        pltpu.VMEM((tm, 1536), F32),
        pltpu.VMEM((tm, LANES), F32),
        pltpu.VMEM((LANES, tm), F32),
        pltpu.VMEM((tm, 2048), BF16),
    ]
    return pl.pallas_call(
        _mix_kernel,
        grid=(b, s // tm),
        in_specs=in_specs, out_specs=out_specs, out_shape=out_shape, scratch_shapes=scratch,
        compiler_params=_params(("parallel", "arbitrary")),
        name="mix_prompt",
    )(x, mod1, p["g_mix"], p["w_uv"], p["w_z"], p["w_xbc"], p["w_dt"], p["w_dtT"],
      p["dt_bias_row"], p["dt_bias_col"], p["a_log_row"], p["a_log_sq"], p["expand"],
      p["g_v"], p["w_s"], p["b_s_rep"], p["conv_w"], p["conv_b"], p["d_skip_exp"], p["g_ssm"], p["w_out"])


def _s1_kernel(x_ref, mod_ref, gmix_ref, wuv_ref, wz_ref, wxbc_ref, wdt_ref, dtb_ref, alog_ref, exp64_ref,
               gv_ref, ws0_ref, bs0_ref, cw_ref, cb_ref, sconv_ref,
               v_ref, cnew_ref, ya_ref, z_ref, xs_ref, bc_ref, xdtT_ref, dec_ref):
    x = x_ref[...]
    h = _rmsnorm(x, gmix_ref[...]) * (1.0 + mod_ref[1]) + mod_ref[0]
    hb = h.astype(BF16)
    uv = _dot(hb, wuv_ref[...])
    gv = gv_ref[...]
    for g in range(A_HEADS):
        sl = slice(g * A_HEAD_DIM, (g + 1) * A_HEAD_DIM)
        ug = _gelu(uv[:, sl])
        vn = _rmsnorm(_gelu(uv[:, 1024 + g * A_HEAD_DIM:1024 + (g + 1) * A_HEAD_DIM]), gv[:, sl])
        v_ref[:, sl] = vn
        s = ws0_ref[:, sl] * vn + bs0_ref[:, sl]
        ya_ref[:, sl] = (ug * s).astype(BF16)
    z_ref[...] = _dot(hb, wz_ref[...])
    raw = _dot(hb, wxbc_ref[...])
    cd = raw.shape[1]
    acc = cb_ref[...] + raw * cw_ref[CONV_W - 1:CONV_W, :]
    for k in range(CONV_W - 1):
        acc = acc + sconv_ref[:, k * cd:(k + 1) * cd] * cw_ref[k:k + 1, :]
    cnew_ref[:, 0:(CONV_W - 2) * cd] = sconv_ref[:, cd:(CONV_W - 1) * cd]
    cnew_ref[:, (CONV_W - 2) * cd:(CONV_W - 1) * cd] = raw
    xbc = _silu(acc)
    xs = xbc[:, 0:1024]
    xs_ref[...] = xs
    bc_ref[...] = xbc[:, 1024:1536]
    dt = _softplus(_dot(hb, wdt_ref[...]) + dtb_ref[...])
    dec_ref[...] = jnp.exp(dt * (-jnp.exp(alog_ref[...])))
    f1, f2, f3 = _split3(dt)
    dt64 = _dot(f1, exp64_ref[...]) + (_dot(f2, exp64_ref[...]) + _dot(f3, exp64_ref[...]))
    xdtT_ref[...] = (xs * dt64).T.astype(BF16)


def _s1_call(x, mod1, sconv, p):
    n, d = x.shape
    outs = [
        jax.ShapeDtypeStruct((n, 1024), F32),
        jax.ShapeDtypeStruct((n, 3 * 1536), F32),
        jax.ShapeDtypeStruct((n, 1024), BF16),
        jax.ShapeDtypeStruct((n, 1024), F32),
        jax.ShapeDtypeStruct((n, 1024), F32),
        jax.ShapeDtypeStruct((n, 512), F32),
        jax.ShapeDtypeStruct((1024, n), BF16),
        jax.ShapeDtypeStruct((n, LANES), F32),
    ]
    args = (x, mod1, p["g_mix"], p["w_uv"], p["w_z"], p["w_xbc"], p["w_dt"], p["dt_bias_row"], p["a_log_row"],
            p["expand64"], p["g_v"], p["w_s00"], p["b_s0"], p["conv_w"], p["conv_b"], sconv)
    return pl.pallas_call(
        _s1_kernel,
        grid=(1,),
        in_specs=[_full(a.shape) for a in args],
        out_specs=[_full(o.shape) for o in outs],
        out_shape=outs,
        compiler_params=_params(("arbitrary",)),
        name="mix_sample_in",
    )(*args)


def _s2_kernel(dec_ref, st_ref, xdtT_ref, bc_ref, cblk_ref, o_ref, y_ref):
    i = pl.program_id(0)
    bt = st_ref.shape[0]
    n = bc_ref.shape[0]
    half = B_HEADS * B_HEAD_DIM // B_GROUPS
    rowi = lax.broadcasted_iota(jnp.int32, (n, 1), 0)
    for bb in range(bt):
        b = i * bt + bb
        for g in range(B_GROUPS):
            bm = bc_ref[:, g * D_STATE:(g + 1) * D_STATE]
            rb = jnp.where(rowi == b, bm, 0.0).astype(BF16)
            outer = _dot(xdtT_ref[g * half:(g + 1) * half, :], rb)
            for k in range(B_HEADS // B_GROUPS):
                hd = g * (B_HEADS // B_GROUPS) + k
                sl = slice(k * B_HEAD_DIM, (k + 1) * B_HEAD_DIM)
                o_ref[bb, g, sl, :] = st_ref[bb, g, sl, :] * dec_ref[b, hd] + outer[sl, :]
            crow = cblk_ref[bb:bb + 1, 256 + g * D_STATE:256 + (g + 1) * D_STATE]
            c8 = jnp.broadcast_to(crow, (SUBLANES, D_STATE)).astype(BF16)
            yr = _dot_nt(c8, o_ref[bb, g].astype(BF16))
            y_ref[bb:bb + 1, g * half:(g + 1) * half] = yr[0:1]


def _s2_call(dec, state, xdtT, bc, bt=S2_TOKENS):
    n = state.shape[0]
    half = B_HEADS * B_HEAD_DIM // B_GROUPS
    st = state.reshape(n, B_GROUPS, half, D_STATE)
    grid_spec = pltpu.PrefetchScalarGridSpec(
        num_scalar_prefetch=1,
        grid=(n // bt,),
        in_specs=[
            pl.BlockSpec((bt, B_GROUPS, half, D_STATE), lambda i, d: (i, 0, 0, 0)),
            pl.BlockSpec(xdtT.shape, lambda i, d: (0, 0)),
            pl.BlockSpec(bc.shape, lambda i, d: (0, 0)),
            pl.BlockSpec((bt, bc.shape[1]), lambda i, d: (i, 0)),
        ],
        out_specs=[
            pl.BlockSpec((bt, B_GROUPS, half, D_STATE), lambda i, d: (i, 0, 0, 0)),
            pl.BlockSpec((bt, 1024), lambda i, d: (i, 0)),
        ],
    )
    new_state, y = pl.pallas_call(
        _s2_kernel,
        grid_spec=grid_spec,
        out_shape=[jax.ShapeDtypeStruct(st.shape, F32), jax.ShapeDtypeStruct((n, 1024), F32)],
        compiler_params=_params(("arbitrary",)),
        name="mix_sample_state",
    )(dec, st, xdtT, bc, bc)
    return new_state.reshape(state.shape), y


def _s3_kernel(x_ref, mod_ref, y_ref, xs_ref, z_ref, ya_ref, dsk_ref, gssm_ref, wout_ref, o_ref):
    y = (y_ref[...] + dsk_ref[...] * xs_ref[...]) * _silu(z_ref[...])
    half = B_HEADS * B_HEAD_DIM // B_GROUPS
    parts = [ya_ref[...]]
    for g in range(B_GROUPS):
        sl = slice(g * half, (g + 1) * half)
        parts.append(_rmsnorm(y[:, sl], gssm_ref[:, sl]).astype(BF16))
    cat = jnp.concatenate(parts, axis=1)
    o_ref[...] = x_ref[...] + mod_ref[2] * _dot(cat, wout_ref[...])


def _s3_call(x, mod1, y, xs, z, ya, p):
    args = (x, mod1, y, xs, z, ya, p["d_skip_exp"], p["g_ssm"], p["w_out"])
    return pl.pallas_call(
        _s3_kernel,
        grid=(1,),
        in_specs=[_full(a.shape) for a in args],
        out_specs=_full(x.shape),
        out_shape=jax.ShapeDtypeStruct(x.shape, F32),
        compiler_params=_params(("arbitrary",)),
        name="mix_sample_out",
    )(*args)


def _cand_blocks(a, b):
    row8 = lax.broadcasted_iota(jnp.int32, (SUBLANES, 1), 0)
    a8 = a[0:SUBLANES]
    blocks = []
    for j in range(SUBLANES):
        blk = a8 + b[j:j + 1]
        cnt = min(SUBLANES, (TOPK + 1) // (j + 1))
        if cnt < SUBLANES:
            blk = jnp.where(row8 < cnt, blk, NEG_INF)
        blocks.append(blk)
    blocks.append(a[0:1] + b[SUBLANES:TOPK])
    blocks.append(a[SUBLANES:TOPK] + b[0:1])
    last = jnp.where(row8 == 0, a[0:1] + b[TOPK:TOPK + 1], jnp.where(row8 == 1, a[TOPK:TOPK + 1] + b[0:1], NEG_INF))
    blocks.append(last)
    return blocks


def _merge_exchange_pairs(n):
    pairs = []
    t = max(1, math.ceil(math.log2(n)))
    p = 1 << (t - 1)
    while p > 0:
        q, r, d = 1 << (t - 1), 0, p
        while d > 0:
            pairs.extend((i, i + d) for i in range(n - d) if (i & p) == r)
            d, q, r = q - p, q >> 1, p
        p >>= 1
    return pairs


def _sort_blocks_desc(blocks):
    blocks = list(blocks)
    for i, j in _merge_exchange_pairs(len(blocks)):
        hi = jnp.maximum(blocks[i], blocks[j])
        blocks[j] = jnp.minimum(blocks[i], blocks[j])
        blocks[i] = hi
    return blocks


def _pop_largest(blocks, k):
    blocks = list(blocks)
    nb = len(blocks)
    rows = []
    for t in range(k):
        m = jnp.max(blocks[0], axis=0, keepdims=True)
        rows.append(m)
        need = min(k - t - 1, nb)
        if need == 0:
            break
        sel = blocks[0] == m
        for j in range(need):
            nxt = blocks[j + 1] if j + 1 < nb else NEG_INF
            blocks[j] = jnp.where(sel, nxt, blocks[j])
    return rows


def _peer_kernel(x_ref, mod_ref, gffn_ref, gfin_ref, wqT_ref, keys_ref, u_ref, vt_ref, y_ref,
                 h2_ref, q_ref, s1_ref, s2_ref, e1_ref, e2_ref, tops_ref, act_ref, coef_ref, outT_ref,
                 *, n_chunks):
    c = pl.program_id(1)
    n_blk = u_ref.shape[0] // N_KEYS

    @pl.when(c == 0)
    def _():
        x = x_ref[...]
        h = _rmsnorm(x, gffn_ref[...]) * (1.0 + mod_ref[1]) + mod_ref[0]
        ht = h.T.astype(BF16)
        h2_ref[...] = ht
        qh = wqT_ref.shape[0] // 2
        q_ref[0:qh, :] = _dot(wqT_ref[0:qh, :], ht)
        q_ref[qh:, :] = _dot(wqT_ref[qh:, :], ht)
        outT_ref[...] = jnp.zeros_like(outT_ref)

        def one_head(hh, top_ref):
            scores = []
            for s in range(2):
                r0 = pl.multiple_of((2 * hh + s) * N_KEYS, N_KEYS)
                qs = q_ref[pl.ds(r0, N_KEYS), :].astype(BF16)
                sc = _dot(keys_ref[2 * hh + s], qs) * LOG2E
                if s == 0:
                    s1_ref[hh] = sc
                else:
                    s2_ref[hh] = sc
                scores.append(sc)
            n_tok = x_ref.shape[0]
            strip = min(n_tok, PEER_SELECT_STRIP)
            for c0 in range(0, n_tok, strip):
                cols = slice(c0, c0 + strip)
                for s in range(2):
                    srt = _sort_blocks_desc([scores[s][j * SUBLANES:(j + 1) * SUBLANES, cols]
                                             for j in range(N_KEYS // SUBLANES)])
                    for k, m in enumerate(_pop_largest(srt, TOPK + 1)):
                        top_ref[s, k:k + 1, cols] = m
                a = top_ref[0, 0:TOPK + 1, cols]
                b = top_ref[1, 0:TOPK + 1, cols]
                best = a[0:1] + b[0:1]
                pops = _pop_largest(_sort_blocks_desc(_cand_blocks(a, b)), TOPK + 1)
                zsum = None
                for m in pops[:TOPK]:
                    e = jnp.exp2(m - best)
                    zsum = e if zsum is None else zsum + e
                thr = jnp.where(pops[TOPK] == NEG_INF, pops[TOPK - 1], 0.5 * (pops[TOPK - 1] + pops[TOPK]))
                s1 = s1_ref[hh, :, cols]
                s1_ref[hh, :, cols] = thr - s1
                e1_ref[hh, :, cols] = 0.5 * jnp.exp2(s1 - (a[0:1] + jnp.log2(zsum)))
                e2_ref[hh, :, cols] = jnp.exp2(scores[1][:, cols] - b[0:1])

        def head_body(hp, carry):
            for e in range(2):
                one_head(2 * hp + e, tops_ref.at[e])
            return carry

        lax.fori_loop(0, R_HEADS // 2, head_body, 0)

    eh = u_ref.shape[0] // 2
    dh = vt_ref.shape[0] // 2

    @pl.when(c < n_chunks)
    def _():
        act_w = act_ref.at[c % 2]
        act_w[0:eh, :] = _dot(u_ref[0:eh, :], h2_ref[...])
        act_w[eh:, :] = _dot(u_ref[eh:, :], h2_ref[...])

    @pl.when(c >= 1)
    def _():
        act_r = act_ref.at[(c - 1) % 2]
        i1 = pl.multiple_of((c - 1) * n_blk, n_blk)
        u_rows = [s1_ref[hh, pl.ds(i1, n_blk), :] for hh in range(R_HEADS)]
        e1_rows = [e1_ref[hh, pl.ds(i1, n_blk), :] for hh in range(R_HEADS)]
        for r in range(n_blk):
            acc = None
            for hh in range(R_HEADS):
                gate = jnp.where(s2_ref[hh] >= u_rows[hh][r:r + 1, :], e2_ref[hh] * e1_rows[hh][r:r + 1, :], 0.0)
                acc = gate if acc is None else acc + gate
            blk = slice(r * N_KEYS, (r + 1) * N_KEYS)
            x = act_r[blk, :]
            coef_ref[blk, :] = (acc * (x * (1.0 + lax.erf(x * (1.0 / math.sqrt(2.0)))))).astype(BF16)
        outT_ref[0:dh, :] += _dot(vt_ref[0:dh, :], coef_ref[...])
        outT_ref[dh:, :] += _dot(vt_ref[dh:, :], coef_ref[...])

    @pl.when(c == n_chunks)
    def _():
        x2 = x_ref[...] + mod_ref[2] * outT_ref[...].T
        y_ref[...] = _rmsnorm(x2, gfin_ref[...]) * (1.0 + mod_ref[4]) + mod_ref[3]


def _peer_call(x, mod5, tiles_per_row, p, tl, ec=PEER_EXPERTS):
    t, d = x.shape
    n_exp = p["expert_u"].shape[0]
    r = mod5.shape[2]
    scratch = [
        pltpu.VMEM((d, tl), BF16),
        pltpu.VMEM((R_HEADS * 2 * N_KEYS, tl), F32),
        pltpu.VMEM((R_HEADS, N_KEYS, tl), F32),
        pltpu.VMEM((R_HEADS, N_KEYS, tl), F32),
        pltpu.VMEM((R_HEADS, N_KEYS, tl), F32),
        pltpu.VMEM((R_HEADS, N_KEYS, tl), F32),
        pltpu.VMEM((2, 2, TOPK + SUBLANES, tl), F32),
        pltpu.VMEM((2, ec, tl), F32),
        pltpu.VMEM((ec, tl), BF16),
        pltpu.VMEM((d, tl), F32),
    ]
    n_chunks = n_exp // ec
    return pl.pallas_call(
        functools.partial(_peer_kernel, n_chunks=n_chunks),
        grid=(t // tl, n_chunks + 1),
        in_specs=[
            pl.BlockSpec((tl, d), lambda i, c: (i, 0)),
            pl.BlockSpec((None, 5, r, d), lambda i, c: (i // tiles_per_row, 0, 0, 0)),
            _full((1, d)), _full((1, d)),
            _full(p["w_qT"].shape), _full(p["keys"].shape),
            pl.BlockSpec((ec, d), lambda i, c: (jnp.minimum(c, n_chunks - 1), 0)),
            pl.BlockSpec((d, ec), lambda i, c: (0, jnp.maximum(c - 1, 0))),
        ],
        out_specs=pl.BlockSpec((tl, d), lambda i, c: (i, 0)),
        out_shape=jax.ShapeDtypeStruct((t, d), F32),
        scratch_shapes=scratch,
        compiler_params=_params(("parallel", "arbitrary")),
        name="peer",
    )(x, mod5, p["g_ffn"], p["g_final"], p["w_qT"], p["keys"], p["expert_u"], p["expert_vT"])


def _prep(w_in, g_mix, g_v, w_s, b_s, conv_w, conv_b, dt_bias, a_log, d_skip, g_ssm, w_out, g_ffn, w_q,
          sub_keys, expert_u, expert_v, g_final):
    d = w_in.shape[0]
    o_z = 2 * 1024
    o_x = o_z + 1024
    o_dt = o_x + 1536
    w_dt = jnp.zeros((d, LANES), F32).at[:, :B_HEADS].set(w_in[:, o_dt:o_dt + B_HEADS])
    pad16 = lambda v: jnp.zeros((LANES,), F32).at[:B_HEADS].set(v)
    heads = jnp.arange(LANES)[:, None]
    p = {
        "g_mix": g_mix.reshape(1, d),
        "w_uv": w_in[:, 0:o_z].astype(BF16),
        "w_z": w_in[:, o_z:o_x].astype(BF16),
        "w_xbc": w_in[:, o_x:o_dt].astype(BF16),
        "w_dt": w_dt.astype(BF16),
        "w_dtT": w_dt.T.astype(BF16),
        "dt_bias_row": pad16(dt_bias).reshape(1, LANES),
        "dt_bias_col": pad16(dt_bias).reshape(LANES, 1),
        "a_log_row": pad16(a_log).reshape(1, LANES),
        "a_log_sq": jnp.broadcast_to(pad16(a_log).reshape(LANES, 1), (LANES, LANES)),
        "expand": (heads == jnp.arange(B_HEADS * LANES)[None, :] // LANES).astype(BF16),
        "expand64": (heads == jnp.arange(B_HEADS * B_HEAD_DIM)[None, :] // B_HEAD_DIM).astype(BF16),
        "g_v": g_v.reshape(1, -1),
        "w_s": w_s,
        "b_s_rep": jnp.broadcast_to(b_s[:, :, None], (A_HEADS, CHUNK, CHUNK)),
        "w_s00": jnp.repeat(w_s[:, 0, 0], A_HEAD_DIM).reshape(1, -1),
        "b_s0": jnp.repeat(b_s[:, 0], A_HEAD_DIM).reshape(1, -1),
        "conv_w": conv_w,
        "conv_b": conv_b.reshape(1, -1),
        "d_skip_exp": jnp.repeat(d_skip, B_HEAD_DIM).reshape(1, -1),
        "g_ssm": g_ssm.reshape(1, -1),
        "w_out": w_out.astype(BF16),
        "g_ffn": g_ffn.reshape(1, d),
        "g_final": g_final.reshape(1, d),
        "w_qT": w_q.T.astype(BF16),
        "keys": sub_keys.reshape(R_HEADS * 2, N_KEYS, -1).astype(BF16),
        "expert_u": expert_u.astype(BF16),
        "expert_vT": expert_v.T.astype(BF16),
    }
    return p


def kernel(x_prompt, x_sample, c_prompt, c_sample, state_ssm, state_conv, w_ada, b_ada, g_mix, w_in, g_v, w_s, b_s,
           conv_w, conv_b, dt_bias, a_log, d_skip, g_ssm, w_out, g_ffn, w_q, sub_keys, expert_u, expert_v,
           w_ada_f, b_ada_f, g_final):
    assert w_ada.shape[0] == 1, "single-layer trunk"
    bp, seq, d = x_prompt.shape
    ns = x_sample.shape[0]
    p = _prep(w_in[0], g_mix[0], g_v[0], w_s[0], b_s[0], conv_w[0], conv_b[0], dt_bias[0], a_log[0], d_skip[0],
              g_ssm[0], w_out[0], g_ffn[0], w_q[0], sub_keys[0], expert_u[0], expert_v[0], g_final)

    c_all = jnp.concatenate([c_prompt, c_sample], axis=0)
    mod = _ada_call(c_all, w_ada[0], b_ada[0]).reshape(bp + ns, N_MOD, d)
    modf = _ada_call(c_all, w_ada_f, b_ada_f).reshape(bp + ns, 2, d)
    mod5 = jnp.concatenate([mod[:, 3:6], modf], axis=1)

    x1_p, ssm_p, ctail_p = _mix_call(x_prompt, mod[:bp, 0:3].reshape(bp, 3, 1, d), p)
    y_p = _peer_call(x1_p.reshape(bp * seq, d), mod5[:bp].reshape(bp, 5, 1, d), seq // PEER_TOKENS, p, PEER_TOKENS)

    xs_in = x_sample.reshape(ns, d)
    mod1_s = jnp.transpose(mod[bp:, 0:3], (1, 0, 2))
    sconv = state_conv[0].reshape(ns, -1)
    v_s, cnew_s, ya_s, z_s, xc_s, bc_s, xdtT_s, dec_s = _s1_call(xs_in, mod1_s, sconv, p)
    ssm_s, yssd_s = _s2_call(dec_s[:, :B_HEADS], state_ssm[0], xdtT_s, bc_s)
    x1_s = _s3_call(xs_in, mod1_s, yssd_s, xc_s, z_s, ya_s, p)
    mod5_s = jnp.transpose(mod5[bp:], (1, 0, 2)).reshape(1, 5, ns, d)
    y_s = _peer_call(x1_s, mod5_s, 1, p, ns)

    return (
        y_p.reshape(bp, seq, d),
        y_s.reshape(ns, 1, d),
        ssm_p.reshape(1, bp, B_HEADS, B_HEAD_DIM, D_STATE),
        ctail_p[:, SUBLANES - (CONV_W - 1):, :].reshape(1, bp, CONV_W - 1, -1),
        ssm_s.reshape(1, ns, B_HEADS, B_HEAD_DIM, D_STATE),
        cnew_s.reshape(1, ns, CONV_W - 1, -1),
        v_s.reshape(1, ns, 1, -1),
    )
```

```python
import functools
import math

import jax
import jax.numpy as jnp
from jax import lax
from jax.experimental import pallas as pl
from jax.experimental.pallas import tpu as pltpu

F32 = jnp.float32
BF16 = jnp.bfloat16
NEG_INF = float("-inf")

EPS = 1e-6
A_HEADS = 8
A_HEAD_DIM = 128
CHUNK = 128
B_HEADS = 16
B_HEAD_DIM = 64
B_GROUPS = 2
D_STATE = 128
CONV_W = 4
N_KEYS = 128
R_HEADS = 8
TOPK = 16
N_MOD = 6

LANES = 128
SUBLANES = 8
VMEM_LIMIT = 56 * 1024 * 1024

MIX_ROWS = 512
PEER_TOKENS = 256
PEER_EXPERTS = 2048
PEER_SELECT_STRIP = 256
PEER_HEADS_PER_TRIP = 4
LOG2E = 1.4426950408889634
S2_TOKENS = 8


def _dot(a, b):
    return jnp.dot(a, b, preferred_element_type=F32)


def _dot_nt(a, b):
    return lax.dot_general(a, b, (((1,), (1,)), ((), ())), preferred_element_type=F32)


def _dot_tn(a, b):
    return lax.dot_general(a, b, (((0,), (0,)), ((), ())), preferred_element_type=F32)


def _split2(x):
    hi = x.astype(BF16)
    lo = (x - hi.astype(F32)).astype(BF16)
    return hi, lo


def _split3(x):
    hi = x.astype(BF16)
    r = x - hi.astype(F32)
    mid = r.astype(BF16)
    lo = (r - mid.astype(F32)).astype(BF16)
    return hi, mid, lo


def _dot_x3(a, b):
    a1, a2 = _split2(a)
    b1, b2 = _split2(b)
    return _dot(a1, b1) + (_dot(a1, b2) + _dot(a2, b1))


def _silu(x):
    return x / (1.0 + jnp.exp(-x))


def _gelu(x):
    return 0.5 * x * (1.0 + lax.erf(x * (1.0 / math.sqrt(2.0))))


def _softplus(x):
    return jnp.maximum(x, 0.0) + jnp.log1p(jnp.exp(-jnp.abs(x)))


def _rmsnorm(x, g):
    return x * lax.rsqrt(jnp.mean(x * x, axis=-1, keepdims=True) + EPS) * g


def _full(shape):
    nd = len(shape)
    return pl.BlockSpec(shape, lambda *_: (0,) * nd, pipeline_mode=pl.Buffered(1))


def _dot2(a, w_ref):
    h = w_ref.shape[1] // 2
    return jnp.concatenate([_dot(a, w_ref[:, :h]), _dot(a, w_ref[:, h:])], axis=1)


def _params(sem):
    return pltpu.CompilerParams(dimension_semantics=sem, vmem_limit_bytes=VMEM_LIMIT)


def _ada_kernel(c_ref, w_ref, b_ref, o_ref):
    o_ref[...] = _dot_x3(_silu(c_ref[...]), w_ref[...]) + b_ref[...]


def _ada_call(c, w, b, tn=512):
    m, k = c.shape
    n = w.shape[1]
    return pl.pallas_call(
        _ada_kernel,
        grid=(n // tn,),
        in_specs=[_full((m, k)), pl.BlockSpec((k, tn), lambda j: (0, j)), pl.BlockSpec((1, tn), lambda j: (0, j))],
        out_specs=pl.BlockSpec((m, tn), lambda j: (0, j)),
        out_shape=jax.ShapeDtypeStruct((m, n), F32),
        compiler_params=_params(("parallel",)),
        name="ada",
    )(c, w, b.reshape(1, n))


def _gate_mlp_chunk(u, v, gv, ws_ref, bs_ref, causal):
    ys, vs = [], []
    for g in range(A_HEADS):
        sl = slice(g * A_HEAD_DIM, (g + 1) * A_HEAD_DIM)
        ug = _gelu(u[:, sl])
        vn = _rmsnorm(_gelu(v[:, sl]), gv[:, sl])
        w = jnp.where(causal, ws_ref[g], 0.0).astype(BF16)
        s = _dot(w, vn.astype(BF16)) + bs_ref[g]
        ys.append((ug * s).astype(BF16))
        vs.append(vn)
    return ys, vs


def _mix_kernel(x_ref, mod_ref, gmix_ref, wuv_ref, wz_ref, wxbc_ref, wdt_ref, wdtT_ref,
                dtb_ref, dtbT_ref, alog_ref, alogT_ref, exp_ref, gv_ref, ws_ref, bs_ref,
                cw_ref, cb_ref, dsk_ref, gssm_ref, wout_ref,
                o_ref, ssm_ref, ctail_ref,
                hst_ref, tail_ref, uv_ref, z_ref, xs_ref, dt_ref, dtT_ref, ycat_ref):
    t = pl.program_id(1)
    tm = x_ref.shape[0]
    n_chunks = tm // CHUNK

    @pl.when(t == 0)
    def _():
        hst_ref[...] = jnp.zeros_like(hst_ref)
        tail_ref[...] = jnp.zeros_like(tail_ref)

    x = x_ref[...]
    h = _rmsnorm(x, gmix_ref[...]) * (1.0 + mod_ref[1]) + mod_ref[0]
    hb = h.astype(BF16)
    uv_ref[...] = _dot2(hb, wuv_ref)
    z_ref[...] = _dot2(hb, wz_ref)
    dt_ref[...] = _softplus(_dot(hb, wdt_ref[...]) + dtb_ref[...])
    dtT_ref[...] = _softplus(_dot_nt(wdtT_ref[...], hb) + dtbT_ref[...])

    cur = _dot2(hb, wxbc_ref)
    prev = tail_ref[...]
    row8 = lax.broadcasted_iota(jnp.int32, (SUBLANES, 1), 0)
    acc = cb_ref[...] + cur * cw_ref[CONV_W - 1:CONV_W, :]
    for k in range(1, CONV_W):
        r = pltpu.roll(cur, k, axis=0)
        head = jnp.where(row8 < k, pltpu.roll(prev, k, axis=0), r[0:SUBLANES])
        shifted = jnp.concatenate([head, r[SUBLANES:]], axis=0)
        acc = acc + shifted * cw_ref[CONV_W - 1 - k:CONV_W - k, :]
    xs_ref[...] = _silu(acc)
    new_tail = cur[tm - SUBLANES:tm]
    tail_ref[...] = new_tail
    ctail_ref[...] = new_tail

    ri = lax.broadcasted_iota(jnp.int32, (CHUNK, CHUNK), 0)
    ci = lax.broadcasted_iota(jnp.int32, (CHUNK, CHUNK), 1)
    causal = ri >= ci
    tril = jnp.where(causal, 1.0, 0.0).astype(BF16)
    triu = jnp.where(ri <= ci, 1.0, 0.0).astype(BF16)
    low_half = ri < B_HEAD_DIM
    low_lane = ci < B_HEAD_DIM
    a_col = -jnp.exp(alog_ref[...])
    a_row = -jnp.exp(alogT_ref[...])

    def chunk_body(c, carry):
        r0 = pl.multiple_of(c * CHUNK, CHUNK)
        rows = pl.ds(r0, CHUNK)

        ys, _ = _gate_mlp_chunk(uv_ref[rows, 0:1024], uv_ref[rows, 1024:2048], gv_ref[...], ws_ref, bs_ref, causal)
        for g in range(A_HEADS):
            ycat_ref[rows, g * A_HEAD_DIM:(g + 1) * A_HEAD_DIM] = ys[g]

        dt = dt_ref[rows, :]
        d_a = dt * a_col
        h1, h2, h3 = _split3(d_a)
        acum = _dot(tril, h1) + (_dot(tril, h2) + _dot(tril, h3))
        e1, e2, e3 = _split3(acum)
        acum_bc = _dot(e1, exp_ref[...]) + (_dot(e2, exp_ref[...]) + _dot(e3, exp_ref[...]))
        f1, f2 = _split2(dt)
        dt_bc = _dot(f1, exp_ref[...]) + _dot(f2, exp_ref[...])
        dt_t = dtT_ref[:, rows]
        g1, g2, g3 = _split3(dt_t * a_row)
        acum_t = _dot(g1, triu) + (_dot(g2, triu) + _dot(g3, triu))

        xs = xs_ref[rows, 0:1024]
        zz = z_ref[rows, :]
        y_pairs = []
        for g in range(B_GROUPS):
            bg = xs_ref[rows, 1024 + g * D_STATE:1024 + (g + 1) * D_STATE]
            cg = xs_ref[rows, 1280 + g * D_STATE:1280 + (g + 1) * D_STATE]
            cbm = _dot_nt(cg.astype(BF16), bg.astype(BF16))
            for pr in range(B_HEADS // B_GROUPS // 2):
                pair = g * 4 + pr
                xpair = xs[:, pair * 128:(pair + 1) * 128]
                hprev = hst_ref[pair]
                gds, xhs, css, hps, bscs, alasts = [], [], [], [], [], []
                for e in range(2):
                    hd = pair * 2 + e
                    ab = acum_bc[:, hd * 128:(hd + 1) * 128]
                    seg = ab - acum_t[hd:hd + 1, :]
                    lm = jnp.exp(jnp.where(causal, seg, NEG_INF))
                    gds.append((cbm * lm * dt_t[hd:hd + 1, :]).astype(BF16))
                    sel_l = low_lane if e == 0 else jnp.logical_not(low_lane)
                    sel_r = low_half if e == 0 else jnp.logical_not(low_half)
                    xhs.append(jnp.where(sel_l, xpair, 0.0).astype(BF16))
                    hps.append(jnp.where(sel_r, hprev, 0.0).astype(BF16))
                    css.append((cg * jnp.exp(ab)).astype(BF16))
                    alast = ab[CHUNK - 1:CHUNK, :]
                    bscs.append((bg * (jnp.exp(alast - ab) * dt_bc[:, hd * 128:(hd + 1) * 128])).astype(BF16))
                    alasts.append(alast)
                x2 = jnp.concatenate(xhs, axis=0)
                y_pairs.append(_dot(jnp.concatenate(gds, axis=1), x2)
                               + _dot_nt(jnp.concatenate(css, axis=1), jnp.concatenate(hps, axis=1)))
                st = _dot_tn(x2, jnp.concatenate(bscs, axis=0))
                decay = jnp.exp(jnp.where(low_half, alasts[0], alasts[1]))
                hst_ref[pair] = hprev * decay + st
        y = jnp.concatenate(y_pairs, axis=1) + dsk_ref[...] * xs
        y = y * _silu(zz)
        half = B_HEADS * B_HEAD_DIM // B_GROUPS
        for g in range(B_GROUPS):
            sl = slice(g * half, (g + 1) * half)
            ycat_ref[rows, 1024 + g * half:1024 + (g + 1) * half] = _rmsnorm(y[:, sl], gssm_ref[:, sl]).astype(BF16)
        return carry

    lax.fori_loop(0, n_chunks, chunk_body, 0)

    o_ref[...] = x + mod_ref[2] * _dot2(ycat_ref[...], wout_ref)

    @pl.when(t == pl.num_programs(1) - 1)
    def _():
        ssm_ref[...] = hst_ref[...]


def _mix_call(x, mod1, p, tm=MIX_ROWS):
    b, s, d = x.shape
    n_pairs = B_HEADS // 2
    in_specs = [
        pl.BlockSpec((None, tm, d), lambda i, t: (i, t, 0)),
        pl.BlockSpec((None, 3, 1, d), lambda i, t: (i, 0, 0, 0)),
        _full((1, d)),
        _full(p["w_uv"].shape), _full(p["w_z"].shape), _full(p["w_xbc"].shape),
        _full(p["w_dt"].shape), _full(p["w_dtT"].shape),
        _full((1, LANES)), _full((LANES, 1)), _full((1, LANES)), _full((LANES, LANES)),
        _full(p["expand"].shape),
        _full((1, 1024)), _full((A_HEADS, CHUNK, CHUNK)), _full((A_HEADS, CHUNK, CHUNK)),
        _full((CONV_W, 1536)), _full((1, 1536)), _full((1, 1024)), _full((1, 1024)),
        _full(p["w_out"].shape),
    ]
    out_specs = [
        pl.BlockSpec((None, tm, d), lambda i, t: (i, t, 0)),
        pl.BlockSpec((None, n_pairs, 128, D_STATE), lambda i, t: (i, 0, 0, 0)),
        pl.BlockSpec((None, SUBLANES, 1536), lambda i, t: (i, 0, 0)),
    ]
    out_shape = [
        jax.ShapeDtypeStruct((b, s, d), F32),
        jax.ShapeDtypeStruct((b, n_pairs, 128, D_STATE), F32),
        jax.ShapeDtypeStruct((b, SUBLANES, 1536), F32),
    ]
    scratch = [
        pltpu.VMEM((n_pairs, 128, D_STATE), F32),
        pltpu.VMEM((SUBLANES, 1536), F32),
        pltpu.VMEM((tm, 2048), F32),
        pltpu.VMEM((tm, 1024), F32),
        pltpu.VMEM((tm, 1536), F32),
        pltpu.VMEM((tm, LANES), F32),
        pltpu.VMEM((LANES, tm), F32),
        pltpu.VMEM((tm, 2048), BF16),
    ]
    return pl.pallas_call(
        _mix_kernel,
        grid=(b, s // tm),
        in_specs=in_specs, out_specs=out_specs, out_shape=out_shape, scratch_shapes=scratch,
        compiler_params=_params(("parallel", "arbitrary")),
        name="mix_prompt",
    )(x, mod1, p["g_mix"], p["w_uv"], p["w_z"], p["w_xbc"], p["w_dt"], p["w_dtT"],
      p["dt_bias_row"], p["dt_bias_col"], p["a_log_row"], p["a_log_sq"], p["expand"],
      p["g_v"], p["w_s"], p["b_s_rep"], p["conv_w"], p["conv_b"], p["d_skip_exp"], p["g_ssm"], p["w_out"])


def _s1_kernel(x_ref, mod_ref, gmix_ref, wuv_ref, wz_ref, wxbc_ref, wdt_ref, dtb_ref, alog_ref, exp64_ref,
               gv_ref, ws0_ref, bs0_ref, cw_ref, cb_ref, sconv_ref,
               v_ref, cnew_ref, ya_ref, z_ref, xs_ref, bc_ref, xdtT_ref, dec_ref):
    x = x_ref[...]
    h = _rmsnorm(x, gmix_ref[...]) * (1.0 + mod_ref[1]) + mod_ref[0]
    hb = h.astype(BF16)
    uv = _dot(hb, wuv_ref[...])
    gv = gv_ref[...]
    for g in range(A_HEADS):
        sl = slice(g * A_HEAD_DIM, (g + 1) * A_HEAD_DIM)
        ug = _gelu(uv[:, sl])
        vn = _rmsnorm(_gelu(uv[:, 1024 + g * A_HEAD_DIM:1024 + (g + 1) * A_HEAD_DIM]), gv[:, sl])
        v_ref[:, sl] = vn
        s = ws0_ref[:, sl] * vn + bs0_ref[:, sl]
        ya_ref[:, sl] = (ug * s).astype(BF16)
    z_ref[...] = _dot(hb, wz_ref[...])
    raw = _dot(hb, wxbc_ref[...])
    cd = raw.shape[1]
    acc = cb_ref[...] + raw * cw_ref[CONV_W - 1:CONV_W, :]
    for k in range(CONV_W - 1):
        acc = acc + sconv_ref[:, k * cd:(k + 1) * cd] * cw_ref[k:k + 1, :]
    cnew_ref[:, 0:(CONV_W - 2) * cd] = sconv_ref[:, cd:(CONV_W - 1) * cd]
    cnew_ref[:, (CONV_W - 2) * cd:(CONV_W - 1) * cd] = raw
    xbc = _silu(acc)
    xs = xbc[:, 0:1024]
    xs_ref[...] = xs
    bc_ref[...] = xbc[:, 1024:1536]
    dt = _softplus(_dot(hb, wdt_ref[...]) + dtb_ref[...])
    dec_ref[...] = jnp.exp(dt * (-jnp.exp(alog_ref[...])))
    f1, f2, f3 = _split3(dt)
    dt64 = _dot(f1, exp64_ref[...]) + (_dot(f2, exp64_ref[...]) + _dot(f3, exp64_ref[...]))
    xdtT_ref[...] = (xs * dt64).T.astype(BF16)


def _s1_call(x, mod1, sconv, p):
    n, d = x.shape
    outs = [
        jax.ShapeDtypeStruct((n, 1024), F32),
        jax.ShapeDtypeStruct((n, 3 * 1536), F32),
        jax.ShapeDtypeStruct((n, 1024), BF16),
        jax.ShapeDtypeStruct((n, 1024), F32),
        jax.ShapeDtypeStruct((n, 1024), F32),
        jax.ShapeDtypeStruct((n, 512), F32),
        jax.ShapeDtypeStruct((1024, n), BF16),
        jax.ShapeDtypeStruct((n, LANES), F32),
    ]
    args = (x, mod1, p["g_mix"], p["w_uv"], p["w_z"], p["w_xbc"], p["w_dt"], p["dt_bias_row"], p["a_log_row"],
            p["expand64"], p["g_v"], p["w_s00"], p["b_s0"], p["conv_w"], p["conv_b"], sconv)
    return pl.pallas_call(
        _s1_kernel,
        grid=(1,),
        in_specs=[_full(a.shape) for a in args],
        out_specs=[_full(o.shape) for o in outs],
        out_shape=outs,
        compiler_params=_params(("arbitrary",)),
        name="mix_sample_in",
    )(*args)


def _s2_kernel(dec_ref, st_ref, xdtT_ref, bc_ref, cblk_ref, o_ref, y_ref):
    i = pl.program_id(0)
    bt = st_ref.shape[0]
    n = bc_ref.shape[0]
    half = B_HEADS * B_HEAD_DIM // B_GROUPS
    rowi = lax.broadcasted_iota(jnp.int32, (n, 1), 0)
    for bb in range(bt):
        b = i * bt + bb
        for g in range(B_GROUPS):
            bm = bc_ref[:, g * D_STATE:(g + 1) * D_STATE]
            rb = jnp.where(rowi == b, bm, 0.0).astype(BF16)
            outer = _dot(xdtT_ref[g * half:(g + 1) * half, :], rb)
            for k in range(B_HEADS // B_GROUPS):
                hd = g * (B_HEADS // B_GROUPS) + k
                sl = slice(k * B_HEAD_DIM, (k + 1) * B_HEAD_DIM)
                o_ref[bb, g, sl, :] = st_ref[bb, g, sl, :] * dec_ref[b, hd] + outer[sl, :]
            crow = cblk_ref[bb:bb + 1, 256 + g * D_STATE:256 + (g + 1) * D_STATE]
            c8 = jnp.broadcast_to(crow, (SUBLANES, D_STATE)).astype(BF16)
            yr = _dot_nt(c8, o_ref[bb, g].astype(BF16))
            y_ref[bb:bb + 1, g * half:(g + 1) * half] = yr[0:1]


def _s2_call(dec, state, xdtT, bc, bt=S2_TOKENS):
    n = state.shape[0]
    half = B_HEADS * B_HEAD_DIM // B_GROUPS
    st = state.reshape(n, B_GROUPS, half, D_STATE)
    grid_spec = pltpu.PrefetchScalarGridSpec(
        num_scalar_prefetch=1,
        grid=(n // bt,),
        in_specs=[
            pl.BlockSpec((bt, B_GROUPS, half, D_STATE), lambda i, d: (i, 0, 0, 0)),
            pl.BlockSpec(xdtT.shape, lambda i, d: (0, 0)),
            pl.BlockSpec(bc.shape, lambda i, d: (0, 0)),
            pl.BlockSpec((bt, bc.shape[1]), lambda i, d: (i, 0)),
        ],
        out_specs=[
            pl.BlockSpec((bt, B_GROUPS, half, D_STATE), lambda i, d: (i, 0, 0, 0)),
            pl.BlockSpec((bt, 1024), lambda i, d: (i, 0)),
        ],
    )
    new_state, y = pl.pallas_call(
        _s2_kernel,
        grid_spec=grid_spec,
        out_shape=[jax.ShapeDtypeStruct(st.shape, F32), jax.ShapeDtypeStruct((n, 1024), F32)],
        compiler_params=_params(("arbitrary",)),
        name="mix_sample_state",
    )(dec, st, xdtT, bc, bc)
    return new_state.reshape(state.shape), y


def _s3_kernel(x_ref, mod_ref, y_ref, xs_ref, z_ref, ya_ref, dsk_ref, gssm_ref, wout_ref, o_ref):
    y = (y_ref[...] + dsk_ref[...] * xs_ref[...]) * _silu(z_ref[...])
    half = B_HEADS * B_HEAD_DIM // B_GROUPS
    parts = [ya_ref[...]]
    for g in range(B_GROUPS):
        sl = slice(g * half, (g + 1) * half)
        parts.append(_rmsnorm(y[:, sl], gssm_ref[:, sl]).astype(BF16))
    cat = jnp.concatenate(parts, axis=1)
    o_ref[...] = x_ref[...] + mod_ref[2] * _dot(cat, wout_ref[...])


def _s3_call(x, mod1, y, xs, z, ya, p):
    args = (x, mod1, y, xs, z, ya, p["d_skip_exp"], p["g_ssm"], p["w_out"])
    return pl.pallas_call(
        _s3_kernel,
        grid=(1,),
        in_specs=[_full(a.shape) for a in args],
        out_specs=_full(x.shape),
        out_shape=jax.ShapeDtypeStruct(x.shape, F32),
        compiler_params=_params(("arbitrary",)),
        name="mix_sample_out",
    )(*args)


def _cand_blocks(a, b):
    row8 = lax.broadcasted_iota(jnp.int32, (SUBLANES, 1), 0)
    a8 = a[0:SUBLANES]
    blocks = []
    for j in range(SUBLANES):
        blk = a8 + b[j:j + 1]
        cnt = min(SUBLANES, (TOPK + 1) // (j + 1))
        if cnt < SUBLANES:
            blk = jnp.where(row8 < cnt, blk, NEG_INF)
        blocks.append(blk)
    blocks.append(a[0:1] + b[SUBLANES:TOPK])
    blocks.append(a[SUBLANES:TOPK] + b[0:1])
    last = jnp.where(row8 == 0, a[0:1] + b[TOPK:TOPK + 1], jnp.where(row8 == 1, a[TOPK:TOPK + 1] + b[0:1], NEG_INF))
    blocks.append(last)
    return blocks


def _merge_exchange_pairs(n):
    pairs = []
    t = max(1, math.ceil(math.log2(n)))
    p = 1 << (t - 1)
    while p > 0:
        q, r, d = 1 << (t - 1), 0, p
        while d > 0:
            pairs.extend((i, i + d) for i in range(n - d) if (i & p) == r)
            d, q, r = q - p, q >> 1, p
        p >>= 1
    return pairs


def _sort_blocks_desc(blocks):
    blocks = list(blocks)
    for i, j in _merge_exchange_pairs(len(blocks)):
        hi = jnp.maximum(blocks[i], blocks[j])
        blocks[j] = jnp.minimum(blocks[i], blocks[j])
        blocks[i] = hi
    return blocks


def _pop_largest(blocks, k):
    blocks = list(blocks)
    nb = len(blocks)
    rows = []
    for t in range(k):
        m = jnp.max(blocks[0], axis=0, keepdims=True)
        rows.append(m)
        need = min(k - t - 1, nb)
        if need == 0:
            break
        sel = blocks[0] == m
        for j in range(need):
            nxt = blocks[j + 1] if j + 1 < nb else NEG_INF
            blocks[j] = jnp.where(sel, nxt, blocks[j])
    return rows


def _peer_kernel(x_ref, mod_ref, gffn_ref, gfin_ref, wqT_ref, keys_ref, u_ref, vt_ref, y_ref,
                 h2_ref, q_ref, s1_ref, s2_ref, e1_ref, e2_ref, tops_ref, act_ref, coef_ref, outT_ref,
                 *, n_chunks):
    c = pl.program_id(1)
    n_blk = u_ref.shape[0] // N_KEYS

    @pl.when(c == 0)
    def _():
        x = x_ref[...]
        h = _rmsnorm(x, gffn_ref[...]) * (1.0 + mod_ref[1]) + mod_ref[0]
        ht = h.T.astype(BF16)
        h2_ref[...] = ht
        qh = wqT_ref.shape[0] // 2
        q_ref[0:qh, :] = _dot(wqT_ref[0:qh, :], ht)
        q_ref[qh:, :] = _dot(wqT_ref[qh:, :], ht)
        outT_ref[...] = jnp.zeros_like(outT_ref)

        def one_head(hh, top_ref):
            scores = []
            for s in range(2):
                r0 = pl.multiple_of((2 * hh + s) * N_KEYS, N_KEYS)
                qs = q_ref[pl.ds(r0, N_KEYS), :].astype(BF16)
                sc = _dot(keys_ref[2 * hh + s], qs) * LOG2E
                if s == 0:
                    s1_ref[hh] = sc
                else:
                    s2_ref[hh] = sc
                scores.append(sc)
            n_tok = x_ref.shape[0]
            strip = min(n_tok, PEER_SELECT_STRIP)
            for c0 in range(0, n_tok, strip):
                cols = slice(c0, c0 + strip)
                for s in range(2):
                    srt = _sort_blocks_desc([scores[s][j * SUBLANES:(j + 1) * SUBLANES, cols]
                                             for j in range(N_KEYS // SUBLANES)])
                    for k, m in enumerate(_pop_largest(srt, TOPK + 1)):
                        top_ref[s, k:k + 1, cols] = m
                a = top_ref[0, 0:TOPK + 1, cols]
                b = top_ref[1, 0:TOPK + 1, cols]
                best = a[0:1] + b[0:1]
                pops = _pop_largest(_sort_blocks_desc(_cand_blocks(a, b)), TOPK + 1)
                zsum = None
                for m in pops[:TOPK]:
                    e = jnp.exp2(m - best)
                    zsum = e if zsum is None else zsum + e
                thr = jnp.where(pops[TOPK] == NEG_INF, pops[TOPK - 1], 0.5 * (pops[TOPK - 1] + pops[TOPK]))
                s1 = s1_ref[hh, :, cols]
                s1_ref[hh, :, cols] = thr - s1
                e1_ref[hh, :, cols] = 0.5 * jnp.exp2(s1 - (a[0:1] + jnp.log2(zsum)))
                e2_ref[hh, :, cols] = jnp.exp2(scores[1][:, cols] - b[0:1])

        def head_body(hp, carry):
            for e in range(PEER_HEADS_PER_TRIP):
                one_head(PEER_HEADS_PER_TRIP * hp + e, tops_ref.at[e])
            return carry

        lax.fori_loop(0, R_HEADS // PEER_HEADS_PER_TRIP, head_body, 0)

    eh = u_ref.shape[0] // 2
    dh = vt_ref.shape[0] // 2

    @pl.when(c < n_chunks)
    def _():
        act_w = act_ref.at[c % 2]
        act_w[0:eh, :] = _dot(u_ref[0:eh, :], h2_ref[...])
        act_w[eh:, :] = _dot(u_ref[eh:, :], h2_ref[...])

    @pl.when(c >= 1)
    def _():
        act_r = act_ref.at[(c - 1) % 2]
        i1 = pl.multiple_of((c - 1) * n_blk, n_blk)
        u_rows = [s1_ref[hh, pl.ds(i1, n_blk), :] for hh in range(R_HEADS)]
        e1_rows = [e1_ref[hh, pl.ds(i1, n_blk), :] for hh in range(R_HEADS)]
        for r in range(n_blk):
            acc = None
            for hh in range(R_HEADS):
                gate = jnp.where(s2_ref[hh] >= u_rows[hh][r:r + 1, :], e2_ref[hh] * e1_rows[hh][r:r + 1, :], 0.0)
                acc = gate if acc is None else acc + gate
            blk = slice(r * N_KEYS, (r + 1) * N_KEYS)
            x = act_r[blk, :]
            coef_ref[blk, :] = (acc * (x * (1.0 + lax.erf(x * (1.0 / math.sqrt(2.0)))))).astype(BF16)
        outT_ref[0:dh, :] += _dot(vt_ref[0:dh, :], coef_ref[...])
        outT_ref[dh:, :] += _dot(vt_ref[dh:, :], coef_ref[...])

    @pl.when(c == n_chunks)
    def _():
        x2 = x_ref[...] + mod_ref[2] * outT_ref[...].T
        y_ref[...] = _rmsnorm(x2, gfin_ref[...]) * (1.0 + mod_ref[4]) + mod_ref[3]


def _peer_call(x, mod5, tiles_per_row, p, tl, ec=PEER_EXPERTS):
    t, d = x.shape
    n_exp = p["expert_u"].shape[0]
    r = mod5.shape[2]
    scratch = [
        pltpu.VMEM((d, tl), BF16),
        pltpu.VMEM((R_HEADS * 2 * N_KEYS, tl), F32),
        pltpu.VMEM((R_HEADS, N_KEYS, tl), F32),
        pltpu.VMEM((R_HEADS, N_KEYS, tl), F32),
        pltpu.VMEM((R_HEADS, N_KEYS, tl), F32),
        pltpu.VMEM((R_HEADS, N_KEYS, tl), F32),
        pltpu.VMEM((PEER_HEADS_PER_TRIP, 2, TOPK + SUBLANES, tl), F32),
        pltpu.VMEM((2, ec, tl), F32),
        pltpu.VMEM((ec, tl), BF16),
        pltpu.VMEM((d, tl), F32),
    ]
    n_chunks = n_exp // ec
    return pl.pallas_call(
        functools.partial(_peer_kernel, n_chunks=n_chunks),
        grid=(t // tl, n_chunks + 1),
        in_specs=[
            pl.BlockSpec((tl, d), lambda i, c: (i, 0)),
            pl.BlockSpec((None, 5, r, d), lambda i, c: (i // tiles_per_row, 0, 0, 0)),
            _full((1, d)), _full((1, d)),
            _full(p["w_qT"].shape), _full(p["keys"].shape),
            pl.BlockSpec((ec, d), lambda i, c: (jnp.minimum(c, n_chunks - 1), 0)),
            pl.BlockSpec((d, ec), lambda i, c: (0, jnp.maximum(c - 1, 0))),
        ],
        out_specs=pl.BlockSpec((tl, d), lambda i, c: (i, 0)),
        out_shape=jax.ShapeDtypeStruct((t, d), F32),
        scratch_shapes=scratch,
        compiler_params=_params(("parallel", "arbitrary")),
        name="peer",
    )(x, mod5, p["g_ffn"], p["g_final"], p["w_qT"], p["keys"], p["expert_u"], p["expert_vT"])


def _prep(w_in, g_mix, g_v, w_s, b_s, conv_w, conv_b, dt_bias, a_log, d_skip, g_ssm, w_out, g_ffn, w_q,
          sub_keys, expert_u, expert_v, g_final):
    d = w_in.shape[0]
    o_z = 2 * 1024
    o_x = o_z + 1024
    o_dt = o_x + 1536
    w_dt = jnp.zeros((d, LANES), F32).at[:, :B_HEADS].set(w_in[:, o_dt:o_dt + B_HEADS])
    pad16 = lambda v: jnp.zeros((LANES,), F32).at[:B_HEADS].set(v)
    heads = jnp.arange(LANES)[:, None]
    p = {
        "g_mix": g_mix.reshape(1, d),
        "w_uv": w_in[:, 0:o_z].astype(BF16),
        "w_z": w_in[:, o_z:o_x].astype(BF16),
        "w_xbc": w_in[:, o_x:o_dt].astype(BF16),
        "w_dt": w_dt.astype(BF16),
        "w_dtT": w_dt.T.astype(BF16),
        "dt_bias_row": pad16(dt_bias).reshape(1, LANES),
        "dt_bias_col": pad16(dt_bias).reshape(LANES, 1),
        "a_log_row": pad16(a_log).reshape(1, LANES),
        "a_log_sq": jnp.broadcast_to(pad16(a_log).reshape(LANES, 1), (LANES, LANES)),
        "expand": (heads == jnp.arange(B_HEADS * LANES)[None, :] // LANES).astype(BF16),
        "expand64": (heads == jnp.arange(B_HEADS * B_HEAD_DIM)[None, :] // B_HEAD_DIM).astype(BF16),
        "g_v": g_v.reshape(1, -1),
        "w_s": w_s,
        "b_s_rep": jnp.broadcast_to(b_s[:, :, None], (A_HEADS, CHUNK, CHUNK)),
        "w_s00": jnp.repeat(w_s[:, 0, 0], A_HEAD_DIM).reshape(1, -1),
        "b_s0": jnp.repeat(b_s[:, 0], A_HEAD_DIM).reshape(1, -1),
        "conv_w": conv_w,
        "conv_b": conv_b.reshape(1, -1),
        "d_skip_exp": jnp.repeat(d_skip, B_HEAD_DIM).reshape(1, -1),
        "g_ssm": g_ssm.reshape(1, -1),
        "w_out": w_out.astype(BF16),
        "g_ffn": g_ffn.reshape(1, d),
        "g_final": g_final.reshape(1, d),
        "w_qT": w_q.T.astype(BF16),
        "keys": sub_keys.reshape(R_HEADS * 2, N_KEYS, -1).astype(BF16),
        "expert_u": expert_u.astype(BF16),
        "expert_vT": expert_v.T.astype(BF16),
    }
    return p


def kernel(x_prompt, x_sample, c_prompt, c_sample, state_ssm, state_conv, w_ada, b_ada, g_mix, w_in, g_v, w_s, b_s,
           conv_w, conv_b, dt_bias, a_log, d_skip, g_ssm, w_out, g_ffn, w_q, sub_keys, expert_u, expert_v,
           w_ada_f, b_ada_f, g_final):
    assert w_ada.shape[0] == 1, "single-layer trunk"
    bp, seq, d = x_prompt.shape
    ns = x_sample.shape[0]
    p = _prep(w_in[0], g_mix[0], g_v[0], w_s[0], b_s[0], conv_w[0], conv_b[0], dt_bias[0], a_log[0], d_skip[0],
              g_ssm[0], w_out[0], g_ffn[0], w_q[0], sub_keys[0], expert_u[0], expert_v[0], g_final)

    c_all = jnp.concatenate([c_prompt, c_sample], axis=0)
    mod = _ada_call(c_all, w_ada[0], b_ada[0]).reshape(bp + ns, N_MOD, d)
    modf = _ada_call(c_all, w_ada_f, b_ada_f).reshape(bp + ns, 2, d)
    mod5 = jnp.concatenate([mod[:, 3:6], modf], axis=1)

    x1_p, ssm_p, ctail_p = _mix_call(x_prompt, mod[:bp, 0:3].reshape(bp, 3, 1, d), p)
    y_p = _peer_call(x1_p.reshape(bp * seq, d), mod5[:bp].reshape(bp, 5, 1, d), seq // PEER_TOKENS, p, PEER_TOKENS)

    xs_in = x_sample.reshape(ns, d)
    mod1_s = jnp.transpose(mod[bp:, 0:3], (1, 0, 2))
    sconv = state_conv[0].reshape(ns, -1)
    v_s, cnew_s, ya_s, z_s, xc_s, bc_s, xdtT_s, dec_s = _s1_call(xs_in, mod1_s, sconv, p)
    ssm_s, yssd_s = _s2_call(dec_s[:, :B_HEADS], state_ssm[0], xdtT_s, bc_s)
    x1_s = _s3_call(xs_in, mod1_s, yssd_s, xc_s, z_s, ya_s, p)
    mod5_s = jnp.transpose(mod5[bp:], (1, 0, 2)).reshape(1, 5, ns, d)
    y_s = _peer_call(x1_s, mod5_s, 1, p, ns)

    return (
        y_p.reshape(bp, seq, d),
        y_s.reshape(ns, 1, d),
        ssm_p.reshape(1, bp, B_HEADS, B_HEAD_DIM, D_STATE),
        ctail_p[:, SUBLANES - (CONV_W - 1):, :].reshape(1, bp, CONV_W - 1, -1),
        ssm_s.reshape(1, ns, B_HEADS, B_HEAD_DIM, D_STATE),
        cnew_s.reshape(1, ns, CONV_W - 1, -1),
        v_s.reshape(1, ns, 1, -1),
    )
```

```python
import functools
import math

import jax
import jax.numpy as jnp
from jax import lax
from jax.experimental import pallas as pl
from jax.experimental.pallas import tpu as pltpu

F32 = jnp.float32
BF16 = jnp.bfloat16
NEG_INF = float("-inf")

EPS = 1e-6
A_HEADS = 8
A_HEAD_DIM = 128
CHUNK = 128
B_HEADS = 16
B_HEAD_DIM = 64
B_GROUPS = 2
D_STATE = 128
CONV_W = 4
N_KEYS = 128
R_HEADS = 8
TOPK = 16
N_MOD = 6

LANES = 128
SUBLANES = 8
VMEM_LIMIT = 56 * 1024 * 1024

MIX_ROWS = 512
PEER_TOKENS = 256
PEER_EXPERTS = 2048
PEER_SELECT_STRIP = 256
PEER_HEADS_PER_TRIP = 4
LOG2E = 1.4426950408889634
S2_TOKENS = 8


def _dot(a, b):
    return jnp.dot(a, b, preferred_element_type=F32)


def _dot_nt(a, b):
    return lax.dot_general(a, b, (((1,), (1,)), ((), ())), preferred_element_type=F32)


def _dot_tn(a, b):
    return lax.dot_general(a, b, (((0,), (0,)), ((), ())), preferred_element_type=F32)


def _split2(x):
    hi = x.astype(BF16)
    lo = (x - hi.astype(F32)).astype(BF16)
    return hi, lo


def _split3(x):
    hi = x.astype(BF16)
    r = x - hi.astype(F32)
    mid = r.astype(BF16)
    lo = (r - mid.astype(F32)).astype(BF16)
    return hi, mid, lo


def _dot_x3(a, b):
    a1, a2 = _split2(a)
    b1, b2 = _split2(b)
    return _dot(a1, b1) + (_dot(a1, b2) + _dot(a2, b1))


def _silu(x):
    return x / (1.0 + jnp.exp(-x))


def _gelu(x):
    return 0.5 * x * (1.0 + lax.erf(x * (1.0 / math.sqrt(2.0))))


def _softplus(x):
    return jnp.maximum(x, 0.0) + jnp.log1p(jnp.exp(-jnp.abs(x)))


def _rmsnorm(x, g):
    return x * lax.rsqrt(jnp.mean(x * x, axis=-1, keepdims=True) + EPS) * g


def _full(shape):
    nd = len(shape)
    return pl.BlockSpec(shape, lambda *_: (0,) * nd, pipeline_mode=pl.Buffered(1))


def _dot2(a, w_ref):
    h = w_ref.shape[1] // 2
    return jnp.concatenate([_dot(a, w_ref[:, :h]), _dot(a, w_ref[:, h:])], axis=1)


def _params(sem):
    return pltpu.CompilerParams(dimension_semantics=sem, vmem_limit_bytes=VMEM_LIMIT)


def _ada_kernel(c_ref, w_ref, b_ref, o_ref):
    o_ref[...] = _dot_x3(_silu(c_ref[...]), w_ref[...]) + b_ref[...]


def _ada_call(c, w, b, tn=512):
    m, k = c.shape
    n = w.shape[1]
    return pl.pallas_call(
        _ada_kernel,
        grid=(n // tn,),
        in_specs=[_full((m, k)), pl.BlockSpec((k, tn), lambda j: (0, j)), pl.BlockSpec((1, tn), lambda j: (0, j))],
        out_specs=pl.BlockSpec((m, tn), lambda j: (0, j)),
        out_shape=jax.ShapeDtypeStruct((m, n), F32),
        compiler_params=_params(("parallel",)),
        name="ada",
    )(c, w, b.reshape(1, n))


def _gate_mlp_chunk(u, v, gv, ws_ref, bs_ref, causal):
    ys, vs = [], []
    for g in range(A_HEADS):
        sl = slice(g * A_HEAD_DIM, (g + 1) * A_HEAD_DIM)
        ug = _gelu(u[:, sl])
        vn = _rmsnorm(_gelu(v[:, sl]), gv[:, sl])
        w = jnp.where(causal, ws_ref[g], 0.0).astype(BF16)
        s = _dot(w, vn.astype(BF16)) + bs_ref[g]
        ys.append((ug * s).astype(BF16))
        vs.append(vn)
    return ys, vs


def _mix_kernel(x_ref, mod_ref, gmix_ref, wuv_ref, wz_ref, wxbc_ref, wdt_ref, wdtT_ref,
                dtb_ref, dtbT_ref, alog_ref, alogT_ref, gv_ref, ws_ref, bs_ref,
                cw_ref, cb_ref, dsk_ref, gssm_ref, wout_ref,
                o_ref, ssm_ref, ctail_ref,
                hst_ref, tail_ref, uv_ref, z_ref, xs_ref, dt_ref, dtT_ref, ycat_ref):
    t = pl.program_id(1)
    tm = x_ref.shape[0]
    n_chunks = tm // CHUNK

    @pl.when(t == 0)
    def _():
        hst_ref[...] = jnp.zeros_like(hst_ref)
        tail_ref[...] = jnp.zeros_like(tail_ref)

    x = x_ref[...]
    h = _rmsnorm(x, gmix_ref[...]) * (1.0 + mod_ref[1]) + mod_ref[0]
    hb = h.astype(BF16)
    uv_ref[...] = _dot2(hb, wuv_ref)
    z_ref[...] = _dot2(hb, wz_ref)
    dt_ref[...] = _softplus(_dot(hb, wdt_ref[...]) + dtb_ref[...])
    dtT_ref[...] = _softplus(_dot_nt(wdtT_ref[...], hb) + dtbT_ref[...])

    cur = _dot2(hb, wxbc_ref)
    prev = tail_ref[...]
    row8 = lax.broadcasted_iota(jnp.int32, (SUBLANES, 1), 0)
    acc = cb_ref[...] + cur * cw_ref[CONV_W - 1:CONV_W, :]
    for k in range(1, CONV_W):
        r = pltpu.roll(cur, k, axis=0)
        head = jnp.where(row8 < k, pltpu.roll(prev, k, axis=0), r[0:SUBLANES])
        shifted = jnp.concatenate([head, r[SUBLANES:]], axis=0)
        acc = acc + shifted * cw_ref[CONV_W - 1 - k:CONV_W - k, :]
    xs_ref[...] = _silu(acc)
    new_tail = cur[tm - SUBLANES:tm]
    tail_ref[...] = new_tail
    ctail_ref[...] = new_tail

    ri = lax.broadcasted_iota(jnp.int32, (CHUNK, CHUNK), 0)
    ci = lax.broadcasted_iota(jnp.int32, (CHUNK, CHUNK), 1)
    causal = ri >= ci
    tril = jnp.where(causal, 1.0, 0.0).astype(BF16)
    triu = jnp.where(ri <= ci, 1.0, 0.0).astype(BF16)
    low_half = ri < B_HEAD_DIM
    low_lane = ci < B_HEAD_DIM
    a_col = -jnp.exp(alog_ref[...])
    a_row = -jnp.exp(alogT_ref[...])

    def chunk_body(c, carry):
        r0 = pl.multiple_of(c * CHUNK, CHUNK)
        rows = pl.ds(r0, CHUNK)

        ys, _ = _gate_mlp_chunk(uv_ref[rows, 0:1024], uv_ref[rows, 1024:2048], gv_ref[...], ws_ref, bs_ref, causal)
        for g in range(A_HEADS):
            ycat_ref[rows, g * A_HEAD_DIM:(g + 1) * A_HEAD_DIM] = ys[g]

        dt = dt_ref[rows, :]
        d_a = dt * a_col
        h1, h2, h3 = _split3(d_a)
        acum = _dot(tril, h1) + (_dot(tril, h2) + _dot(tril, h3))
        dt_t = dtT_ref[:, rows]
        g1, g2, g3 = _split3(dt_t * a_row)
        acum_t = _dot(g1, triu) + (_dot(g2, triu) + _dot(g3, triu))

        xs = xs_ref[rows, 0:1024]
        zz = z_ref[rows, :]
        y_pairs = []
        for g in range(B_GROUPS):
            bg = xs_ref[rows, 1024 + g * D_STATE:1024 + (g + 1) * D_STATE]
            cg = xs_ref[rows, 1280 + g * D_STATE:1280 + (g + 1) * D_STATE]
            cbm = _dot_nt(cg.astype(BF16), bg.astype(BF16))
            for pr in range(B_HEADS // B_GROUPS // 2):
                pair = g * 4 + pr
                xpair = xs[:, pair * 128:(pair + 1) * 128]
                hprev = hst_ref[pair]
                gds, xhs, css, hps, bscs, alasts = [], [], [], [], [], []
                for e in range(2):
                    hd = pair * 2 + e
                    ab = jnp.broadcast_to(acum[:, hd:hd + 1], (CHUNK, CHUNK))
                    seg = ab - acum_t[hd:hd + 1, :]
                    lm = jnp.exp(jnp.where(causal, seg, NEG_INF))
                    gds.append((cbm * lm * dt_t[hd:hd + 1, :]).astype(BF16))
                    sel_l = low_lane if e == 0 else jnp.logical_not(low_lane)
                    sel_r = low_half if e == 0 else jnp.logical_not(low_half)
                    xhs.append(jnp.where(sel_l, xpair, 0.0).astype(BF16))
                    hps.append(jnp.where(sel_r, hprev, 0.0).astype(BF16))
                    css.append((cg * jnp.exp(ab)).astype(BF16))
                    alast = ab[CHUNK - 1:CHUNK, :]
                    dt_col = jnp.broadcast_to(dt[:, hd:hd + 1], (CHUNK, CHUNK))
                    bscs.append((bg * (jnp.exp(alast - ab) * dt_col)).astype(BF16))
                    alasts.append(alast)
                x2 = jnp.concatenate(xhs, axis=0)
                y_pairs.append(_dot(jnp.concatenate(gds, axis=1), x2)
                               + _dot_nt(jnp.concatenate(css, axis=1), jnp.concatenate(hps, axis=1)))
                st = _dot_tn(x2, jnp.concatenate(bscs, axis=0))
                decay = jnp.exp(jnp.where(low_half, alasts[0], alasts[1]))
                hst_ref[pair] = hprev * decay + st
        y = jnp.concatenate(y_pairs, axis=1) + dsk_ref[...] * xs
        y = y * _silu(zz)
        half = B_HEADS * B_HEAD_DIM // B_GROUPS
        for g in range(B_GROUPS):
            sl = slice(g * half, (g + 1) * half)
            ycat_ref[rows, 1024 + g * half:1024 + (g + 1) * half] = _rmsnorm(y[:, sl], gssm_ref[:, sl]).astype(BF16)
        return carry

    lax.fori_loop(0, n_chunks, chunk_body, 0)

    o_ref[...] = x + mod_ref[2] * _dot2(ycat_ref[...], wout_ref)

    @pl.when(t == pl.num_programs(1) - 1)
    def _():
        ssm_ref[...] = hst_ref[...]


def _mix_call(x, mod1, p, tm=MIX_ROWS):
    b, s, d = x.shape
    n_pairs = B_HEADS // 2
    in_specs = [
        pl.BlockSpec((None, tm, d), lambda i, t: (i, t, 0)),
        pl.BlockSpec((None, 3, 1, d), lambda i, t: (i, 0, 0, 0)),
        _full((1, d)),
        _full(p["w_uv"].shape), _full(p["w_z"].shape), _full(p["w_xbc"].shape),
        _full(p["w_dt"].shape), _full(p["w_dtT"].shape),
        _full((1, LANES)), _full((LANES, 1)), _full((1, LANES)), _full((LANES, LANES)),
        _full((1, 1024)), _full((A_HEADS, CHUNK, CHUNK)), _full((A_HEADS, CHUNK, CHUNK)),
        _full((CONV_W, 1536)), _full((1, 1536)), _full((1, 1024)), _full((1, 1024)),
        _full(p["w_out"].shape),
    ]
    out_specs = [
        pl.BlockSpec((None, tm, d), lambda i, t: (i, t, 0)),
        pl.BlockSpec((None, n_pairs, 128, D_STATE), lambda i, t: (i, 0, 0, 0)),
        pl.BlockSpec((None, SUBLANES, 1536), lambda i, t: (i, 0, 0)),
    ]
    out_shape = [
        jax.ShapeDtypeStruct((b, s, d), F32),
        jax.ShapeDtypeStruct((b, n_pairs, 128, D_STATE), F32),
        jax.ShapeDtypeStruct((b, SUBLANES, 1536), F32),
    ]
    scratch = [
        pltpu.VMEM((n_pairs, 128, D_STATE), F32),
        pltpu.VMEM((SUBLANES, 1536), F32),
        pltpu.VMEM((tm, 2048), F32),
        pltpu.VMEM((tm, 1024), F32),
        pltpu.VMEM((tm, 1536), F32),
        pltpu.VMEM((tm, LANES), F32),
        pltpu.VMEM((LANES, tm), F32),
        pltpu.VMEM((tm, 2048), BF16),
    ]
    return pl.pallas_call(
        _mix_kernel,
        grid=(b, s // tm),
        in_specs=in_specs, out_specs=out_specs, out_shape=out_shape, scratch_shapes=scratch,
        compiler_params=_params(("parallel", "arbitrary")),
        name="mix_prompt",
    )(x, mod1, p["g_mix"], p["w_uv"], p["w_z"], p["w_xbc"], p["w_dt"], p["w_dtT"],
      p["dt_bias_row"], p["dt_bias_col"], p["a_log_row"], p["a_log_sq"],
      p["g_v"], p["w_s"], p["b_s_rep"], p["conv_w"], p["conv_b"], p["d_skip_exp"], p["g_ssm"], p["w_out"])


def _s1_kernel(x_ref, mod_ref, gmix_ref, wuv_ref, wz_ref, wxbc_ref, wdt_ref, dtb_ref, alog_ref, exp64_ref,
               gv_ref, ws0_ref, bs0_ref, cw_ref, cb_ref, sconv_ref,
               v_ref, cnew_ref, ya_ref, z_ref, xs_ref, bc_ref, xdtT_ref, dec_ref):
    x = x_ref[...]
    h = _rmsnorm(x, gmix_ref[...]) * (1.0 + mod_ref[1]) + mod_ref[0]
    hb = h.astype(BF16)
    uv = _dot(hb, wuv_ref[...])
    gv = gv_ref[...]
    for g in range(A_HEADS):
        sl = slice(g * A_HEAD_DIM, (g + 1) * A_HEAD_DIM)
        ug = _gelu(uv[:, sl])
        vn = _rmsnorm(_gelu(uv[:, 1024 + g * A_HEAD_DIM:1024 + (g + 1) * A_HEAD_DIM]), gv[:, sl])
        v_ref[:, sl] = vn
        s = ws0_ref[:, sl] * vn + bs0_ref[:, sl]
        ya_ref[:, sl] = (ug * s).astype(BF16)
    z_ref[...] = _dot(hb, wz_ref[...])
    raw = _dot(hb, wxbc_ref[...])
    cd = raw.shape[1]
    acc = cb_ref[...] + raw * cw_ref[CONV_W - 1:CONV_W, :]
    for k in range(CONV_W - 1):
        acc = acc + sconv_ref[:, k * cd:(k + 1) * cd] * cw_ref[k:k + 1, :]
    cnew_ref[:, 0:(CONV_W - 2) * cd] = sconv_ref[:, cd:(CONV_W - 1) * cd]
    cnew_ref[:, (CONV_W - 2) * cd:(CONV_W - 1) * cd] = raw
    xbc = _silu(acc)
    xs = xbc[:, 0:1024]
    xs_ref[...] = xs
    bc_ref[...] = xbc[:, 1024:1536]
    dt = _softplus(_dot(hb, wdt_ref[...]) + dtb_ref[...])
    dec_ref[...] = jnp.exp(dt * (-jnp.exp(alog_ref[...])))
    f1, f2, f3 = _split3(dt)
    dt64 = _dot(f1, exp64_ref[...]) + (_dot(f2, exp64_ref[...]) + _dot(f3, exp64_ref[...]))
    xdtT_ref[...] = (xs * dt64).T.astype(BF16)


def _s1_call(x, mod1, sconv, p):
    n, d = x.shape
    outs = [
        jax.ShapeDtypeStruct((n, 1024), F32),
        jax.ShapeDtypeStruct((n, 3 * 1536), F32),
        jax.ShapeDtypeStruct((n, 1024), BF16),
        jax.ShapeDtypeStruct((n, 1024), F32),
        jax.ShapeDtypeStruct((n, 1024), F32),
        jax.ShapeDtypeStruct((n, 512), F32),
        jax.ShapeDtypeStruct((1024, n), BF16),
        jax.ShapeDtypeStruct((n, LANES), F32),
    ]
    args = (x, mod1, p["g_mix"], p["w_uv"], p["w_z"], p["w_xbc"], p["w_dt"], p["dt_bias_row"], p["a_log_row"],
            p["expand64"], p["g_v"], p["w_s00"], p["b_s0"], p["conv_w"], p["conv_b"], sconv)
    return pl.pallas_call(
        _s1_kernel,
        grid=(1,),
        in_specs=[_full(a.shape) for a in args],
        out_specs=[_full(o.shape) for o in outs],
        out_shape=outs,
        compiler_params=_params(("arbitrary",)),
        name="mix_sample_in",
    )(*args)


def _s2_kernel(dec_ref, st_ref, xdtT_ref, bc_ref, cblk_ref, o_ref, y_ref):
    i = pl.program_id(0)
    bt = st_ref.shape[0]
    n = bc_ref.shape[0]
    half = B_HEADS * B_HEAD_DIM // B_GROUPS
    rowi = lax.broadcasted_iota(jnp.int32, (n, 1), 0)
    for bb in range(bt):
        b = i * bt + bb
        for g in range(B_GROUPS):
            bm = bc_ref[:, g * D_STATE:(g + 1) * D_STATE]
            rb = jnp.where(rowi == b, bm, 0.0).astype(BF16)
            outer = _dot(xdtT_ref[g * half:(g + 1) * half, :], rb)
            for k in range(B_HEADS // B_GROUPS):
                hd = g * (B_HEADS // B_GROUPS) + k
                sl = slice(k * B_HEAD_DIM, (k + 1) * B_HEAD_DIM)
                o_ref[bb, g, sl, :] = st_ref[bb, g, sl, :] * dec_ref[b, hd] + outer[sl, :]
            crow = cblk_ref[bb:bb + 1, 256 + g * D_STATE:256 + (g + 1) * D_STATE]
            c8 = jnp.broadcast_to(crow, (SUBLANES, D_STATE)).astype(BF16)
            yr = _dot_nt(c8, o_ref[bb, g].astype(BF16))
            y_ref[bb:bb + 1, g * half:(g + 1) * half] = yr[0:1]


def _s2_call(dec, state, xdtT, bc, bt=S2_TOKENS):
    n = state.shape[0]
    half = B_HEADS * B_HEAD_DIM // B_GROUPS
    st = state.reshape(n, B_GROUPS, half, D_STATE)
    grid_spec = pltpu.PrefetchScalarGridSpec(
        num_scalar_prefetch=1,
        grid=(n // bt,),
        in_specs=[
            pl.BlockSpec((bt, B_GROUPS, half, D_STATE), lambda i, d: (i, 0, 0, 0)),
            pl.BlockSpec(xdtT.shape, lambda i, d: (0, 0)),
            pl.BlockSpec(bc.shape, lambda i, d: (0, 0)),
            pl.BlockSpec((bt, bc.shape[1]), lambda i, d: (i, 0)),
        ],
        out_specs=[
            pl.BlockSpec((bt, B_GROUPS, half, D_STATE), lambda i, d: (i, 0, 0, 0)),
            pl.BlockSpec((bt, 1024), lambda i, d: (i, 0)),
        ],
    )
    new_state, y = pl.pallas_call(
        _s2_kernel,
        grid_spec=grid_spec,
        out_shape=[jax.ShapeDtypeStruct(st.shape, F32), jax.ShapeDtypeStruct((n, 1024), F32)],
        compiler_params=_params(("arbitrary",)),
        name="mix_sample_state",
    )(dec, st, xdtT, bc, bc)
    return new_state.reshape(state.shape), y


def _s3_kernel(x_ref, mod_ref, y_ref, xs_ref, z_ref, ya_ref, dsk_ref, gssm_ref, wout_ref, o_ref):
    y = (y_ref[...] + dsk_ref[...] * xs_ref[...]) * _silu(z_ref[...])
    half = B_HEADS * B_HEAD_DIM // B_GROUPS
    parts = [ya_ref[...]]
    for g in range(B_GROUPS):
        sl = slice(g * half, (g + 1) * half)
        parts.append(_rmsnorm(y[:, sl], gssm_ref[:, sl]).astype(BF16))
    cat = jnp.concatenate(parts, axis=1)
    o_ref[...] = x_ref[...] + mod_ref[2] * _dot(cat, wout_ref[...])


def _s3_call(x, mod1, y, xs, z, ya, p):
    args = (x, mod1, y, xs, z, ya, p["d_skip_exp"], p["g_ssm"], p["w_out"])
    return pl.pallas_call(
        _s3_kernel,
        grid=(1,),
        in_specs=[_full(a.shape) for a in args],
        out_specs=_full(x.shape),
        out_shape=jax.ShapeDtypeStruct(x.shape, F32),
        compiler_params=_params(("arbitrary",)),
        name="mix_sample_out",
    )(*args)


def _cand_blocks(a, b):
    row8 = lax.broadcasted_iota(jnp.int32, (SUBLANES, 1), 0)
    a8 = a[0:SUBLANES]
    blocks = []
    for j in range(SUBLANES):
        blk = a8 + b[j:j + 1]
        cnt = min(SUBLANES, (TOPK + 1) // (j + 1))
        if cnt < SUBLANES:
            blk = jnp.where(row8 < cnt, blk, NEG_INF)
        blocks.append(blk)
    blocks.append(a[0:1] + b[SUBLANES:TOPK])
    blocks.append(a[SUBLANES:TOPK] + b[0:1])
    last = jnp.where(row8 == 0, a[0:1] + b[TOPK:TOPK + 1], jnp.where(row8 == 1, a[TOPK:TOPK + 1] + b[0:1], NEG_INF))
    blocks.append(last)
    return blocks


def _merge_exchange_pairs(n):
    pairs = []
    t = max(1, math.ceil(math.log2(n)))
    p = 1 << (t - 1)
    while p > 0:
        q, r, d = 1 << (t - 1), 0, p
        while d > 0:
            pairs.extend((i, i + d) for i in range(n - d) if (i & p) == r)
            d, q, r = q - p, q >> 1, p
        p >>= 1
    return pairs


def _sort_blocks_desc(blocks):
    blocks = list(blocks)
    for i, j in _merge_exchange_pairs(len(blocks)):
        hi = jnp.maximum(blocks[i], blocks[j])
        blocks[j] = jnp.minimum(blocks[i], blocks[j])
        blocks[i] = hi
    return blocks


def _pop_largest(blocks, k):
    blocks = list(blocks)
    nb = len(blocks)
    rows = []
    for t in range(k):
        m = jnp.max(blocks[0], axis=0, keepdims=True)
        rows.append(m)
        need = min(k - t - 1, nb)
        if need == 0:
            break
        sel = blocks[0] == m
        for j in range(need):
            nxt = blocks[j + 1] if j + 1 < nb else NEG_INF
            blocks[j] = jnp.where(sel, nxt, blocks[j])
    return rows


def _peer_kernel(x_ref, mod_ref, gffn_ref, gfin_ref, wqT_ref, keys_ref, u_ref, vt_ref, y_ref,
                 h2_ref, q_ref, s1_ref, s2_ref, e1_ref, e2_ref, tops_ref, act_ref, coef_ref, outT_ref,
                 *, n_chunks):
    c = pl.program_id(1)
    n_blk = u_ref.shape[0] // N_KEYS

    @pl.when(c == 0)
    def _():
        x = x_ref[...]
        h = _rmsnorm(x, gffn_ref[...]) * (1.0 + mod_ref[1]) + mod_ref[0]
        ht = h.T.astype(BF16)
        h2_ref[...] = ht
        qh = wqT_ref.shape[0] // 2
        q_ref[0:qh, :] = _dot(wqT_ref[0:qh, :], ht)
        q_ref[qh:, :] = _dot(wqT_ref[qh:, :], ht)
        outT_ref[...] = jnp.zeros_like(outT_ref)

        def one_head(hh, top_ref):
            scores = []
            for s in range(2):
                r0 = pl.multiple_of((2 * hh + s) * N_KEYS, N_KEYS)
                qs = q_ref[pl.ds(r0, N_KEYS), :].astype(BF16)
                sc = _dot(keys_ref[2 * hh + s], qs) * LOG2E
                if s == 0:
                    s1_ref[hh] = sc
                else:
                    s2_ref[hh] = sc
                scores.append(sc)
            n_tok = x_ref.shape[0]
            strip = min(n_tok, PEER_SELECT_STRIP)
            for c0 in range(0, n_tok, strip):
                cols = slice(c0, c0 + strip)
                for s in range(2):
                    srt = _sort_blocks_desc([scores[s][j * SUBLANES:(j + 1) * SUBLANES, cols]
                                             for j in range(N_KEYS // SUBLANES)])
                    for k, m in enumerate(_pop_largest(srt, TOPK + 1)):
                        top_ref[s, k:k + 1, cols] = m
                a = top_ref[0, 0:TOPK + 1, cols]
                b = top_ref[1, 0:TOPK + 1, cols]
                best = a[0:1] + b[0:1]
                pops = _pop_largest(_sort_blocks_desc(_cand_blocks(a, b)), TOPK + 1)
                zsum = None
                for m in pops[:TOPK]:
                    e = jnp.exp2(m - best)
                    zsum = e if zsum is None else zsum + e
                thr = jnp.where(pops[TOPK] == NEG_INF, pops[TOPK - 1], 0.5 * (pops[TOPK - 1] + pops[TOPK]))
                s1 = s1_ref[hh, :, cols]
                s1_ref[hh, :, cols] = thr - s1
                e1_ref[hh, :, cols] = 0.5 * jnp.exp2(s1 - (a[0:1] + jnp.log2(zsum)))
                e2_ref[hh, :, cols] = jnp.exp2(scores[1][:, cols] - b[0:1])

        def head_body(hp, carry):
            for e in range(PEER_HEADS_PER_TRIP):
                one_head(PEER_HEADS_PER_TRIP * hp + e, tops_ref.at[e])
            return carry

        lax.fori_loop(0, R_HEADS // PEER_HEADS_PER_TRIP, head_body, 0)

    eh = u_ref.shape[0] // 2
    dh = vt_ref.shape[0] // 2

    @pl.when(c < n_chunks)
    def _():
        act_w = act_ref.at[c % 2]
        act_w[0:eh, :] = _dot(u_ref[0:eh, :], h2_ref[...])
        act_w[eh:, :] = _dot(u_ref[eh:, :], h2_ref[...])

    @pl.when(c >= 1)
    def _():
        act_r = act_ref.at[(c - 1) % 2]
        i1 = pl.multiple_of((c - 1) * n_blk, n_blk)
        u_rows = [s1_ref[hh, pl.ds(i1, n_blk), :] for hh in range(R_HEADS)]
        e1_rows = [e1_ref[hh, pl.ds(i1, n_blk), :] for hh in range(R_HEADS)]
        for r in range(n_blk):
            acc = None
            for hh in range(R_HEADS):
                gate = jnp.where(s2_ref[hh] >= u_rows[hh][r:r + 1, :], e2_ref[hh] * e1_rows[hh][r:r + 1, :], 0.0)
                acc = gate if acc is None else acc + gate
            blk = slice(r * N_KEYS, (r + 1) * N_KEYS)
            x = act_r[blk, :]
            coef_ref[blk, :] = (acc * (x * (1.0 + lax.erf(x * (1.0 / math.sqrt(2.0)))))).astype(BF16)
        outT_ref[0:dh, :] += _dot(vt_ref[0:dh, :], coef_ref[...])
        outT_ref[dh:, :] += _dot(vt_ref[dh:, :], coef_ref[...])

    @pl.when(c == n_chunks)
    def _():
        x2 = x_ref[...] + mod_ref[2] * outT_ref[...].T
        y_ref[...] = _rmsnorm(x2, gfin_ref[...]) * (1.0 + mod_ref[4]) + mod_ref[3]


def _peer_call(x, mod5, tiles_per_row, p, tl, ec=PEER_EXPERTS):
    t, d = x.shape
    n_exp = p["expert_u"].shape[0]
    r = mod5.shape[2]
    scratch = [
        pltpu.VMEM((d, tl), BF16),
        pltpu.VMEM((R_HEADS * 2 * N_KEYS, tl), F32),
        pltpu.VMEM((R_HEADS, N_KEYS, tl), F32),
        pltpu.VMEM((R_HEADS, N_KEYS, tl), F32),
        pltpu.VMEM((R_HEADS, N_KEYS, tl), F32),
        pltpu.VMEM((R_HEADS, N_KEYS, tl), F32),
        pltpu.VMEM((PEER_HEADS_PER_TRIP, 2, TOPK + SUBLANES, tl), F32),
        pltpu.VMEM((2, ec, tl), F32),
        pltpu.VMEM((ec, tl), BF16),
        pltpu.VMEM((d, tl), F32),
    ]
    n_chunks = n_exp // ec
    return pl.pallas_call(
        functools.partial(_peer_kernel, n_chunks=n_chunks),
        grid=(t // tl, n_chunks + 1),
        in_specs=[
            pl.BlockSpec((tl, d), lambda i, c: (i, 0)),
            pl.BlockSpec((None, 5, r, d), lambda i, c: (i // tiles_per_row, 0, 0, 0)),
            _full((1, d)), _full((1, d)),
            _full(p["w_qT"].shape), _full(p["keys"].shape),
            pl.BlockSpec((ec, d), lambda i, c: (jnp.minimum(c, n_chunks - 1), 0)),
            pl.BlockSpec((d, ec), lambda i, c: (0, jnp.maximum(c - 1, 0))),
        ],
        out_specs=pl.BlockSpec((tl, d), lambda i, c: (i, 0)),
        out_shape=jax.ShapeDtypeStruct((t, d), F32),
        scratch_shapes=scratch,
        compiler_params=_params(("parallel", "arbitrary")),
        name="peer",
    )(x, mod5, p["g_ffn"], p["g_final"], p["w_qT"], p["keys"], p["expert_u"], p["expert_vT"])


def _prep(w_in, g_mix, g_v, w_s, b_s, conv_w, conv_b, dt_bias, a_log, d_skip, g_ssm, w_out, g_ffn, w_q,
          sub_keys, expert_u, expert_v, g_final):
    d = w_in.shape[0]
    o_z = 2 * 1024
    o_x = o_z + 1024
    o_dt = o_x + 1536
    w_dt = jnp.zeros((d, LANES), F32).at[:, :B_HEADS].set(w_in[:, o_dt:o_dt + B_HEADS])
    pad16 = lambda v: jnp.zeros((LANES,), F32).at[:B_HEADS].set(v)
    heads = jnp.arange(LANES)[:, None]
    p = {
        "g_mix": g_mix.reshape(1, d),
        "w_uv": w_in[:, 0:o_z].astype(BF16),
        "w_z": w_in[:, o_z:o_x].astype(BF16),
        "w_xbc": w_in[:, o_x:o_dt].astype(BF16),
        "w_dt": w_dt.astype(BF16),
        "w_dtT": w_dt.T.astype(BF16),
        "dt_bias_row": pad16(dt_bias).reshape(1, LANES),
        "dt_bias_col": pad16(dt_bias).reshape(LANES, 1),
        "a_log_row": pad16(a_log).reshape(1, LANES),
        "a_log_sq": jnp.broadcast_to(pad16(a_log).reshape(LANES, 1), (LANES, LANES)),
        "expand64": (heads == jnp.arange(B_HEADS * B_HEAD_DIM)[None, :] // B_HEAD_DIM).astype(BF16),
        "g_v": g_v.reshape(1, -1),
        "w_s": w_s,
        "b_s_rep": jnp.broadcast_to(b_s[:, :, None], (A_HEADS, CHUNK, CHUNK)),
        "w_s00": jnp.repeat(w_s[:, 0, 0], A_HEAD_DIM).reshape(1, -1),
        "b_s0": jnp.repeat(b_s[:, 0], A_HEAD_DIM).reshape(1, -1),
        "conv_w": conv_w,
        "conv_b": conv_b.reshape(1, -1),
        "d_skip_exp": jnp.repeat(d_skip, B_HEAD_DIM).reshape(1, -1),
        "g_ssm": g_ssm.reshape(1, -1),
        "w_out": w_out.astype(BF16),
        "g_ffn": g_ffn.reshape(1, d),
        "g_final": g_final.reshape(1, d),
        "w_qT": w_q.T.astype(BF16),
        "keys": sub_keys.reshape(R_HEADS * 2, N_KEYS, -1).astype(BF16),
        "expert_u": expert_u.astype(BF16),
        "expert_vT": expert_v.T.astype(BF16),
    }
    return p


def kernel(x_prompt, x_sample, c_prompt, c_sample, state_ssm, state_conv, w_ada, b_ada, g_mix, w_in, g_v, w_s, b_s,
           conv_w, conv_b, dt_bias, a_log, d_skip, g_ssm, w_out, g_ffn, w_q, sub_keys, expert_u, expert_v,
           w_ada_f, b_ada_f, g_final):
    assert w_ada.shape[0] == 1, "single-layer trunk"
    bp, seq, d = x_prompt.shape
    ns = x_sample.shape[0]
    p = _prep(w_in[0], g_mix[0], g_v[0], w_s[0], b_s[0], conv_w[0], conv_b[0], dt_bias[0], a_log[0], d_skip[0],
              g_ssm[0], w_out[0], g_ffn[0], w_q[0], sub_keys[0], expert_u[0], expert_v[0], g_final)

    c_all = jnp.concatenate([c_prompt, c_sample], axis=0)
    mod = _ada_call(c_all, w_ada[0], b_ada[0]).reshape(bp + ns, N_MOD, d)
    modf = _ada_call(c_all, w_ada_f, b_ada_f).reshape(bp + ns, 2, d)
    mod5 = jnp.concatenate([mod[:, 3:6], modf], axis=1)

    x1_p, ssm_p, ctail_p = _mix_call(x_prompt, mod[:bp, 0:3].reshape(bp, 3, 1, d), p)
    y_p = _peer_call(x1_p.reshape(bp * seq, d), mod5[:bp].reshape(bp, 5, 1, d), seq // PEER_TOKENS, p, PEER_TOKENS)

    xs_in = x_sample.reshape(ns, d)
    mod1_s = jnp.transpose(mod[bp:, 0:3], (1, 0, 2))
    sconv = state_conv[0].reshape(ns, -1)
    v_s, cnew_s, ya_s, z_s, xc_s, bc_s, xdtT_s, dec_s = _s1_call(xs_in, mod1_s, sconv, p)
    ssm_s, yssd_s = _s2_call(dec_s[:, :B_HEADS], state_ssm[0], xdtT_s, bc_s)
    x1_s = _s3_call(xs_in, mod1_s, yssd_s, xc_s, z_s, ya_s, p)
    mod5_s = jnp.transpose(mod5[bp:], (1, 0, 2)).reshape(1, 5, ns, d)
    y_s = _peer_call(x1_s, mod5_s, 1, p, ns)

    return (
        y_p.reshape(bp, seq, d),
        y_s.reshape(ns, 1, d),
        ssm_p.reshape(1, bp, B_HEADS, B_HEAD_DIM, D_STATE),
        ctail_p[:, SUBLANES - (CONV_W - 1):, :].reshape(1, bp, CONV_W - 1, -1),
        ssm_s.reshape(1, ns, B_HEADS, B_HEAD_DIM, D_STATE),
        cnew_s.reshape(1, ns, CONV_W - 1, -1),
        v_s.reshape(1, ns, 1, -1),
    )
```

```python
import functools
import math

import jax
import jax.numpy as jnp
from jax import lax
from jax.experimental import pallas as pl
from jax.experimental.pallas import tpu as pltpu

F32 = jnp.float32
BF16 = jnp.bfloat16
NEG_INF = float("-inf")

EPS = 1e-6
A_HEADS = 8
A_HEAD_DIM = 128
CHUNK = 128
B_HEADS = 16
B_HEAD_DIM = 64
B_GROUPS = 2
D_STATE = 128
CONV_W = 4
N_KEYS = 128
R_HEADS = 8
TOPK = 16
N_MOD = 6
A_WIDTH = A_HEADS * A_HEAD_DIM
B_WIDTH = B_HEADS * B_HEAD_DIM
BC_WIDTH = B_GROUPS * D_STATE
CONV_DIM = B_WIDTH + 2 * BC_WIDTH
MIX_WIDTH = A_WIDTH + B_WIDTH

LANES = 128
SUBLANES = 8
VMEM_LIMIT = 56 * 1024 * 1024

MIX_ROWS = 512
PEER_TOKENS = 256
PEER_EXPERTS = 2048
PEER_SUBTILES = 2
PEER_SELECT_STRIP = 256
PEER_HEADS_PER_TRIP = 4
LOG2E = 1.4426950408889634
S2_TOKENS = 8


def _dot(a, b):
    return jnp.dot(a, b, preferred_element_type=F32)


def _dot_nt(a, b):
    return lax.dot_general(a, b, (((1,), (1,)), ((), ())), preferred_element_type=F32)


def _dot_tn(a, b):
    return lax.dot_general(a, b, (((0,), (0,)), ((), ())), preferred_element_type=F32)


def _split2(x):
    hi = x.astype(BF16)
    lo = (x - hi.astype(F32)).astype(BF16)
    return hi, lo


def _split3(x):
    hi = x.astype(BF16)
    r = x - hi.astype(F32)
    mid = r.astype(BF16)
    lo = (r - mid.astype(F32)).astype(BF16)
    return hi, mid, lo


def _dot_x3(a, b):
    a1, a2 = _split2(a)
    b1, b2 = _split2(b)
    return _dot(a1, b1) + (_dot(a1, b2) + _dot(a2, b1))


def _silu(x):
    return x / (1.0 + jnp.exp(-x))


def _gelu(x):
    return 0.5 * x * (1.0 + lax.erf(x * (1.0 / math.sqrt(2.0))))


def _softplus(x):
    return jnp.maximum(x, 0.0) + jnp.log1p(jnp.exp(-jnp.abs(x)))


def _rmsnorm(x, g):
    return x * lax.rsqrt(jnp.mean(x * x, axis=-1, keepdims=True) + EPS) * g


def _full(shape):
    nd = len(shape)
    return pl.BlockSpec(shape, lambda *_: (0,) * nd, pipeline_mode=pl.Buffered(1))


def _dot2(a, w_ref):
    h = w_ref.shape[1] // 2
    return jnp.concatenate([_dot(a, w_ref[:, :h]), _dot(a, w_ref[:, h:])], axis=1)


def _params(sem):
    return pltpu.CompilerParams(dimension_semantics=sem, vmem_limit_bytes=VMEM_LIMIT)


def _ada_kernel(c_ref, w_ref, b_ref, o_ref):
    o_ref[...] = _dot_x3(_silu(c_ref[...]), w_ref[...]) + b_ref[...]


def _ada_call(c, w, b, tn=512):
    m, k = c.shape
    n = w.shape[1]
    return pl.pallas_call(
        _ada_kernel,
        grid=(n // tn,),
        in_specs=[_full((m, k)), pl.BlockSpec((k, tn), lambda j: (0, j)), pl.BlockSpec((1, tn), lambda j: (0, j))],
        out_specs=pl.BlockSpec((m, tn), lambda j: (0, j)),
        out_shape=jax.ShapeDtypeStruct((m, n), F32),
        compiler_params=_params(("parallel",)),
        name="ada",
    )(c, w, b.reshape(1, n))


def _gate_mlp_chunk(u, v, gv, ws_ref, bs_ref, causal):
    ys, vs = [], []
    for g in range(A_HEADS):
        sl = slice(g * A_HEAD_DIM, (g + 1) * A_HEAD_DIM)
        ug = _gelu(u[:, sl])
        vn = _rmsnorm(_gelu(v[:, sl]), gv[:, sl])
        w = jnp.where(causal, ws_ref[g], 0.0).astype(BF16)
        s = _dot(w, vn.astype(BF16)) + bs_ref[g]
        ys.append((ug * s).astype(BF16))
        vs.append(vn)
    return ys, vs


def _mix_kernel(x_ref, mod_ref, gmix_ref, wuv_ref, wz_ref, wxbc_ref, wdt_ref, wdtT_ref,
                dtb_ref, dtbT_ref, alog_ref, alogT_ref, gv_ref, ws_ref, bs_ref,
                cw_ref, cb_ref, dsk_ref, gssm_ref, wout_ref,
                o_ref, ssm_ref, ctail_ref,
                hst_ref, tail_ref, uv_ref, z_ref, xs_ref, dt_ref, dtT_ref, ycat_ref):
    t = pl.program_id(1)
    tm = x_ref.shape[0]
    n_chunks = tm // CHUNK

    @pl.when(t == 0)
    def _():
        hst_ref[...] = jnp.zeros_like(hst_ref)
        tail_ref[...] = jnp.zeros_like(tail_ref)

    x = x_ref[...]
    h = _rmsnorm(x, gmix_ref[...]) * (1.0 + mod_ref[1]) + mod_ref[0]
    hb = h.astype(BF16)
    uv_ref[...] = _dot2(hb, wuv_ref)
    z_ref[...] = _dot2(hb, wz_ref)
    dt_ref[...] = _softplus(_dot(hb, wdt_ref[...]) + dtb_ref[...])
    dtT_ref[...] = _softplus(_dot_nt(wdtT_ref[...], hb) + dtbT_ref[...])

    cur = _dot2(hb, wxbc_ref)
    prev = tail_ref[...]
    row8 = lax.broadcasted_iota(jnp.int32, (SUBLANES, 1), 0)
    acc = cb_ref[...] + cur * cw_ref[CONV_W - 1:CONV_W, :]
    for k in range(1, CONV_W):
        r = pltpu.roll(cur, k, axis=0)
        head = jnp.where(row8 < k, pltpu.roll(prev, k, axis=0), r[0:SUBLANES])
        shifted = jnp.concatenate([head, r[SUBLANES:]], axis=0)
        acc = acc + shifted * cw_ref[CONV_W - 1 - k:CONV_W - k, :]
    xs_ref[...] = _silu(acc)
    new_tail = cur[tm - SUBLANES:tm]
    tail_ref[...] = new_tail
    ctail_ref[...] = new_tail

    ri = lax.broadcasted_iota(jnp.int32, (CHUNK, CHUNK), 0)
    ci = lax.broadcasted_iota(jnp.int32, (CHUNK, CHUNK), 1)
    causal = ri >= ci
    tril = jnp.where(causal, 1.0, 0.0).astype(BF16)
    triu = jnp.where(ri <= ci, 1.0, 0.0).astype(BF16)
    low_half = ri < B_HEAD_DIM
    low_lane = ci < B_HEAD_DIM
    a_col = -jnp.exp(alog_ref[...])
    a_row = -jnp.exp(alogT_ref[...])

    def chunk_body(c, carry):
        r0 = pl.multiple_of(c * CHUNK, CHUNK)
        rows = pl.ds(r0, CHUNK)

        ys, _ = _gate_mlp_chunk(uv_ref[rows, 0:A_WIDTH], uv_ref[rows, A_WIDTH:2 * A_WIDTH], gv_ref[...], ws_ref, bs_ref, causal)
        for g in range(A_HEADS):
            ycat_ref[rows, g * A_HEAD_DIM:(g + 1) * A_HEAD_DIM] = ys[g]

        dt = dt_ref[rows, :]
        d_a = dt * a_col
        h1, h2, h3 = _split3(d_a)
        acum = _dot(tril, h1) + (_dot(tril, h2) + _dot(tril, h3))
        dt_t = dtT_ref[:, rows]
        g1, g2, g3 = _split3(dt_t * a_row)
        acum_t = _dot(g1, triu) + (_dot(g2, triu) + _dot(g3, triu))

        xs = xs_ref[rows, 0:B_WIDTH]
        zz = z_ref[rows, :]
        y_pairs = []
        for g in range(B_GROUPS):
            bg = xs_ref[rows, B_WIDTH + g * D_STATE:B_WIDTH + (g + 1) * D_STATE]
            cg = xs_ref[rows, B_WIDTH + BC_WIDTH + g * D_STATE:B_WIDTH + BC_WIDTH + (g + 1) * D_STATE]
            cbm = _dot_nt(cg.astype(BF16), bg.astype(BF16))
            for pr in range(B_HEADS // B_GROUPS // 2):
                pair = g * 4 + pr
                xpair = xs[:, pair * 128:(pair + 1) * 128]
                hprev = hst_ref[pair]
                gds, xhs, css, hps, bscs, alasts = [], [], [], [], [], []
                for e in range(2):
                    hd = pair * 2 + e
                    ab = jnp.broadcast_to(acum[:, hd:hd + 1], (CHUNK, CHUNK))
                    seg = ab - acum_t[hd:hd + 1, :]
                    lm = jnp.exp(jnp.where(causal, seg, NEG_INF))
                    gds.append((cbm * lm * dt_t[hd:hd + 1, :]).astype(BF16))
                    sel_l = low_lane if e == 0 else jnp.logical_not(low_lane)
                    sel_r = low_half if e == 0 else jnp.logical_not(low_half)
                    xhs.append(jnp.where(sel_l, xpair, 0.0).astype(BF16))
                    hps.append(jnp.where(sel_r, hprev, 0.0).astype(BF16))
                    css.append((cg * jnp.exp(ab)).astype(BF16))
                    alast = ab[CHUNK - 1:CHUNK, :]
                    dt_col = jnp.broadcast_to(dt[:, hd:hd + 1], (CHUNK, CHUNK))
                    bscs.append((bg * (jnp.exp(alast - ab) * dt_col)).astype(BF16))
                    alasts.append(alast)
                x2 = jnp.concatenate(xhs, axis=0)
                y_pairs.append(_dot(jnp.concatenate(gds, axis=1), x2)
                               + _dot_nt(jnp.concatenate(css, axis=1), jnp.concatenate(hps, axis=1)))
                st = _dot_tn(x2, jnp.concatenate(bscs, axis=0))
                decay = jnp.exp(jnp.where(low_half, alasts[0], alasts[1]))
                hst_ref[pair] = hprev * decay + st
        y = jnp.concatenate(y_pairs, axis=1) + dsk_ref[...] * xs
        y = y * _silu(zz)
        half = B_HEADS * B_HEAD_DIM // B_GROUPS
        for g in range(B_GROUPS):
            sl = slice(g * half, (g + 1) * half)
            ycat_ref[rows, A_WIDTH + g * half:A_WIDTH + (g + 1) * half] = _rmsnorm(y[:, sl], gssm_ref[:, sl]).astype(BF16)
        return carry

    lax.fori_loop(0, n_chunks, chunk_body, 0)

    o_ref[...] = x + mod_ref[2] * _dot2(ycat_ref[...], wout_ref)

    @pl.when(t == pl.num_programs(1) - 1)
    def _():
        ssm_ref[...] = hst_ref[...]


def _mix_call(x, mod1, p, tm=MIX_ROWS):
    b, s, d = x.shape
    n_pairs = B_HEADS // 2
    in_specs = [
        pl.BlockSpec((None, tm, d), lambda i, t: (i, t, 0)),
        pl.BlockSpec((None, 3, 1, d), lambda i, t: (i, 0, 0, 0)),
        _full((1, d)),
        _full(p["w_uv"].shape), _full(p["w_z"].shape), _full(p["w_xbc"].shape),
        _full(p["w_dt"].shape), _full(p["w_dtT"].shape),
        _full((1, LANES)), _full((LANES, 1)), _full((1, LANES)), _full((LANES, LANES)),
        _full((1, A_WIDTH)), _full((A_HEADS, CHUNK, CHUNK)), _full((A_HEADS, CHUNK, CHUNK)),
        _full((CONV_W, CONV_DIM)), _full((1, CONV_DIM)), _full((1, B_WIDTH)), _full((1, B_WIDTH)),
        _full(p["w_out"].shape),
    ]
    out_specs = [
        pl.BlockSpec((None, tm, d), lambda i, t: (i, t, 0)),
        pl.BlockSpec((None, n_pairs, 128, D_STATE), lambda i, t: (i, 0, 0, 0)),
        pl.BlockSpec((None, SUBLANES, CONV_DIM), lambda i, t: (i, 0, 0)),
    ]
    out_shape = [
        jax.ShapeDtypeStruct((b, s, d), F32),
        jax.ShapeDtypeStruct((b, n_pairs, 128, D_STATE), F32),
        jax.ShapeDtypeStruct((b, SUBLANES, CONV_DIM), F32),
    ]
    scratch = [
        pltpu.VMEM((n_pairs, 128, D_STATE), F32),
        pltpu.VMEM((SUBLANES, CONV_DIM), F32),
        pltpu.VMEM((tm, 2 * A_WIDTH), F32),
        pltpu.VMEM((tm, B_WIDTH), F32),
        pltpu.VMEM((tm, CONV_DIM), F32),
        pltpu.VMEM((tm, LANES), F32),
        pltpu.VMEM((LANES, tm), F32),
        pltpu.VMEM((tm, MIX_WIDTH), BF16),
    ]
    return pl.pallas_call(
        _mix_kernel,
        grid=(b, s // tm),
        in_specs=in_specs, out_specs=out_specs, out_shape=out_shape, scratch_shapes=scratch,
        compiler_params=_params(("parallel", "arbitrary")),
        name="mix_prompt",
    )(x, mod1, p["g_mix"], p["w_uv"], p["w_z"], p["w_xbc"], p["w_dt"], p["w_dtT"],
      p["dt_bias_row"], p["dt_bias_col"], p["a_log_row"], p["a_log_sq"],
      p["g_v"], p["w_s"], p["b_s_rep"], p["conv_w"], p["conv_b"], p["d_skip_exp"], p["g_ssm"], p["w_out"])


def _s1_kernel(x_ref, mod_ref, gmix_ref, wuv_ref, wz_ref, wxbc_ref, wdt_ref, dtb_ref, alog_ref, exp64_ref,
               gv_ref, ws0_ref, bs0_ref, cw_ref, cb_ref, sconv_ref,
               v_ref, cnew_ref, ya_ref, z_ref, xs_ref, bc_ref, xdtT_ref, dec_ref):
    x = x_ref[...]
    h = _rmsnorm(x, gmix_ref[...]) * (1.0 + mod_ref[1]) + mod_ref[0]
    hb = h.astype(BF16)
    uv = _dot(hb, wuv_ref[...])
    gv = gv_ref[...]
    for g in range(A_HEADS):
        sl = slice(g * A_HEAD_DIM, (g + 1) * A_HEAD_DIM)
        ug = _gelu(uv[:, sl])
        vn = _rmsnorm(_gelu(uv[:, A_WIDTH + g * A_HEAD_DIM:A_WIDTH + (g + 1) * A_HEAD_DIM]), gv[:, sl])
        v_ref[:, sl] = vn
        s = ws0_ref[:, sl] * vn + bs0_ref[:, sl]
        ya_ref[:, sl] = (ug * s).astype(BF16)
    z_ref[...] = _dot(hb, wz_ref[...])
    raw = _dot(hb, wxbc_ref[...])
    cd = raw.shape[1]
    acc = cb_ref[...] + raw * cw_ref[CONV_W - 1:CONV_W, :]
    for k in range(CONV_W - 1):
        acc = acc + sconv_ref[:, k * cd:(k + 1) * cd] * cw_ref[k:k + 1, :]
    cnew_ref[:, 0:(CONV_W - 2) * cd] = sconv_ref[:, cd:(CONV_W - 1) * cd]
    cnew_ref[:, (CONV_W - 2) * cd:(CONV_W - 1) * cd] = raw
    xbc = _silu(acc)
    xs = xbc[:, 0:B_WIDTH]
    xs_ref[...] = xs
    bc_ref[...] = xbc[:, B_WIDTH:CONV_DIM]
    dt = _softplus(_dot(hb, wdt_ref[...]) + dtb_ref[...])
    dec_ref[...] = jnp.exp(dt * (-jnp.exp(alog_ref[...])))
    f1, f2, f3 = _split3(dt)
    dt64 = _dot(f1, exp64_ref[...]) + (_dot(f2, exp64_ref[...]) + _dot(f3, exp64_ref[...]))
    xdtT_ref[...] = (xs * dt64).T.astype(BF16)


def _s1_call(x, mod1, sconv, p):
    n, d = x.shape
    outs = [
        jax.ShapeDtypeStruct((n, A_WIDTH), F32),
        jax.ShapeDtypeStruct((n, (CONV_W - 1) * CONV_DIM), F32),
        jax.ShapeDtypeStruct((n, A_WIDTH), BF16),
        jax.ShapeDtypeStruct((n, B_WIDTH), F32),
        jax.ShapeDtypeStruct((n, B_WIDTH), F32),
        jax.ShapeDtypeStruct((n, 2 * BC_WIDTH), F32),
        jax.ShapeDtypeStruct((B_WIDTH, n), BF16),
        jax.ShapeDtypeStruct((n, LANES), F32),
    ]
    args = (x, mod1, p["g_mix"], p["w_uv"], p["w_z"], p["w_xbc"], p["w_dt"], p["dt_bias_row"], p["a_log_row"],
            p["expand64"], p["g_v"], p["w_s00"], p["b_s0"], p["conv_w"], p["conv_b"], sconv)
    return pl.pallas_call(
        _s1_kernel,
        grid=(1,),
        in_specs=[_full(a.shape) for a in args],
        out_specs=[_full(o.shape) for o in outs],
        out_shape=outs,
        compiler_params=_params(("arbitrary",)),
        name="mix_sample_in",
    )(*args)


def _s2_kernel(dec_ref, st_ref, xdtT_ref, bc_ref, cblk_ref, o_ref, y_ref):
    i = pl.program_id(0)
    bt = st_ref.shape[0]
    n = bc_ref.shape[0]
    half = B_HEADS * B_HEAD_DIM // B_GROUPS
    rowi = lax.broadcasted_iota(jnp.int32, (n, 1), 0)
    for bb in range(bt):
        b = i * bt + bb
        for g in range(B_GROUPS):
            bm = bc_ref[:, g * D_STATE:(g + 1) * D_STATE]
            rb = jnp.where(rowi == b, bm, 0.0).astype(BF16)
            outer = _dot(xdtT_ref[g * half:(g + 1) * half, :], rb)
            for k in range(B_HEADS // B_GROUPS):
                hd = g * (B_HEADS // B_GROUPS) + k
                sl = slice(k * B_HEAD_DIM, (k + 1) * B_HEAD_DIM)
                o_ref[bb, g, sl, :] = st_ref[bb, g, sl, :] * dec_ref[b, hd] + outer[sl, :]
            crow = cblk_ref[bb:bb + 1, BC_WIDTH + g * D_STATE:BC_WIDTH + (g + 1) * D_STATE]
            c8 = jnp.broadcast_to(crow, (SUBLANES, D_STATE)).astype(BF16)
            yr = _dot_nt(c8, o_ref[bb, g].astype(BF16))
            y_ref[bb:bb + 1, g * half:(g + 1) * half] = yr[0:1]


def _s2_call(dec, state, xdtT, bc, bt=S2_TOKENS):
    n = state.shape[0]
    half = B_HEADS * B_HEAD_DIM // B_GROUPS
    st = state.reshape(n, B_GROUPS, half, D_STATE)
    grid_spec = pltpu.PrefetchScalarGridSpec(
        num_scalar_prefetch=1,
        grid=(n // bt,),
        in_specs=[
            pl.BlockSpec((bt, B_GROUPS, half, D_STATE), lambda i, d: (i, 0, 0, 0)),
            pl.BlockSpec(xdtT.shape, lambda i, d: (0, 0)),
            pl.BlockSpec(bc.shape, lambda i, d: (0, 0)),
            pl.BlockSpec((bt, bc.shape[1]), lambda i, d: (i, 0)),
        ],
        out_specs=[
            pl.BlockSpec((bt, B_GROUPS, half, D_STATE), lambda i, d: (i, 0, 0, 0)),
            pl.BlockSpec((bt, B_WIDTH), lambda i, d: (i, 0)),
        ],
    )
    new_state, y = pl.pallas_call(
        _s2_kernel,
        grid_spec=grid_spec,
        out_shape=[jax.ShapeDtypeStruct(st.shape, F32), jax.ShapeDtypeStruct((n, B_WIDTH), F32)],
        compiler_params=_params(("arbitrary",)),
        name="mix_sample_state",
    )(dec, st, xdtT, bc, bc)
    return new_state.reshape(state.shape), y


def _s3_kernel(x_ref, mod_ref, y_ref, xs_ref, z_ref, ya_ref, dsk_ref, gssm_ref, wout_ref, o_ref):
    y = (y_ref[...] + dsk_ref[...] * xs_ref[...]) * _silu(z_ref[...])
    half = B_HEADS * B_HEAD_DIM // B_GROUPS
    parts = [ya_ref[...]]
    for g in range(B_GROUPS):
        sl = slice(g * half, (g + 1) * half)
        parts.append(_rmsnorm(y[:, sl], gssm_ref[:, sl]).astype(BF16))
    cat = jnp.concatenate(parts, axis=1)
    o_ref[...] = x_ref[...] + mod_ref[2] * _dot(cat, wout_ref[...])


def _s3_call(x, mod1, y, xs, z, ya, p):
    args = (x, mod1, y, xs, z, ya, p["d_skip_exp"], p["g_ssm"], p["w_out"])
    return pl.pallas_call(
        _s3_kernel,
        grid=(1,),
        in_specs=[_full(a.shape) for a in args],
        out_specs=_full(x.shape),
        out_shape=jax.ShapeDtypeStruct(x.shape, F32),
        compiler_params=_params(("arbitrary",)),
        name="mix_sample_out",
    )(*args)


def _cand_blocks(a, b):
    row8 = lax.broadcasted_iota(jnp.int32, (SUBLANES, 1), 0)
    a8 = a[0:SUBLANES]
    blocks = []
    for j in range(SUBLANES):
        blk = a8 + b[j:j + 1]
        cnt = min(SUBLANES, (TOPK + 1) // (j + 1))
        if cnt < SUBLANES:
            blk = jnp.where(row8 < cnt, blk, NEG_INF)
        blocks.append(blk)
    blocks.append(a[0:1] + b[SUBLANES:TOPK])
    blocks.append(a[SUBLANES:TOPK] + b[0:1])
    last = jnp.where(row8 == 0, a[0:1] + b[TOPK:TOPK + 1], jnp.where(row8 == 1, a[TOPK:TOPK + 1] + b[0:1], NEG_INF))
    blocks.append(last)
    return blocks


def _merge_exchange_pairs(n):
    pairs = []
    t = max(1, math.ceil(math.log2(n)))
    p = 1 << (t - 1)
    while p > 0:
        q, r, d = 1 << (t - 1), 0, p
        while d > 0:
            pairs.extend((i, i + d) for i in range(n - d) if (i & p) == r)
            d, q, r = q - p, q >> 1, p
        p >>= 1
    return pairs


def _sort_blocks_desc(blocks):
    blocks = list(blocks)
    for i, j in _merge_exchange_pairs(len(blocks)):
        hi = jnp.maximum(blocks[i], blocks[j])
        blocks[j] = jnp.minimum(blocks[i], blocks[j])
        blocks[i] = hi
    return blocks


def _pop_largest(blocks, k):
    blocks = list(blocks)
    nb = len(blocks)
    rows = []
    for t in range(k):
        m = jnp.max(blocks[0], axis=0, keepdims=True)
        rows.append(m)
        need = min(k - t - 1, nb)
        if need == 0:
            break
        sel = blocks[0] == m
        for j in range(need):
            nxt = blocks[j + 1] if j + 1 < nb else NEG_INF
            blocks[j] = jnp.where(sel, nxt, blocks[j])
    return rows


def _peer_tile(c, n_chunks, x_ref, mod_ref, gffn_ref, gfin_ref, wqT_ref, keys_ref, u_ref, vt_ref, y_ref,
               h2_ref, q_ref, s1_ref, s2_ref, e1_ref, e2_ref, tops_ref, act_ref, coef_ref, outT_ref):
    n_blk = u_ref.shape[0] // N_KEYS

    @pl.when(c == 0)
    def _():
        x = x_ref[...]
        h = _rmsnorm(x, gffn_ref[...]) * (1.0 + mod_ref[1]) + mod_ref[0]
        ht = h.T.astype(BF16)
        h2_ref[...] = ht
        qh = wqT_ref.shape[0] // 2
        q_ref[0:qh, :] = _dot(wqT_ref[0:qh, :], ht)
        q_ref[qh:, :] = _dot(wqT_ref[qh:, :], ht)
        outT_ref[...] = jnp.zeros_like(outT_ref)

        def one_head(hh, top_ref):
            scores = []
            for s in range(2):
                r0 = pl.multiple_of((2 * hh + s) * N_KEYS, N_KEYS)
                qs = q_ref[pl.ds(r0, N_KEYS), :].astype(BF16)
                sc = _dot(keys_ref[2 * hh + s], qs) * LOG2E
                if s == 0:
                    s1_ref[hh] = sc
                else:
                    s2_ref[hh] = sc
                scores.append(sc)
            n_tok = x_ref.shape[0]
            strip = min(n_tok, PEER_SELECT_STRIP)
            for c0 in range(0, n_tok, strip):
                cols = slice(c0, c0 + strip)
                for s in range(2):
                    srt = _sort_blocks_desc([scores[s][j * SUBLANES:(j + 1) * SUBLANES, cols]
                                             for j in range(N_KEYS // SUBLANES)])
                    for k, m in enumerate(_pop_largest(srt, TOPK + 1)):
                        top_ref[s, k:k + 1, cols] = m
                a = top_ref[0, 0:TOPK + 1, cols]
                b = top_ref[1, 0:TOPK + 1, cols]
                best = a[0:1] + b[0:1]
                pops = _pop_largest(_sort_blocks_desc(_cand_blocks(a, b)), TOPK + 1)
                zsum = None
                for m in pops[:TOPK]:
                    e = jnp.exp2(m - best)
                    zsum = e if zsum is None else zsum + e
                thr = jnp.where(pops[TOPK] == NEG_INF, pops[TOPK - 1], 0.5 * (pops[TOPK - 1] + pops[TOPK]))
                s1 = s1_ref[hh, :, cols]
                s1_ref[hh, :, cols] = thr - s1
                e1_ref[hh, :, cols] = 0.5 * jnp.exp2(s1 - (a[0:1] + jnp.log2(zsum)))
                e2_ref[hh, :, cols] = jnp.exp2(scores[1][:, cols] - b[0:1])

        def head_body(hp, carry):
            for e in range(PEER_HEADS_PER_TRIP):
                one_head(PEER_HEADS_PER_TRIP * hp + e, tops_ref.at[e])
            return carry

        lax.fori_loop(0, R_HEADS // PEER_HEADS_PER_TRIP, head_body, 0)

    eh = u_ref.shape[0] // 2
    dh = vt_ref.shape[0] // 2

    @pl.when(c < n_chunks)
    def _():
        act_w = act_ref.at[c % 2]
        act_w[0:eh, :] = _dot(u_ref[0:eh, :], h2_ref[...])
        act_w[eh:, :] = _dot(u_ref[eh:, :], h2_ref[...])

    @pl.when(c >= 1)
    def _():
        act_r = act_ref.at[(c - 1) % 2]
        i1 = pl.multiple_of((c - 1) * n_blk, n_blk)
        u_rows = [s1_ref[hh, pl.ds(i1, n_blk), :] for hh in range(R_HEADS)]
        e1_rows = [e1_ref[hh, pl.ds(i1, n_blk), :] for hh in range(R_HEADS)]
        for r in range(n_blk):
            acc = None
            for hh in range(R_HEADS):
                gate = jnp.where(s2_ref[hh] >= u_rows[hh][r:r + 1, :], e2_ref[hh] * e1_rows[hh][r:r + 1, :], 0.0)
                acc = gate if acc is None else acc + gate
            blk = slice(r * N_KEYS, (r + 1) * N_KEYS)
            x = act_r[blk, :]
            coef_ref[blk, :] = (acc * (x * (1.0 + lax.erf(x * (1.0 / math.sqrt(2.0)))))).astype(BF16)
        outT_ref[0:dh, :] += _dot(vt_ref[0:dh, :], coef_ref[...])
        outT_ref[dh:, :] += _dot(vt_ref[dh:, :], coef_ref[...])

    @pl.when(c == n_chunks)
    def _():
        x2 = x_ref[...] + mod_ref[2] * outT_ref[...].T
        y_ref[...] = _rmsnorm(x2, gfin_ref[...]) * (1.0 + mod_ref[4]) + mod_ref[3]


def _peer_kernel(x_ref, mod_ref, gffn_ref, gfin_ref, wqT_ref, keys_ref, u_ref, vt_ref, y_ref,
                 h2_ref, q_ref, s1_ref, s2_ref, e1_ref, e2_ref, tops_ref, act_ref, coef_ref, outT_ref,
                 *, n_chunks, n_sub):
    c = pl.program_id(1)
    tl = x_ref.shape[0] // n_sub

    def sub_tile(j, carry):
        rows = pl.ds(pl.multiple_of(j * tl, tl), tl)
        _peer_tile(c, n_chunks, x_ref.at[rows], mod_ref, gffn_ref, gfin_ref, wqT_ref, keys_ref, u_ref, vt_ref,
                   y_ref.at[rows], h2_ref.at[j], q_ref, s1_ref.at[j], s2_ref.at[j], e1_ref.at[j], e2_ref.at[j],
                   tops_ref, act_ref.at[j], coef_ref, outT_ref.at[j])
        return carry

    lax.fori_loop(0, n_sub, sub_tile, 0)


def _peer_call(x, mod5, tiles_per_row, p, tl, n_sub, ec=PEER_EXPERTS):
    t, d = x.shape
    n_exp = p["expert_u"].shape[0]
    r = mod5.shape[2]
    scratch = [
        pltpu.VMEM((n_sub, d, tl), BF16),
        pltpu.VMEM((R_HEADS * 2 * N_KEYS, tl), F32),
        pltpu.VMEM((n_sub, R_HEADS, N_KEYS, tl), F32),
        pltpu.VMEM((n_sub, R_HEADS, N_KEYS, tl), F32),
        pltpu.VMEM((n_sub, R_HEADS, N_KEYS, tl), F32),
        pltpu.VMEM((n_sub, R_HEADS, N_KEYS, tl), F32),
        pltpu.VMEM((PEER_HEADS_PER_TRIP, 2, TOPK + SUBLANES, tl), F32),
        pltpu.VMEM((n_sub, 2, ec, tl), F32),
        pltpu.VMEM((ec, tl), BF16),
        pltpu.VMEM((n_sub, d, tl), F32),
    ]
    n_chunks = n_exp // ec
    return pl.pallas_call(
        functools.partial(_peer_kernel, n_chunks=n_chunks, n_sub=n_sub),
        grid=(t // (n_sub * tl), n_chunks + 1),
        in_specs=[
            pl.BlockSpec((n_sub * tl, d), lambda i, c: (i, 0)),
            pl.BlockSpec((None, 5, r, d), lambda i, c: (i // tiles_per_row, 0, 0, 0)),
            _full((1, d)), _full((1, d)),
            _full(p["w_qT"].shape), _full(p["keys"].shape),
            pl.BlockSpec((ec, d), lambda i, c: (jnp.minimum(c, n_chunks - 1), 0)),
            pl.BlockSpec((d, ec), lambda i, c: (0, jnp.maximum(c - 1, 0))),
        ],
        out_specs=pl.BlockSpec((n_sub * tl, d), lambda i, c: (i, 0)),
        out_shape=jax.ShapeDtypeStruct((t, d), F32),
        scratch_shapes=scratch,
        compiler_params=_params(("parallel", "arbitrary")),
        name="peer",
    )(x, mod5, p["g_ffn"], p["g_final"], p["w_qT"], p["keys"], p["expert_u"], p["expert_vT"])


def _prep(w_in, g_mix, g_v, w_s, b_s, conv_w, conv_b, dt_bias, a_log, d_skip, g_ssm, w_out, g_ffn, w_q,
          sub_keys, expert_u, expert_v, g_final):
    d = w_in.shape[0]
    o_z = 2 * A_WIDTH
    o_x = o_z + B_WIDTH
    o_dt = o_x + CONV_DIM
    w_dt = jnp.zeros((d, LANES), F32).at[:, :B_HEADS].set(w_in[:, o_dt:o_dt + B_HEADS])
    pad16 = lambda v: jnp.zeros((LANES,), F32).at[:B_HEADS].set(v)
    heads = jnp.arange(LANES)[:, None]
    p = {
        "g_mix": g_mix.reshape(1, d),
        "w_uv": w_in[:, 0:o_z].astype(BF16),
        "w_z": w_in[:, o_z:o_x].astype(BF16),
        "w_xbc": w_in[:, o_x:o_dt].astype(BF16),
        "w_dt": w_dt.astype(BF16),
        "w_dtT": w_dt.T.astype(BF16),
        "dt_bias_row": pad16(dt_bias).reshape(1, LANES),
        "dt_bias_col": pad16(dt_bias).reshape(LANES, 1),
        "a_log_row": pad16(a_log).reshape(1, LANES),
        "a_log_sq": jnp.broadcast_to(pad16(a_log).reshape(LANES, 1), (LANES, LANES)),
        "expand64": (heads == jnp.arange(B_HEADS * B_HEAD_DIM)[None, :] // B_HEAD_DIM).astype(BF16),
        "g_v": g_v.reshape(1, -1),
        "w_s": w_s,
        "b_s_rep": jnp.broadcast_to(b_s[:, :, None], (A_HEADS, CHUNK, CHUNK)),
        "w_s00": jnp.repeat(w_s[:, 0, 0], A_HEAD_DIM).reshape(1, -1),
        "b_s0": jnp.repeat(b_s[:, 0], A_HEAD_DIM).reshape(1, -1),
        "conv_w": conv_w,
        "conv_b": conv_b.reshape(1, -1),
        "d_skip_exp": jnp.repeat(d_skip, B_HEAD_DIM).reshape(1, -1),
        "g_ssm": g_ssm.reshape(1, -1),
        "w_out": w_out.astype(BF16),
        "g_ffn": g_ffn.reshape(1, d),
        "g_final": g_final.reshape(1, d),
        "w_qT": w_q.T.astype(BF16),
        "keys": sub_keys.reshape(R_HEADS * 2, N_KEYS, -1).astype(BF16),
        "expert_u": expert_u.astype(BF16),
        "expert_vT": expert_v.T.astype(BF16),
    }
    return p


def kernel(x_prompt, x_sample, c_prompt, c_sample, state_ssm, state_conv, w_ada, b_ada, g_mix, w_in, g_v, w_s, b_s,
           conv_w, conv_b, dt_bias, a_log, d_skip, g_ssm, w_out, g_ffn, w_q, sub_keys, expert_u, expert_v,
           w_ada_f, b_ada_f, g_final):
    assert w_ada.shape[0] == 1, "single-layer trunk"
    bp, seq, d = x_prompt.shape
    ns = x_sample.shape[0]
    p = _prep(w_in[0], g_mix[0], g_v[0], w_s[0], b_s[0], conv_w[0], conv_b[0], dt_bias[0], a_log[0], d_skip[0],
              g_ssm[0], w_out[0], g_ffn[0], w_q[0], sub_keys[0], expert_u[0], expert_v[0], g_final)

    c_all = jnp.concatenate([c_prompt, c_sample], axis=0)
    mod = _ada_call(c_all, w_ada[0], b_ada[0]).reshape(bp + ns, N_MOD, d)
    modf = _ada_call(c_all, w_ada_f, b_ada_f).reshape(bp + ns, 2, d)
    mod5 = jnp.concatenate([mod[:, 3:6], modf], axis=1)

    x1_p, ssm_p, ctail_p = _mix_call(x_prompt, mod[:bp, 0:3].reshape(bp, 3, 1, d), p)
    y_p = _peer_call(x1_p.reshape(bp * seq, d), mod5[:bp].reshape(bp, 5, 1, d), seq // (PEER_SUBTILES * PEER_TOKENS), p,
                     PEER_TOKENS, PEER_SUBTILES)

    xs_in = x_sample.reshape(ns, d)
    mod1_s = jnp.transpose(mod[bp:, 0:3], (1, 0, 2))
    sconv = state_conv[0].reshape(ns, -1)
    v_s, cnew_s, ya_s, z_s, xc_s, bc_s, xdtT_s, dec_s = _s1_call(xs_in, mod1_s, sconv, p)
    ssm_s, yssd_s = _s2_call(dec_s[:, :B_HEADS], state_ssm[0], xdtT_s, bc_s)
    x1_s = _s3_call(xs_in, mod1_s, yssd_s, xc_s, z_s, ya_s, p)
    mod5_s = jnp.transpose(mod5[bp:], (1, 0, 2)).reshape(1, 5, ns, d)
    y_s = _peer_call(x1_s, mod5_s, 1, p, ns, 1)

    return (
        y_p.reshape(bp, seq, d),
        y_s.reshape(ns, 1, d),
        ssm_p.reshape(1, bp, B_HEADS, B_HEAD_DIM, D_STATE),
        ctail_p[:, SUBLANES - (CONV_W - 1):, :].reshape(1, bp, CONV_W - 1, -1),
        ssm_s.reshape(1, ns, B_HEADS, B_HEAD_DIM, D_STATE),
        cnew_s.reshape(1, ns, CONV_W - 1, -1),
        v_s.reshape(1, ns, 1, -1),
    )
```

```python
import functools
import math

import jax
import jax.numpy as jnp
from jax import lax
from jax.experimental import pallas as pl
from jax.experimental.pallas import tpu as pltpu

F32 = jnp.float32
BF16 = jnp.bfloat16
NEG_INF = float("-inf")

EPS = 1e-6
A_HEADS = 8
A_HEAD_DIM = 128
CHUNK = 128
B_HEADS = 16
B_HEAD_DIM = 64
B_GROUPS = 2
D_STATE = 128
CONV_W = 4
N_KEYS = 128
R_HEADS = 8
TOPK = 16
N_MOD = 6
A_WIDTH = A_HEADS * A_HEAD_DIM
B_WIDTH = B_HEADS * B_HEAD_DIM
BC_WIDTH = B_GROUPS * D_STATE
CONV_DIM = B_WIDTH + 2 * BC_WIDTH
MIX_WIDTH = A_WIDTH + B_WIDTH

LANES = 128
SUBLANES = 8
VMEM_LIMIT = 56 * 1024 * 1024

MIX_ROWS = 512
PEER_TOKENS = 256
PEER_EXPERTS = 2048
PEER_SUBTILES = 2
PEER_SELECT_STRIP = 256
PEER_HEADS_PER_TRIP = 4
LOG2E = 1.4426950408889634
S2_TOKENS = 8


def _dot(a, b):
    return jnp.dot(a, b, preferred_element_type=F32)


def _dot_nt(a, b):
    return lax.dot_general(a, b, (((1,), (1,)), ((), ())), preferred_element_type=F32)


def _dot_tn(a, b):
    return lax.dot_general(a, b, (((0,), (0,)), ((), ())), preferred_element_type=F32)


def _split2(x):
    hi = x.astype(BF16)
    lo = (x - hi.astype(F32)).astype(BF16)
    return hi, lo


def _split3(x):
    hi = x.astype(BF16)
    r = x - hi.astype(F32)
    mid = r.astype(BF16)
    lo = (r - mid.astype(F32)).astype(BF16)
    return hi, mid, lo


def _dot_x3(a, b):
    a1, a2 = _split2(a)
    b1, b2 = _split2(b)
    return _dot(a1, b1) + (_dot(a1, b2) + _dot(a2, b1))


def _silu(x):
    return x / (1.0 + jnp.exp(-x))


def _gelu(x):
    return 0.5 * x * (1.0 + lax.erf(x * (1.0 / math.sqrt(2.0))))


def _softplus(x):
    return jnp.maximum(x, 0.0) + jnp.log1p(jnp.exp(-jnp.abs(x)))


def _rmsnorm(x, g):
    return x * lax.rsqrt(jnp.mean(x * x, axis=-1, keepdims=True) + EPS) * g


def _full(shape):
    nd = len(shape)
    return pl.BlockSpec(shape, lambda *_: (0,) * nd, pipeline_mode=pl.Buffered(1))


def _dot2(a, w_ref):
    h = w_ref.shape[1] // 2
    return jnp.concatenate([_dot(a, w_ref[:, :h]), _dot(a, w_ref[:, h:])], axis=1)


def _params(sem):
    return pltpu.CompilerParams(dimension_semantics=sem, vmem_limit_bytes=VMEM_LIMIT)


def _ada_kernel(c_ref, w_ref, b_ref, o_ref):
    o_ref[...] = _dot_x3(_silu(c_ref[...]), w_ref[...]) + b_ref[...]


def _ada_call(c, w, b, tn=512):
    m, k = c.shape
    n = w.shape[1]
    return pl.pallas_call(
        _ada_kernel,
        grid=(n // tn,),
        in_specs=[_full((m, k)), pl.BlockSpec((k, tn), lambda j: (0, j)), pl.BlockSpec((1, tn), lambda j: (0, j))],
        out_specs=pl.BlockSpec((m, tn), lambda j: (0, j)),
        out_shape=jax.ShapeDtypeStruct((m, n), F32),
        compiler_params=_params(("parallel",)),
        name="ada",
    )(c, w, b.reshape(1, n))


def _gate_mlp_chunk(u, v, gv, ws_ref, bs_ref, causal):
    ys, vs = [], []
    for g in range(A_HEADS):
        sl = slice(g * A_HEAD_DIM, (g + 1) * A_HEAD_DIM)
        ug = _gelu(u[:, sl])
        vn = _rmsnorm(_gelu(v[:, sl]), gv[:, sl])
        w = jnp.where(causal, ws_ref[g], 0.0).astype(BF16)
        s = _dot(w, vn.astype(BF16)) + bs_ref[g]
        ys.append((ug * s).astype(BF16))
        vs.append(vn)
    return ys, vs


def _mix_kernel(x_ref, mod_ref, gmix_ref, wuv_ref, wz_ref, wxbc_ref, wdt_ref, wdtT_ref,
                dtb_ref, dtbT_ref, alog_ref, alogT_ref, gv_ref, ws_ref, bs_ref,
                cw_ref, cb_ref, dsk_ref, gssm_ref, wout_ref,
                o_ref, ssm_ref, ctail_ref,
                hst_ref, tail_ref, uv_ref, z_ref, xs_ref, dt_ref, dtT_ref, ycat_ref):
    t = pl.program_id(1)
    tm = x_ref.shape[0]
    n_chunks = tm // CHUNK

    @pl.when(t == 0)
    def _():
        hst_ref[...] = jnp.zeros_like(hst_ref)
        tail_ref[...] = jnp.zeros_like(tail_ref)

    x = x_ref[...]
    h = _rmsnorm(x, gmix_ref[...]) * (1.0 + mod_ref[1]) + mod_ref[0]
    hb = h.astype(BF16)
    uv_ref[...] = _dot2(hb, wuv_ref)
    z_ref[...] = _dot2(hb, wz_ref)
    dt_ref[...] = _softplus(_dot(hb, wdt_ref[...]) + dtb_ref[...])
    dtT_ref[...] = _softplus(_dot_nt(wdtT_ref[...], hb) + dtbT_ref[...])

    cur = _dot2(hb, wxbc_ref)
    prev = tail_ref[...]
    row8 = lax.broadcasted_iota(jnp.int32, (SUBLANES, 1), 0)
    acc = cb_ref[...] + cur * cw_ref[CONV_W - 1:CONV_W, :]
    for k in range(1, CONV_W):
        r = pltpu.roll(cur, k, axis=0)
        head = jnp.where(row8 < k, pltpu.roll(prev, k, axis=0), r[0:SUBLANES])
        shifted = jnp.concatenate([head, r[SUBLANES:]], axis=0)
        acc = acc + shifted * cw_ref[CONV_W - 1 - k:CONV_W - k, :]
    xs_ref[...] = _silu(acc)
    new_tail = cur[tm - SUBLANES:tm]
    tail_ref[...] = new_tail
    ctail_ref[...] = new_tail

    ri = lax.broadcasted_iota(jnp.int32, (CHUNK, CHUNK), 0)
    ci = lax.broadcasted_iota(jnp.int32, (CHUNK, CHUNK), 1)
    causal = ri >= ci
    tril = jnp.where(causal, 1.0, 0.0).astype(BF16)
    triu = jnp.where(ri <= ci, 1.0, 0.0).astype(BF16)
    low_half = ri < B_HEAD_DIM
    low_lane = ci < B_HEAD_DIM
    a_col = -jnp.exp(alog_ref[...])
    a_row = -jnp.exp(alogT_ref[...])

    def chunk_body(c, carry):
        r0 = pl.multiple_of(c * CHUNK, CHUNK)
        rows = pl.ds(r0, CHUNK)

        ys, _ = _gate_mlp_chunk(uv_ref[rows, 0:A_WIDTH], uv_ref[rows, A_WIDTH:2 * A_WIDTH], gv_ref[...], ws_ref, bs_ref, causal)
        for g in range(A_HEADS):
            ycat_ref[rows, g * A_HEAD_DIM:(g + 1) * A_HEAD_DIM] = ys[g]

        dt = dt_ref[rows, :]
        d_a = dt * a_col
        h1, h2, h3 = _split3(d_a)
        acum = _dot(tril, h1) + (_dot(tril, h2) + _dot(tril, h3))
        dt_t = dtT_ref[:, rows]
        g1, g2, g3 = _split3(dt_t * a_row)
        acum_t = _dot(g1, triu) + (_dot(g2, triu) + _dot(g3, triu))

        xs = xs_ref[rows, 0:B_WIDTH]
        zz = z_ref[rows, :]
        y_pairs = []
        for g in range(B_GROUPS):
            bg = xs_ref[rows, B_WIDTH + g * D_STATE:B_WIDTH + (g + 1) * D_STATE]
            cg = xs_ref[rows, B_WIDTH + BC_WIDTH + g * D_STATE:B_WIDTH + BC_WIDTH + (g + 1) * D_STATE]
            cbm = _dot_nt(cg.astype(BF16), bg.astype(BF16))
            for pr in range(B_HEADS // B_GROUPS // 2):
                pair = g * 4 + pr
                xpair = xs[:, pair * 128:(pair + 1) * 128]
                hprev = hst_ref[pair]
                gds, xhs, css, hps, bscs, alasts = [], [], [], [], [], []
                for e in range(2):
                    hd = pair * 2 + e
                    ab = jnp.broadcast_to(acum[:, hd:hd + 1], (CHUNK, CHUNK))
                    seg = ab - acum_t[hd:hd + 1, :]
                    lm = jnp.exp(jnp.where(causal, seg, NEG_INF))
                    gds.append((cbm * lm * dt_t[hd:hd + 1, :]).astype(BF16))
                    sel_l = low_lane if e == 0 else jnp.logical_not(low_lane)
                    sel_r = low_half if e == 0 else jnp.logical_not(low_half)
                    xhs.append(jnp.where(sel_l, xpair, 0.0).astype(BF16))
                    hps.append(jnp.where(sel_r, hprev, 0.0).astype(BF16))
                    css.append((cg * jnp.exp(ab)).astype(BF16))
                    alast = ab[CHUNK - 1:CHUNK, :]
                    dt_col = jnp.broadcast_to(dt[:, hd:hd + 1], (CHUNK, CHUNK))
                    bscs.append((bg * (jnp.exp(alast - ab) * dt_col)).astype(BF16))
                    alasts.append(alast)
                x2 = jnp.concatenate(xhs, axis=0)
                y_pairs.append(_dot(jnp.concatenate(gds, axis=1), x2)
                               + _dot_nt(jnp.concatenate(css, axis=1), jnp.concatenate(hps, axis=1)))
                st = _dot_tn(x2, jnp.concatenate(bscs, axis=0))
                decay = jnp.exp(jnp.where(low_half, alasts[0], alasts[1]))
                hst_ref[pair] = hprev * decay + st
        y = jnp.concatenate(y_pairs, axis=1) + dsk_ref[...] * xs
        y = y * _silu(zz)
        half = B_HEADS * B_HEAD_DIM // B_GROUPS
        for g in range(B_GROUPS):
            sl = slice(g * half, (g + 1) * half)
            ycat_ref[rows, A_WIDTH + g * half:A_WIDTH + (g + 1) * half] = _rmsnorm(y[:, sl], gssm_ref[:, sl]).astype(BF16)
        return carry

    lax.fori_loop(0, n_chunks, chunk_body, 0)

    o_ref[...] = x + mod_ref[2] * _dot2(ycat_ref[...], wout_ref)

    @pl.when(t == pl.num_programs(1) - 1)
    def _():
        ssm_ref[...] = hst_ref[...]


def _mix_call(x, mod1, p, tm=MIX_ROWS):
    b, s, d = x.shape
    n_pairs = B_HEADS // 2
    in_specs = [
        pl.BlockSpec((None, tm, d), lambda i, t: (i, t, 0)),
        pl.BlockSpec((None, 3, 1, d), lambda i, t: (i, 0, 0, 0)),
        _full((1, d)),
        _full(p["w_uv"].shape), _full(p["w_z"].shape), _full(p["w_xbc"].shape),
        _full(p["w_dt"].shape), _full(p["w_dtT"].shape),
        _full((1, LANES)), _full((LANES, 1)), _full((1, LANES)), _full((LANES, LANES)),
        _full((1, A_WIDTH)), _full((A_HEADS, CHUNK, CHUNK)), _full((A_HEADS, CHUNK, CHUNK)),
        _full((CONV_W, CONV_DIM)), _full((1, CONV_DIM)), _full((1, B_WIDTH)), _full((1, B_WIDTH)),
        _full(p["w_out"].shape),
    ]
    out_specs = [
        pl.BlockSpec((None, tm, d), lambda i, t: (i, t, 0)),
        pl.BlockSpec((None, n_pairs, 128, D_STATE), lambda i, t: (i, 0, 0, 0)),
        pl.BlockSpec((None, SUBLANES, CONV_DIM), lambda i, t: (i, 0, 0)),
    ]
    out_shape = [
        jax.ShapeDtypeStruct((b, s, d), F32),
        jax.ShapeDtypeStruct((b, n_pairs, 128, D_STATE), F32),
        jax.ShapeDtypeStruct((b, SUBLANES, CONV_DIM), F32),
    ]
    scratch = [
        pltpu.VMEM((n_pairs, 128, D_STATE), F32),
        pltpu.VMEM((SUBLANES, CONV_DIM), F32),
        pltpu.VMEM((tm, 2 * A_WIDTH), F32),
        pltpu.VMEM((tm, B_WIDTH), F32),
        pltpu.VMEM((tm, CONV_DIM), F32),
        pltpu.VMEM((tm, LANES), F32),
        pltpu.VMEM((LANES, tm), F32),
        pltpu.VMEM((tm, MIX_WIDTH), BF16),
    ]
    return pl.pallas_call(
        _mix_kernel,
        grid=(b, s // tm),
        in_specs=in_specs, out_specs=out_specs, out_shape=out_shape, scratch_shapes=scratch,
        compiler_params=_params(("parallel", "arbitrary")),
        name="mix_prompt",
    )(x, mod1, p["g_mix"], p["w_uv"], p["w_z"], p["w_xbc"], p["w_dt"], p["w_dtT"],
      p["dt_bias_row"], p["dt_bias_col"], p["a_log_row"], p["a_log_sq"],
      p["g_v"], p["w_s"], p["b_s_rep"], p["conv_w"], p["conv_b"], p["d_skip_exp"], p["g_ssm"], p["w_out"])


def _s1_kernel(x_ref, mod_ref, gmix_ref, wuv_ref, wz_ref, wxbc_ref, wdt_ref, dtb_ref, alog_ref, exp64_ref,
               gv_ref, ws0_ref, bs0_ref, cw_ref, cb_ref, sconv_ref,
               v_ref, cnew_ref, ya_ref, z_ref, xs_ref, bc_ref, xdtT_ref, dec_ref):
    x = x_ref[...]
    h = _rmsnorm(x, gmix_ref[...]) * (1.0 + mod_ref[1]) + mod_ref[0]
    hb = h.astype(BF16)
    uv = _dot(hb, wuv_ref[...])
    gv = gv_ref[...]
    for g in range(A_HEADS):
        sl = slice(g * A_HEAD_DIM, (g + 1) * A_HEAD_DIM)
        ug = _gelu(uv[:, sl])
        vn = _rmsnorm(_gelu(uv[:, A_WIDTH + g * A_HEAD_DIM:A_WIDTH + (g + 1) * A_HEAD_DIM]), gv[:, sl])
        v_ref[:, sl] = vn
        s = ws0_ref[:, sl] * vn + bs0_ref[:, sl]
        ya_ref[:, sl] = (ug * s).astype(BF16)
    z_ref[...] = _dot(hb, wz_ref[...])
    raw = _dot(hb, wxbc_ref[...])
    cd = raw.shape[1]
    acc = cb_ref[...] + raw * cw_ref[CONV_W - 1:CONV_W, :]
    for k in range(CONV_W - 1):
        acc = acc + sconv_ref[:, k * cd:(k + 1) * cd] * cw_ref[k:k + 1, :]
    cnew_ref[:, 0:(CONV_W - 2) * cd] = sconv_ref[:, cd:(CONV_W - 1) * cd]
    cnew_ref[:, (CONV_W - 2) * cd:(CONV_W - 1) * cd] = raw
    xbc = _silu(acc)
    xs = xbc[:, 0:B_WIDTH]
    xs_ref[...] = xs
    bc_ref[...] = xbc[:, B_WIDTH:CONV_DIM]
    dt = _softplus(_dot(hb, wdt_ref[...]) + dtb_ref[...])
    dec_ref[...] = jnp.exp(dt * (-jnp.exp(alog_ref[...])))
    f1, f2, f3 = _split3(dt)
    dt64 = _dot(f1, exp64_ref[...]) + (_dot(f2, exp64_ref[...]) + _dot(f3, exp64_ref[...]))
    xdtT_ref[...] = (xs * dt64).T.astype(BF16)


def _s1_call(x, mod1, sconv, p):
    n, d = x.shape
    outs = [
        jax.ShapeDtypeStruct((n, A_WIDTH), F32),
        jax.ShapeDtypeStruct((n, (CONV_W - 1) * CONV_DIM), F32),
        jax.ShapeDtypeStruct((n, A_WIDTH), BF16),
        jax.ShapeDtypeStruct((n, B_WIDTH), F32),
        jax.ShapeDtypeStruct((n, B_WIDTH), F32),
        jax.ShapeDtypeStruct((n, 2 * BC_WIDTH), F32),
        jax.ShapeDtypeStruct((B_WIDTH, n), BF16),
        jax.ShapeDtypeStruct((n, LANES), F32),
    ]
    args = (x, mod1, p["g_mix"], p["w_uv"], p["w_z"], p["w_xbc"], p["w_dt"], p["dt_bias_row"], p["a_log_row"],
            p["expand64"], p["g_v"], p["w_s00"], p["b_s0"], p["conv_w"], p["conv_b"], sconv)
    return pl.pallas_call(
        _s1_kernel,
        grid=(1,),
        in_specs=[_full(a.shape) for a in args],
        out_specs=[_full(o.shape) for o in outs],
        out_shape=outs,
        compiler_params=_params(("arbitrary",)),
        name="mix_sample_in",
    )(*args)


def _s2_kernel(dec_ref, st_ref, xdtT_ref, bc_ref, cblk_ref, o_ref, y_ref):
    i = pl.program_id(0)
    bt = st_ref.shape[0]
    n = bc_ref.shape[0]
    half = B_HEADS * B_HEAD_DIM // B_GROUPS
    rowi = lax.broadcasted_iota(jnp.int32, (n, 1), 0)
    for bb in range(bt):
        b = i * bt + bb
        for g in range(B_GROUPS):
            bm = bc_ref[:, g * D_STATE:(g + 1) * D_STATE]
            rb = jnp.where(rowi == b, bm, 0.0).astype(BF16)
            outer = _dot(xdtT_ref[g * half:(g + 1) * half, :], rb)
            for k in range(B_HEADS // B_GROUPS):
                hd = g * (B_HEADS // B_GROUPS) + k
                sl = slice(k * B_HEAD_DIM, (k + 1) * B_HEAD_DIM)
                o_ref[bb, g, sl, :] = st_ref[bb, g, sl, :] * dec_ref[b, hd] + outer[sl, :]
            crow = cblk_ref[bb:bb + 1, BC_WIDTH + g * D_STATE:BC_WIDTH + (g + 1) * D_STATE]
            c8 = jnp.broadcast_to(crow, (SUBLANES, D_STATE)).astype(BF16)
            yr = _dot_nt(c8, o_ref[bb, g].astype(BF16))
            y_ref[bb:bb + 1, g * half:(g + 1) * half] = yr[0:1]


def _s2_call(dec, state, xdtT, bc, bt=S2_TOKENS):
    n = state.shape[0]
    half = B_HEADS * B_HEAD_DIM // B_GROUPS
    st = state.reshape(n, B_GROUPS, half, D_STATE)
    grid_spec = pltpu.PrefetchScalarGridSpec(
        num_scalar_prefetch=1,
        grid=(n // bt,),
        in_specs=[
            pl.BlockSpec((bt, B_GROUPS, half, D_STATE), lambda i, d: (i, 0, 0, 0)),
            pl.BlockSpec(xdtT.shape, lambda i, d: (0, 0)),
            pl.BlockSpec(bc.shape, lambda i, d: (0, 0)),
            pl.BlockSpec((bt, bc.shape[1]), lambda i, d: (i, 0)),
        ],
        out_specs=[
            pl.BlockSpec((bt, B_GROUPS, half, D_STATE), lambda i, d: (i, 0, 0, 0)),
            pl.BlockSpec((bt, B_WIDTH), lambda i, d: (i, 0)),
        ],
    )
    new_state, y = pl.pallas_call(
        _s2_kernel,
        grid_spec=grid_spec,
        out_shape=[jax.ShapeDtypeStruct(st.shape, F32), jax.ShapeDtypeStruct((n, B_WIDTH), F32)],
        compiler_params=_params(("arbitrary",)),
        name="mix_sample_state",
    )(dec, st, xdtT, bc, bc)
    return new_state.reshape(state.shape), y


def _s3_kernel(x_ref, mod_ref, y_ref, xs_ref, z_ref, ya_ref, dsk_ref, gssm_ref, wout_ref, o_ref):
    y = (y_ref[...] + dsk_ref[...] * xs_ref[...]) * _silu(z_ref[...])
    half = B_HEADS * B_HEAD_DIM // B_GROUPS
    parts = [ya_ref[...]]
    for g in range(B_GROUPS):
        sl = slice(g * half, (g + 1) * half)
        parts.append(_rmsnorm(y[:, sl], gssm_ref[:, sl]).astype(BF16))
    cat = jnp.concatenate(parts, axis=1)
    o_ref[...] = x_ref[...] + mod_ref[2] * _dot(cat, wout_ref[...])


def _s3_call(x, mod1, y, xs, z, ya, p):
    args = (x, mod1, y, xs, z, ya, p["d_skip_exp"], p["g_ssm"], p["w_out"])
    return pl.pallas_call(
        _s3_kernel,
        grid=(1,),
        in_specs=[_full(a.shape) for a in args],
        out_specs=_full(x.shape),
        out_shape=jax.ShapeDtypeStruct(x.shape, F32),
        compiler_params=_params(("arbitrary",)),
        name="mix_sample_out",
    )(*args)


def _cand_blocks(a, b):
    row8 = lax.broadcasted_iota(jnp.int32, (SUBLANES, 1), 0)
    a8 = a[0:SUBLANES]
    blocks = []
    for j in range(SUBLANES):
        blk = a8 + b[j:j + 1]
        cnt = min(SUBLANES, (TOPK + 1) // (j + 1))
        if cnt < SUBLANES:
            blk = jnp.where(row8 < cnt, blk, NEG_INF)
        blocks.append(blk)
    blocks.append(a[0:1] + b[SUBLANES:TOPK])
    blocks.append(a[SUBLANES:TOPK] + b[0:1])
    last = jnp.where(row8 == 0, a[0:1] + b[TOPK:TOPK + 1], jnp.where(row8 == 1, a[TOPK:TOPK + 1] + b[0:1], NEG_INF))
    blocks.append(last)
    return blocks


def _merge_exchange_pairs(n):
    pairs = []
    t = max(1, math.ceil(math.log2(n)))
    p = 1 << (t - 1)
    while p > 0:
        q, r, d = 1 << (t - 1), 0, p
        while d > 0:
            pairs.extend((i, i + d) for i in range(n - d) if (i & p) == r)
            d, q, r = q - p, q >> 1, p
        p >>= 1
    return pairs


def _sort_blocks_desc(blocks):
    blocks = list(blocks)
    for i, j in _merge_exchange_pairs(len(blocks)):
        hi = jnp.maximum(blocks[i], blocks[j])
        blocks[j] = jnp.minimum(blocks[i], blocks[j])
        blocks[i] = hi
    return blocks


def _pop_largest(blocks, k):
    blocks = list(blocks)
    nb = len(blocks)
    rows = []
    for t in range(k):
        m = jnp.max(blocks[0], axis=0, keepdims=True)
        rows.append(m)
        need = min(k - t - 1, nb)
        if need == 0:
            break
        sel = blocks[0] == m
        for j in range(need):
            nxt = blocks[j + 1] if j + 1 < nb else NEG_INF
            blocks[j] = jnp.where(sel, nxt, blocks[j])
    return rows


def _peer_tile(c, n_chunks, x_ref, mod_ref, gffn_ref, gfin_ref, wqT_ref, keys_ref, u_ref, vt_ref, y_ref,
               h2_ref, q_ref, s1_ref, s2_ref, e1_ref, e2_ref, tops_ref, act_ref, coef_ref, outT_ref):
    n_blk = u_ref.shape[0] // N_KEYS

    @pl.when(c == 0)
    def _():
        x = x_ref[...]
        h = _rmsnorm(x, gffn_ref[...]) * (1.0 + mod_ref[1]) + mod_ref[0]
        ht = h.T.astype(BF16)
        h2_ref[...] = ht
        qh = wqT_ref.shape[0] // 2
        q_ref[0:qh, :] = _dot(wqT_ref[0:qh, :], ht)
        q_ref[qh:, :] = _dot(wqT_ref[qh:, :], ht)
        outT_ref[...] = jnp.zeros_like(outT_ref)

        def one_head(hh, top_ref):
            scores = []
            for s in range(2):
                r0 = pl.multiple_of((2 * hh + s) * N_KEYS, N_KEYS)
                qs = q_ref[pl.ds(r0, N_KEYS), :].astype(BF16)
                sc = _dot(keys_ref[2 * hh + s], qs) * LOG2E
                if s == 0:
                    s1_ref[hh] = sc
                else:
                    s2_ref[hh] = sc
                scores.append(sc)
            n_tok = x_ref.shape[0]
            strip = min(n_tok, PEER_SELECT_STRIP)
            for c0 in range(0, n_tok, strip):
                cols = slice(c0, c0 + strip)
                for s in range(2):
                    srt = _sort_blocks_desc([scores[s][j * SUBLANES:(j + 1) * SUBLANES, cols]
                                             for j in range(N_KEYS // SUBLANES)])
                    for k, m in enumerate(_pop_largest(srt, TOPK + 1)):
                        top_ref[s, k:k + 1, cols] = m
                a = top_ref[0, 0:TOPK + 1, cols]
                b = top_ref[1, 0:TOPK + 1, cols]
                best = a[0:1] + b[0:1]
                pops = _pop_largest(_sort_blocks_desc(_cand_blocks(a, b)), TOPK + 1)
                zsum = None
                for m in pops[:TOPK]:
                    e = jnp.exp2(m - best)
                    zsum = e if zsum is None else zsum + e
                thr = jnp.where(pops[TOPK] == NEG_INF, pops[TOPK - 1], 0.5 * (pops[TOPK - 1] + pops[TOPK]))
                s1 = s1_ref[hh, :, cols]
                s1_ref[hh, :, cols] = thr - s1
                e1_ref[hh, :, cols] = 0.5 * jnp.exp2(s1 - (a[0:1] + jnp.log2(zsum)))
                e2_ref[hh, :, cols] = jnp.exp2(scores[1][:, cols] - b[0:1]).astype(BF16)

        def head_body(hp, carry):
            for e in range(PEER_HEADS_PER_TRIP):
                one_head(PEER_HEADS_PER_TRIP * hp + e, tops_ref.at[e])
            return carry

        lax.fori_loop(0, R_HEADS // PEER_HEADS_PER_TRIP, head_body, 0)

    eh = u_ref.shape[0] // 2
    dh = vt_ref.shape[0] // 2

    @pl.when(c < n_chunks)
    def _():
        act_w = act_ref.at[c % 2]
        act_w[0:eh, :] = _dot(u_ref[0:eh, :], h2_ref[...])
        act_w[eh:, :] = _dot(u_ref[eh:, :], h2_ref[...])

    @pl.when(c >= 1)
    def _():
        act_r = act_ref.at[(c - 1) % 2]
        i1 = pl.multiple_of((c - 1) * n_blk, n_blk)
        u_rows = [s1_ref[hh, pl.ds(i1, n_blk), :] for hh in range(R_HEADS)]
        e1_rows = [e1_ref[hh, pl.ds(i1, n_blk), :] for hh in range(R_HEADS)]
        for r in range(n_blk):
            acc = None
            for hh in range(R_HEADS):
                prod = e2_ref[hh] * e1_rows[hh][r:r + 1, :].astype(BF16)
                gate = jnp.where(s2_ref[hh] >= u_rows[hh][r:r + 1, :], prod, jnp.zeros_like(prod))
                acc = gate if acc is None else acc + gate
            blk = slice(r * N_KEYS, (r + 1) * N_KEYS)
            x = act_r[blk, :]
            coef_ref[blk, :] = acc * (x * (1.0 + lax.erf(x * (1.0 / math.sqrt(2.0))))).astype(BF16)
        outT_ref[0:dh, :] += _dot(vt_ref[0:dh, :], coef_ref[...])
        outT_ref[dh:, :] += _dot(vt_ref[dh:, :], coef_ref[...])

    @pl.when(c == n_chunks)
    def _():
        x2 = x_ref[...] + mod_ref[2] * outT_ref[...].T
        y_ref[...] = _rmsnorm(x2, gfin_ref[...]) * (1.0 + mod_ref[4]) + mod_ref[3]


def _peer_kernel(x_ref, mod_ref, gffn_ref, gfin_ref, wqT_ref, keys_ref, u_ref, vt_ref, y_ref,
                 h2_ref, q_ref, s1_ref, s2_ref, e1_ref, e2_ref, tops_ref, act_ref, coef_ref, outT_ref,
                 *, n_chunks, n_sub):
    c = pl.program_id(1)
    tl = x_ref.shape[0] // n_sub

    def sub_tile(j, carry):
        rows = pl.ds(pl.multiple_of(j * tl, tl), tl)
        _peer_tile(c, n_chunks, x_ref.at[rows], mod_ref, gffn_ref, gfin_ref, wqT_ref, keys_ref, u_ref, vt_ref,
                   y_ref.at[rows], h2_ref.at[j], q_ref, s1_ref.at[j], s2_ref.at[j], e1_ref.at[j], e2_ref.at[j],
                   tops_ref, act_ref.at[j], coef_ref, outT_ref.at[j])
        return carry

    lax.fori_loop(0, n_sub, sub_tile, 0)


def _peer_call(x, mod5, tiles_per_row, p, tl, n_sub, ec=PEER_EXPERTS):
    t, d = x.shape
    n_exp = p["expert_u"].shape[0]
    r = mod5.shape[2]
    scratch = [
        pltpu.VMEM((n_sub, d, tl), BF16),
        pltpu.VMEM((R_HEADS * 2 * N_KEYS, tl), F32),
        pltpu.VMEM((n_sub, R_HEADS, N_KEYS, tl), F32),
        pltpu.VMEM((n_sub, R_HEADS, N_KEYS, tl), F32),
        pltpu.VMEM((n_sub, R_HEADS, N_KEYS, tl), F32),
        pltpu.VMEM((n_sub, R_HEADS, N_KEYS, tl), BF16),
        pltpu.VMEM((PEER_HEADS_PER_TRIP, 2, TOPK + SUBLANES, tl), F32),
        pltpu.VMEM((n_sub, 2, ec, tl), F32),
        pltpu.VMEM((ec, tl), BF16),
        pltpu.VMEM((n_sub, d, tl), F32),
    ]
    n_chunks = n_exp // ec
    return pl.pallas_call(
        functools.partial(_peer_kernel, n_chunks=n_chunks, n_sub=n_sub),
        grid=(t // (n_sub * tl), n_chunks + 1),
        in_specs=[
            pl.BlockSpec((n_sub * tl, d), lambda i, c: (i, 0)),
            pl.BlockSpec((None, 5, r, d), lambda i, c: (i // tiles_per_row, 0, 0, 0)),
            _full((1, d)), _full((1, d)),
            _full(p["w_qT"].shape), _full(p["keys"].shape),
            pl.BlockSpec((ec, d), lambda i, c: (jnp.minimum(c, n_chunks - 1), 0)),
            pl.BlockSpec((d, ec), lambda i, c: (0, jnp.maximum(c - 1, 0))),
        ],
        out_specs=pl.BlockSpec((n_sub * tl, d), lambda i, c: (i, 0)),
        out_shape=jax.ShapeDtypeStruct((t, d), F32),
        scratch_shapes=scratch,
        compiler_params=_params(("parallel", "arbitrary")),
        name="peer",
    )(x, mod5, p["g_ffn"], p["g_final"], p["w_qT"], p["keys"], p["expert_u"], p["expert_vT"])


def _prep(w_in, g_mix, g_v, w_s, b_s, conv_w, conv_b, dt_bias, a_log, d_skip, g_ssm, w_out, g_ffn, w_q,
          sub_keys, expert_u, expert_v, g_final):
    d = w_in.shape[0]
    o_z = 2 * A_WIDTH
    o_x = o_z + B_WIDTH
    o_dt = o_x + CONV_DIM
    w_dt = jnp.zeros((d, LANES), F32).at[:, :B_HEADS].set(w_in[:, o_dt:o_dt + B_HEADS])
    pad16 = lambda v: jnp.zeros((LANES,), F32).at[:B_HEADS].set(v)
    heads = jnp.arange(LANES)[:, None]
    p = {
        "g_mix": g_mix.reshape(1, d),
        "w_uv": w_in[:, 0:o_z].astype(BF16),
        "w_z": w_in[:, o_z:o_x].astype(BF16),
        "w_xbc": w_in[:, o_x:o_dt].astype(BF16),
        "w_dt": w_dt.astype(BF16),
        "w_dtT": w_dt.T.astype(BF16),
        "dt_bias_row": pad16(dt_bias).reshape(1, LANES),
        "dt_bias_col": pad16(dt_bias).reshape(LANES, 1),
        "a_log_row": pad16(a_log).reshape(1, LANES),
        "a_log_sq": jnp.broadcast_to(pad16(a_log).reshape(LANES, 1), (LANES, LANES)),
        "expand64": (heads == jnp.arange(B_HEADS * B_HEAD_DIM)[None, :] // B_HEAD_DIM).astype(BF16),
        "g_v": g_v.reshape(1, -1),
        "w_s": w_s,
        "b_s_rep": jnp.broadcast_to(b_s[:, :, None], (A_HEADS, CHUNK, CHUNK)),
        "w_s00": jnp.repeat(w_s[:, 0, 0], A_HEAD_DIM).reshape(1, -1),
        "b_s0": jnp.repeat(b_s[:, 0], A_HEAD_DIM).reshape(1, -1),
        "conv_w": conv_w,
        "conv_b": conv_b.reshape(1, -1),
        "d_skip_exp": jnp.repeat(d_skip, B_HEAD_DIM).reshape(1, -1),
        "g_ssm": g_ssm.reshape(1, -1),
        "w_out": w_out.astype(BF16),
        "g_ffn": g_ffn.reshape(1, d),
        "g_final": g_final.reshape(1, d),
        "w_qT": w_q.T.astype(BF16),
        "keys": sub_keys.reshape(R_HEADS * 2, N_KEYS, -1).astype(BF16),
        "expert_u": expert_u.astype(BF16),
        "expert_vT": expert_v.T.astype(BF16),
    }
    return p


def kernel(x_prompt, x_sample, c_prompt, c_sample, state_ssm, state_conv, w_ada, b_ada, g_mix, w_in, g_v, w_s, b_s,
           conv_w, conv_b, dt_bias, a_log, d_skip, g_ssm, w_out, g_ffn, w_q, sub_keys, expert_u, expert_v,
           w_ada_f, b_ada_f, g_final):
    assert w_ada.shape[0] == 1, "single-layer trunk"
    bp, seq, d = x_prompt.shape
    ns = x_sample.shape[0]
    p = _prep(w_in[0], g_mix[0], g_v[0], w_s[0], b_s[0], conv_w[0], conv_b[0], dt_bias[0], a_log[0], d_skip[0],
              g_ssm[0], w_out[0], g_ffn[0], w_q[0], sub_keys[0], expert_u[0], expert_v[0], g_final)

    c_all = jnp.concatenate([c_prompt, c_sample], axis=0)
    mod = _ada_call(c_all, w_ada[0], b_ada[0]).reshape(bp + ns, N_MOD, d)
    modf = _ada_call(c_all, w_ada_f, b_ada_f).reshape(bp + ns, 2, d)
    mod5 = jnp.concatenate([mod[:, 3:6], modf], axis=1)

    x1_p, ssm_p, ctail_p = _mix_call(x_prompt, mod[:bp, 0:3].reshape(bp, 3, 1, d), p)
    y_p = _peer_call(x1_p.reshape(bp * seq, d), mod5[:bp].reshape(bp, 5, 1, d), seq // (PEER_SUBTILES * PEER_TOKENS), p,
                     PEER_TOKENS, PEER_SUBTILES)

    xs_in = x_sample.reshape(ns, d)
    mod1_s = jnp.transpose(mod[bp:, 0:3], (1, 0, 2))
    sconv = state_conv[0].reshape(ns, -1)
    v_s, cnew_s, ya_s, z_s, xc_s, bc_s, xdtT_s, dec_s = _s1_call(xs_in, mod1_s, sconv, p)
    ssm_s, yssd_s = _s2_call(dec_s[:, :B_HEADS], state_ssm[0], xdtT_s, bc_s)
    x1_s = _s3_call(xs_in, mod1_s, yssd_s, xc_s, z_s, ya_s, p)
    mod5_s = jnp.transpose(mod5[bp:], (1, 0, 2)).reshape(1, 5, ns, d)
    y_s = _peer_call(x1_s, mod5_s, 1, p, ns, 1)

    return (
        y_p.reshape(bp, seq, d),
        y_s.reshape(ns, 1, d),
        ssm_p.reshape(1, bp, B_HEADS, B_HEAD_DIM, D_STATE),
        ctail_p[:, SUBLANES - (CONV_W - 1):, :].reshape(1, bp, CONV_W - 1, -1),
        ssm_s.reshape(1, ns, B_HEADS, B_HEAD_DIM, D_STATE),
        cnew_s.reshape(1, ns, CONV_W - 1, -1),
        v_s.reshape(1, ns, 1, -1),
    )
```

```python
import functools
import math

import jax
import jax.numpy as jnp
from jax import lax
from jax.experimental import pallas as pl
from jax.experimental.pallas import tpu as pltpu

F32 = jnp.float32
BF16 = jnp.bfloat16
NEG_INF = float("-inf")

EPS = 1e-6
A_HEADS = 8
A_HEAD_DIM = 128
CHUNK = 128
B_HEADS = 16
B_HEAD_DIM = 64
B_GROUPS = 2
D_STATE = 128
CONV_W = 4
N_KEYS = 128
R_HEADS = 8
TOPK = 16
N_MOD = 6
A_WIDTH = A_HEADS * A_HEAD_DIM
B_WIDTH = B_HEADS * B_HEAD_DIM
BC_WIDTH = B_GROUPS * D_STATE
CONV_DIM = B_WIDTH + 2 * BC_WIDTH
MIX_WIDTH = A_WIDTH + B_WIDTH

LANES = 128
SUBLANES = 8
VMEM_LIMIT = 56 * 1024 * 1024

MIX_ROWS = 512
PEER_TOKENS = 256
PEER_EXPERTS = 2048
PEER_SUBTILES = 2
PEER_SELECT_STRIP = 256
PEER_HEADS_PER_TRIP = 8
LOG2E = 1.4426950408889634
S2_TOKENS = 8


def _dot(a, b):
    return jnp.dot(a, b, preferred_element_type=F32)


def _dot_nt(a, b):
    return lax.dot_general(a, b, (((1,), (1,)), ((), ())), preferred_element_type=F32)


def _dot_tn(a, b):
    return lax.dot_general(a, b, (((0,), (0,)), ((), ())), preferred_element_type=F32)


def _split2(x):
    hi = x.astype(BF16)
    lo = (x - hi.astype(F32)).astype(BF16)
    return hi, lo


def _split3(x):
    hi = x.astype(BF16)
    r = x - hi.astype(F32)
    mid = r.astype(BF16)
    lo = (r - mid.astype(F32)).astype(BF16)
    return hi, mid, lo


def _dot_x3(a, b):
    a1, a2 = _split2(a)
    b1, b2 = _split2(b)
    return _dot(a1, b1) + (_dot(a1, b2) + _dot(a2, b1))


def _silu(x):
    return x / (1.0 + jnp.exp(-x))


def _gelu(x):
    return 0.5 * x * (1.0 + lax.erf(x * (1.0 / math.sqrt(2.0))))


def _softplus(x):
    return jnp.maximum(x, 0.0) + jnp.log1p(jnp.exp(-jnp.abs(x)))


def _rmsnorm(x, g):
    return x * lax.rsqrt(jnp.mean(x * x, axis=-1, keepdims=True) + EPS) * g


def _full(shape):
    nd = len(shape)
    return pl.BlockSpec(shape, lambda *_: (0,) * nd, pipeline_mode=pl.Buffered(1))


def _dot2(a, w_ref):
    h = w_ref.shape[1] // 2
    return jnp.concatenate([_dot(a, w_ref[:, :h]), _dot(a, w_ref[:, h:])], axis=1)


def _params(sem):
    return pltpu.CompilerParams(dimension_semantics=sem, vmem_limit_bytes=VMEM_LIMIT)


def _ada_kernel(c_ref, w_ref, b_ref, o_ref):
    o_ref[...] = _dot_x3(_silu(c_ref[...]), w_ref[...]) + b_ref[...]


def _ada_call(c, w, b, tn=512):
    m, k = c.shape
    n = w.shape[1]
    return pl.pallas_call(
        _ada_kernel,
        grid=(n // tn,),
        in_specs=[_full((m, k)), pl.BlockSpec((k, tn), lambda j: (0, j)), pl.BlockSpec((1, tn), lambda j: (0, j))],
        out_specs=pl.BlockSpec((m, tn), lambda j: (0, j)),
        out_shape=jax.ShapeDtypeStruct((m, n), F32),
        compiler_params=_params(("parallel",)),
        name="ada",
    )(c, w, b.reshape(1, n))


def _gate_mlp_chunk(u, v, gv, ws_ref, bs_ref, causal):
    ys, vs = [], []
    for g in range(A_HEADS):
        sl = slice(g * A_HEAD_DIM, (g + 1) * A_HEAD_DIM)
        ug = _gelu(u[:, sl])
        vn = _rmsnorm(_gelu(v[:, sl]), gv[:, sl])
        w = jnp.where(causal, ws_ref[g], 0.0).astype(BF16)
        s = _dot(w, vn.astype(BF16)) + bs_ref[g]
        ys.append((ug * s).astype(BF16))
        vs.append(vn)
    return ys, vs


def _mix_kernel(x_ref, mod_ref, gmix_ref, wuv_ref, wz_ref, wxbc_ref, wdt_ref, wdtT_ref,
                dtb_ref, dtbT_ref, alog_ref, alogT_ref, gv_ref, ws_ref, bs_ref,
                cw_ref, cb_ref, dsk_ref, gssm_ref, wout_ref,
                o_ref, ssm_ref, ctail_ref,
                hst_ref, tail_ref, uv_ref, z_ref, xs_ref, dt_ref, dtT_ref, ycat_ref):
    t = pl.program_id(1)
    tm = x_ref.shape[0]
    n_chunks = tm // CHUNK

    @pl.when(t == 0)
    def _():
        hst_ref[...] = jnp.zeros_like(hst_ref)
        tail_ref[...] = jnp.zeros_like(tail_ref)

    x = x_ref[...]
    h = _rmsnorm(x, gmix_ref[...]) * (1.0 + mod_ref[1]) + mod_ref[0]
    hb = h.astype(BF16)
    uv_ref[...] = _dot2(hb, wuv_ref)
    z_ref[...] = _dot2(hb, wz_ref)
    dt_ref[...] = _softplus(_dot(hb, wdt_ref[...]) + dtb_ref[...])
    dtT_ref[...] = _softplus(_dot_nt(wdtT_ref[...], hb) + dtbT_ref[...])

    cur = _dot2(hb, wxbc_ref)
    prev = tail_ref[...]
    row8 = lax.broadcasted_iota(jnp.int32, (SUBLANES, 1), 0)
    acc = cb_ref[...] + cur * cw_ref[CONV_W - 1:CONV_W, :]
    for k in range(1, CONV_W):
        r = pltpu.roll(cur, k, axis=0)
        head = jnp.where(row8 < k, pltpu.roll(prev, k, axis=0), r[0:SUBLANES])
        shifted = jnp.concatenate([head, r[SUBLANES:]], axis=0)
        acc = acc + shifted * cw_ref[CONV_W - 1 - k:CONV_W - k, :]
    xs_ref[...] = _silu(acc)
    new_tail = cur[tm - SUBLANES:tm]
    tail_ref[...] = new_tail
    ctail_ref[...] = new_tail

    ri = lax.broadcasted_iota(jnp.int32, (CHUNK, CHUNK), 0)
    ci = lax.broadcasted_iota(jnp.int32, (CHUNK, CHUNK), 1)
    causal = ri >= ci
    tril = jnp.where(causal, 1.0, 0.0).astype(BF16)
    triu = jnp.where(ri <= ci, 1.0, 0.0).astype(BF16)
    low_half = ri < B_HEAD_DIM
    low_lane = ci < B_HEAD_DIM
    a_col = -jnp.exp(alog_ref[...])
    a_row = -jnp.exp(alogT_ref[...])

    def chunk_body(c, carry):
        r0 = pl.multiple_of(c * CHUNK, CHUNK)
        rows = pl.ds(r0, CHUNK)

        ys, _ = _gate_mlp_chunk(uv_ref[rows, 0:A_WIDTH], uv_ref[rows, A_WIDTH:2 * A_WIDTH], gv_ref[...], ws_ref, bs_ref, causal)
        for g in range(A_HEADS):
            ycat_ref[rows, g * A_HEAD_DIM:(g + 1) * A_HEAD_DIM] = ys[g]

        dt = dt_ref[rows, :]
        d_a = dt * a_col
        h1, h2, h3 = _split3(d_a)
        acum = _dot(tril, h1) + (_dot(tril, h2) + _dot(tril, h3))
        dt_t = dtT_ref[:, rows]
        g1, g2, g3 = _split3(dt_t * a_row)
        acum_t = _dot(g1, triu) + (_dot(g2, triu) + _dot(g3, triu))

        xs = xs_ref[rows, 0:B_WIDTH]
        zz = z_ref[rows, :]
        y_pairs = []
        for g in range(B_GROUPS):
            bg = xs_ref[rows, B_WIDTH + g * D_STATE:B_WIDTH + (g + 1) * D_STATE]
            cg = xs_ref[rows, B_WIDTH + BC_WIDTH + g * D_STATE:B_WIDTH + BC_WIDTH + (g + 1) * D_STATE]
            cbm = _dot_nt(cg.astype(BF16), bg.astype(BF16))
            for pr in range(B_HEADS // B_GROUPS // 2):
                pair = g * 4 + pr
                xpair = xs[:, pair * 128:(pair + 1) * 128]
                hprev = hst_ref[pair]
                gds, xhs, css, hps, bscs, alasts = [], [], [], [], [], []
                for e in range(2):
                    hd = pair * 2 + e
                    ab = jnp.broadcast_to(acum[:, hd:hd + 1], (CHUNK, CHUNK))
                    seg = ab - acum_t[hd:hd + 1, :]
                    lm = jnp.exp(jnp.where(causal, seg, NEG_INF))
                    gds.append((cbm * lm * dt_t[hd:hd + 1, :]).astype(BF16))
                    sel_l = low_lane if e == 0 else jnp.logical_not(low_lane)
                    sel_r = low_half if e == 0 else jnp.logical_not(low_half)
                    xhs.append(jnp.where(sel_l, xpair, 0.0).astype(BF16))
                    hps.append(jnp.where(sel_r, hprev, 0.0).astype(BF16))
                    css.append((cg * jnp.exp(ab)).astype(BF16))
                    alast = ab[CHUNK - 1:CHUNK, :]
                    dt_col = jnp.broadcast_to(dt[:, hd:hd + 1], (CHUNK, CHUNK))
                    bscs.append((bg * (jnp.exp(alast - ab) * dt_col)).astype(BF16))
                    alasts.append(alast)
                x2 = jnp.concatenate(xhs, axis=0)
                y_pairs.append(_dot(jnp.concatenate(gds, axis=1), x2)
                               + _dot_nt(jnp.concatenate(css, axis=1), jnp.concatenate(hps, axis=1)))
                st = _dot_tn(x2, jnp.concatenate(bscs, axis=0))
                decay = jnp.exp(jnp.where(low_half, alasts[0], alasts[1]))
                hst_ref[pair] = hprev * decay + st
        y = jnp.concatenate(y_pairs, axis=1) + dsk_ref[...] * xs
        y = y * _silu(zz)
        half = B_HEADS * B_HEAD_DIM // B_GROUPS
        for g in range(B_GROUPS):
            sl = slice(g * half, (g + 1) * half)
            ycat_ref[rows, A_WIDTH + g * half:A_WIDTH + (g + 1) * half] = _rmsnorm(y[:, sl], gssm_ref[:, sl]).astype(BF16)
        return carry

    lax.fori_loop(0, n_chunks, chunk_body, 0)

    o_ref[...] = x + mod_ref[2] * _dot2(ycat_ref[...], wout_ref)

    @pl.when(t == pl.num_programs(1) - 1)
    def _():
        ssm_ref[...] = hst_ref[...]


def _mix_call(x, mod1, p, tm=MIX_ROWS):
    b, s, d = x.shape
    n_pairs = B_HEADS // 2
    in_specs = [
        pl.BlockSpec((None, tm, d), lambda i, t: (i, t, 0)),
        pl.BlockSpec((None, 3, 1, d), lambda i, t: (i, 0, 0, 0)),
        _full((1, d)),
        _full(p["w_uv"].shape), _full(p["w_z"].shape), _full(p["w_xbc"].shape),
        _full(p["w_dt"].shape), _full(p["w_dtT"].shape),
        _full((1, LANES)), _full((LANES, 1)), _full((1, LANES)), _full((LANES, LANES)),
        _full((1, A_WIDTH)), _full((A_HEADS, CHUNK, CHUNK)), _full((A_HEADS, CHUNK, CHUNK)),
        _full((CONV_W, CONV_DIM)), _full((1, CONV_DIM)), _full((1, B_WIDTH)), _full((1, B_WIDTH)),
        _full(p["w_out"].shape),
    ]
    out_specs = [
        pl.BlockSpec((None, tm, d), lambda i, t: (i, t, 0)),
        pl.BlockSpec((None, n_pairs, 128, D_STATE), lambda i, t: (i, 0, 0, 0)),
        pl.BlockSpec((None, SUBLANES, CONV_DIM), lambda i, t: (i, 0, 0)),
    ]
    out_shape = [
        jax.ShapeDtypeStruct((b, s, d), F32),
        jax.ShapeDtypeStruct((b, n_pairs, 128, D_STATE), F32),
        jax.ShapeDtypeStruct((b, SUBLANES, CONV_DIM), F32),
    ]
    scratch = [
        pltpu.VMEM((n_pairs, 128, D_STATE), F32),
        pltpu.VMEM((SUBLANES, CONV_DIM), F32),
        pltpu.VMEM((tm, 2 * A_WIDTH), F32),
        pltpu.VMEM((tm, B_WIDTH), F32),
        pltpu.VMEM((tm, CONV_DIM), F32),
        pltpu.VMEM((tm, LANES), F32),
        pltpu.VMEM((LANES, tm), F32),
        pltpu.VMEM((tm, MIX_WIDTH), BF16),
    ]
    return pl.pallas_call(
        _mix_kernel,
        grid=(b, s // tm),
        in_specs=in_specs, out_specs=out_specs, out_shape=out_shape, scratch_shapes=scratch,
        compiler_params=_params(("parallel", "arbitrary")),
        name="mix_prompt",
    )(x, mod1, p["g_mix"], p["w_uv"], p["w_z"], p["w_xbc"], p["w_dt"], p["w_dtT"],
      p["dt_bias_row"], p["dt_bias_col"], p["a_log_row"], p["a_log_sq"],
      p["g_v"], p["w_s"], p["b_s_rep"], p["conv_w"], p["conv_b"], p["d_skip_exp"], p["g_ssm"], p["w_out"])


def _s1_kernel(x_ref, mod_ref, gmix_ref, wuv_ref, wz_ref, wxbc_ref, wdt_ref, dtb_ref, alog_ref, exp64_ref,
               gv_ref, ws0_ref, bs0_ref, cw_ref, cb_ref, sconv_ref,
               v_ref, cnew_ref, ya_ref, z_ref, xs_ref, bc_ref, xdtT_ref, dec_ref):
    x = x_ref[...]
    h = _rmsnorm(x, gmix_ref[...]) * (1.0 + mod_ref[1]) + mod_ref[0]
    hb = h.astype(BF16)
    uv = _dot(hb, wuv_ref[...])
    gv = gv_ref[...]
    for g in range(A_HEADS):
        sl = slice(g * A_HEAD_DIM, (g + 1) * A_HEAD_DIM)
        ug = _gelu(uv[:, sl])
        vn = _rmsnorm(_gelu(uv[:, A_WIDTH + g * A_HEAD_DIM:A_WIDTH + (g + 1) * A_HEAD_DIM]), gv[:, sl])
        v_ref[:, sl] = vn
        s = ws0_ref[:, sl] * vn + bs0_ref[:, sl]
        ya_ref[:, sl] = (ug * s).astype(BF16)
    z_ref[...] = _dot(hb, wz_ref[...])
    raw = _dot(hb, wxbc_ref[...])
    cd = raw.shape[1]
    acc = cb_ref[...] + raw * cw_ref[CONV_W - 1:CONV_W, :]
    for k in range(CONV_W - 1):
        acc = acc + sconv_ref[:, k * cd:(k + 1) * cd] * cw_ref[k:k + 1, :]
    cnew_ref[:, 0:(CONV_W - 2) * cd] = sconv_ref[:, cd:(CONV_W - 1) * cd]
    cnew_ref[:, (CONV_W - 2) * cd:(CONV_W - 1) * cd] = raw
    xbc = _silu(acc)
    xs = xbc[:, 0:B_WIDTH]
    xs_ref[...] = xs
    bc_ref[...] = xbc[:, B_WIDTH:CONV_DIM]
    dt = _softplus(_dot(hb, wdt_ref[...]) + dtb_ref[...])
    dec_ref[...] = jnp.exp(dt * (-jnp.exp(alog_ref[...])))
    f1, f2, f3 = _split3(dt)
    dt64 = _dot(f1, exp64_ref[...]) + (_dot(f2, exp64_ref[...]) + _dot(f3, exp64_ref[...]))
    xdtT_ref[...] = (xs * dt64).T.astype(BF16)


def _s1_call(x, mod1, sconv, p):
    n, d = x.shape
    outs = [
        jax.ShapeDtypeStruct((n, A_WIDTH), F32),
        jax.ShapeDtypeStruct((n, (CONV_W - 1) * CONV_DIM), F32),
        jax.ShapeDtypeStruct((n, A_WIDTH), BF16),
        jax.ShapeDtypeStruct((n, B_WIDTH), F32),
        jax.ShapeDtypeStruct((n, B_WIDTH), F32),
        jax.ShapeDtypeStruct((n, 2 * BC_WIDTH), F32),
        jax.ShapeDtypeStruct((B_WIDTH, n), BF16),
        jax.ShapeDtypeStruct((n, LANES), F32),
    ]
    args = (x, mod1, p["g_mix"], p["w_uv"], p["w_z"], p["w_xbc"], p["w_dt"], p["dt_bias_row"], p["a_log_row"],
            p["expand64"], p["g_v"], p["w_s00"], p["b_s0"], p["conv_w"], p["conv_b"], sconv)
    return pl.pallas_call(
        _s1_kernel,
        grid=(1,),
        in_specs=[_full(a.shape) for a in args],
        out_specs=[_full(o.shape) for o in outs],
        out_shape=outs,
        compiler_params=_params(("arbitrary",)),
        name="mix_sample_in",
    )(*args)


def _s2_kernel(dec_ref, st_ref, xdtT_ref, bc_ref, cblk_ref, o_ref, y_ref):
    i = pl.program_id(0)
    bt = st_ref.shape[0]
    n = bc_ref.shape[0]
    half = B_HEADS * B_HEAD_DIM // B_GROUPS
    rowi = lax.broadcasted_iota(jnp.int32, (n, 1), 0)
    for bb in range(bt):
        b = i * bt + bb
        for g in range(B_GROUPS):
            bm = bc_ref[:, g * D_STATE:(g + 1) * D_STATE]
            rb = jnp.where(rowi == b, bm, 0.0).astype(BF16)
            outer = _dot(xdtT_ref[g * half:(g + 1) * half, :], rb)
            for k in range(B_HEADS // B_GROUPS):
                hd = g * (B_HEADS // B_GROUPS) + k
                sl = slice(k * B_HEAD_DIM, (k + 1) * B_HEAD_DIM)
                o_ref[bb, g, sl, :] = st_ref[bb, g, sl, :] * dec_ref[b, hd] + outer[sl, :]
            crow = cblk_ref[bb:bb + 1, BC_WIDTH + g * D_STATE:BC_WIDTH + (g + 1) * D_STATE]
            c8 = jnp.broadcast_to(crow, (SUBLANES, D_STATE)).astype(BF16)
            yr = _dot_nt(c8, o_ref[bb, g].astype(BF16))
            y_ref[bb:bb + 1, g * half:(g + 1) * half] = yr[0:1]


def _s2_call(dec, state, xdtT, bc, bt=S2_TOKENS):
    n = state.shape[0]
    half = B_HEADS * B_HEAD_DIM // B_GROUPS
    st = state.reshape(n, B_GROUPS, half, D_STATE)
    grid_spec = pltpu.PrefetchScalarGridSpec(
        num_scalar_prefetch=1,
        grid=(n // bt,),
        in_specs=[
            pl.BlockSpec((bt, B_GROUPS, half, D_STATE), lambda i, d: (i, 0, 0, 0)),
            pl.BlockSpec(xdtT.shape, lambda i, d: (0, 0)),
            pl.BlockSpec(bc.shape, lambda i, d: (0, 0)),
            pl.BlockSpec((bt, bc.shape[1]), lambda i, d: (i, 0)),
        ],
        out_specs=[
            pl.BlockSpec((bt, B_GROUPS, half, D_STATE), lambda i, d: (i, 0, 0, 0)),
            pl.BlockSpec((bt, B_WIDTH), lambda i, d: (i, 0)),
        ],
    )
    new_state, y = pl.pallas_call(
        _s2_kernel,
        grid_spec=grid_spec,
        out_shape=[jax.ShapeDtypeStruct(st.shape, F32), jax.ShapeDtypeStruct((n, B_WIDTH), F32)],
        compiler_params=_params(("arbitrary",)),
        name="mix_sample_state",
    )(dec, st, xdtT, bc, bc)
    return new_state.reshape(state.shape), y


def _s3_kernel(x_ref, mod_ref, y_ref, xs_ref, z_ref, ya_ref, dsk_ref, gssm_ref, wout_ref, o_ref):
    y = (y_ref[...] + dsk_ref[...] * xs_ref[...]) * _silu(z_ref[...])
    half = B_HEADS * B_HEAD_DIM // B_GROUPS
    parts = [ya_ref[...]]
    for g in range(B_GROUPS):
        sl = slice(g * half, (g + 1) * half)
        parts.append(_rmsnorm(y[:, sl], gssm_ref[:, sl]).astype(BF16))
    cat = jnp.concatenate(parts, axis=1)
    o_ref[...] = x_ref[...] + mod_ref[2] * _dot(cat, wout_ref[...])


def _s3_call(x, mod1, y, xs, z, ya, p):
    args = (x, mod1, y, xs, z, ya, p["d_skip_exp"], p["g_ssm"], p["w_out"])
    return pl.pallas_call(
        _s3_kernel,
        grid=(1,),
        in_specs=[_full(a.shape) for a in args],
        out_specs=_full(x.shape),
        out_shape=jax.ShapeDtypeStruct(x.shape, F32),
        compiler_params=_params(("arbitrary",)),
        name="mix_sample_out",
    )(*args)


def _cand_blocks(a, b):
    row8 = lax.broadcasted_iota(jnp.int32, (SUBLANES, 1), 0)
    a8 = a[0:SUBLANES]
    blocks = []
    for j in range(SUBLANES):
        blk = a8 + b[j:j + 1]
        cnt = min(SUBLANES, (TOPK + 1) // (j + 1))
        if cnt < SUBLANES:
            blk = jnp.where(row8 < cnt, blk, NEG_INF)
        blocks.append(blk)
    blocks.append(a[0:1] + b[SUBLANES:TOPK])
    blocks.append(a[SUBLANES:TOPK] + b[0:1])
    last = jnp.where(row8 == 0, a[0:1] + b[TOPK:TOPK + 1], jnp.where(row8 == 1, a[TOPK:TOPK + 1] + b[0:1], NEG_INF))
    blocks.append(last)
    return blocks


def _merge_exchange_pairs(n):
    pairs = []
    t = max(1, math.ceil(math.log2(n)))
    p = 1 << (t - 1)
    while p > 0:
        q, r, d = 1 << (t - 1), 0, p
        while d > 0:
            pairs.extend((i, i + d) for i in range(n - d) if (i & p) == r)
            d, q, r = q - p, q >> 1, p
        p >>= 1
    return pairs


def _sort_blocks_desc(blocks):
    blocks = list(blocks)
    for i, j in _merge_exchange_pairs(len(blocks)):
        hi = jnp.maximum(blocks[i], blocks[j])
        blocks[j] = jnp.minimum(blocks[i], blocks[j])
        blocks[i] = hi
    return blocks


def _pop_largest(blocks, k):
    blocks = list(blocks)
    nb = len(blocks)
    rows = []
    for t in range(k):
        m = jnp.max(blocks[0], axis=0, keepdims=True)
        rows.append(m)
        need = min(k - t - 1, nb)
        if need == 0:
            break
        sel = blocks[0] == m
        for j in range(need):
            nxt = blocks[j + 1] if j + 1 < nb else NEG_INF
            blocks[j] = jnp.where(sel, nxt, blocks[j])
    return rows


def _peer_tile(c, n_chunks, x_ref, mod_ref, gffn_ref, gfin_ref, wqT_ref, keys_ref, u_ref, vt_ref, y_ref,
               h2_ref, q_ref, s1_ref, s2_ref, e1_ref, e2_ref, tops_ref, act_ref, coef_ref, outT_ref):
    n_blk = u_ref.shape[0] // N_KEYS

    @pl.when(c == 0)
    def _():
        x = x_ref[...]
        h = _rmsnorm(x, gffn_ref[...]) * (1.0 + mod_ref[1]) + mod_ref[0]
        ht = h.T.astype(BF16)
        h2_ref[...] = ht
        qh = wqT_ref.shape[0] // 2
        q_ref[0:qh, :] = _dot(wqT_ref[0:qh, :], ht)
        q_ref[qh:, :] = _dot(wqT_ref[qh:, :], ht)
        outT_ref[...] = jnp.zeros_like(outT_ref)

        def one_head(hh, top_ref):
            scores = []
            for s in range(2):
                r0 = pl.multiple_of((2 * hh + s) * N_KEYS, N_KEYS)
                qs = q_ref[pl.ds(r0, N_KEYS), :].astype(BF16)
                sc = _dot(keys_ref[2 * hh + s], qs) * LOG2E
                if s == 0:
                    s1_ref[hh] = sc
                else:
                    s2_ref[hh] = sc
                scores.append(sc)
            n_tok = x_ref.shape[0]
            strip = min(n_tok, PEER_SELECT_STRIP)
            for c0 in range(0, n_tok, strip):
                cols = slice(c0, c0 + strip)
                for s in range(2):
                    srt = _sort_blocks_desc([scores[s][j * SUBLANES:(j + 1) * SUBLANES, cols]
                                             for j in range(N_KEYS // SUBLANES)])
                    for k, m in enumerate(_pop_largest(srt, TOPK + 1)):
                        top_ref[s, k:k + 1, cols] = m
                a = top_ref[0, 0:TOPK + 1, cols]
                b = top_ref[1, 0:TOPK + 1, cols]
                best = a[0:1] + b[0:1]
                pops = _pop_largest(_sort_blocks_desc(_cand_blocks(a, b)), TOPK + 1)
                zsum = None
                for m in pops[:TOPK]:
                    e = jnp.exp2(m - best)
                    zsum = e if zsum is None else zsum + e
                thr = jnp.where(pops[TOPK] == NEG_INF, pops[TOPK - 1], 0.5 * (pops[TOPK - 1] + pops[TOPK]))
                s1 = s1_ref[hh, :, cols]
                s1_ref[hh, :, cols] = thr - s1
                e1_ref[hh, :, cols] = 0.5 * jnp.exp2(s1 - (a[0:1] + jnp.log2(zsum)))
                e2_ref[hh, :, cols] = jnp.exp2(scores[1][:, cols] - b[0:1]).astype(BF16)

        def head_body(hp, carry):
            for e in range(PEER_HEADS_PER_TRIP):
                one_head(PEER_HEADS_PER_TRIP * hp + e, tops_ref.at[e])
            return carry

        lax.fori_loop(0, R_HEADS // PEER_HEADS_PER_TRIP, head_body, 0)

    eh = u_ref.shape[0] // 2
    dh = vt_ref.shape[0] // 2

    @pl.when(c < n_chunks)
    def _():
        act_w = act_ref.at[c % 2]
        act_w[0:eh, :] = _dot(u_ref[0:eh, :], h2_ref[...])
        act_w[eh:, :] = _dot(u_ref[eh:, :], h2_ref[...])

    @pl.when(c >= 1)
    def _():
        act_r = act_ref.at[(c - 1) % 2]
        i1 = pl.multiple_of((c - 1) * n_blk, n_blk)
        u_rows = [s1_ref[hh, pl.ds(i1, n_blk), :] for hh in range(R_HEADS)]
        e1_rows = [e1_ref[hh, pl.ds(i1, n_blk), :] for hh in range(R_HEADS)]
        for r in range(n_blk):
            acc = None
            for hh in range(R_HEADS):
                prod = e2_ref[hh] * e1_rows[hh][r:r + 1, :].astype(BF16)
                gate = jnp.where(s2_ref[hh] >= u_rows[hh][r:r + 1, :], prod, jnp.zeros_like(prod))
                acc = gate if acc is None else acc + gate
            blk = slice(r * N_KEYS, (r + 1) * N_KEYS)
            x = act_r[blk, :]
            coef_ref[blk, :] = acc * (x * (1.0 + lax.erf(x * (1.0 / math.sqrt(2.0))))).astype(BF16)
        outT_ref[0:dh, :] += _dot(vt_ref[0:dh, :], coef_ref[...])
        outT_ref[dh:, :] += _dot(vt_ref[dh:, :], coef_ref[...])

    @pl.when(c == n_chunks)
    def _():
        x2 = x_ref[...] + mod_ref[2] * outT_ref[...].T
        y_ref[...] = _rmsnorm(x2, gfin_ref[...]) * (1.0 + mod_ref[4]) + mod_ref[3]


def _peer_kernel(x_ref, mod_ref, gffn_ref, gfin_ref, wqT_ref, keys_ref, u_ref, vt_ref, y_ref,
                 h2_ref, q_ref, s1_ref, s2_ref, e1_ref, e2_ref, tops_ref, act_ref, coef_ref, outT_ref,
                 *, n_chunks, n_sub):
    c = pl.program_id(1)
    tl = x_ref.shape[0] // n_sub

    def sub_tile(j, carry):
        rows = pl.ds(pl.multiple_of(j * tl, tl), tl)
        _peer_tile(c, n_chunks, x_ref.at[rows], mod_ref, gffn_ref, gfin_ref, wqT_ref, keys_ref, u_ref, vt_ref,
                   y_ref.at[rows], h2_ref.at[j], q_ref, s1_ref.at[j], s2_ref.at[j], e1_ref.at[j], e2_ref.at[j],
                   tops_ref, act_ref.at[j], coef_ref, outT_ref.at[j])
        return carry

    lax.fori_loop(0, n_sub, sub_tile, 0)


def _peer_call(x, mod5, tiles_per_row, p, tl, n_sub, ec=PEER_EXPERTS):
    t, d = x.shape
    n_exp = p["expert_u"].shape[0]
    r = mod5.shape[2]
    scratch = [
        pltpu.VMEM((n_sub, d, tl), BF16),
        pltpu.VMEM((R_HEADS * 2 * N_KEYS, tl), F32),
        pltpu.VMEM((n_sub, R_HEADS, N_KEYS, tl), F32),
        pltpu.VMEM((n_sub, R_HEADS, N_KEYS, tl), F32),
        pltpu.VMEM((n_sub, R_HEADS, N_KEYS, tl), F32),
        pltpu.VMEM((n_sub, R_HEADS, N_KEYS, tl), BF16),
        pltpu.VMEM((PEER_HEADS_PER_TRIP, 2, TOPK + SUBLANES, tl), F32),
        pltpu.VMEM((n_sub, 2, ec, tl), F32),
        pltpu.VMEM((ec, tl), BF16),
        pltpu.VMEM((n_sub, d, tl), F32),
    ]
    n_chunks = n_exp // ec
    return pl.pallas_call(
        functools.partial(_peer_kernel, n_chunks=n_chunks, n_sub=n_sub),
        grid=(t // (n_sub * tl), n_chunks + 1),
        in_specs=[
            pl.BlockSpec((n_sub * tl, d), lambda i, c: (i, 0)),
            pl.BlockSpec((None, 5, r, d), lambda i, c: (i // tiles_per_row, 0, 0, 0)),
            _full((1, d)), _full((1, d)),
            _full(p["w_qT"].shape), _full(p["keys"].shape),
            pl.BlockSpec((ec, d), lambda i, c: (jnp.minimum(c, n_chunks - 1), 0)),
            pl.BlockSpec((d, ec), lambda i, c: (0, jnp.maximum(c - 1, 0))),
        ],
        out_specs=pl.BlockSpec((n_sub * tl, d), lambda i, c: (i, 0)),
        out_shape=jax.ShapeDtypeStruct((t, d), F32),
        scratch_shapes=scratch,
        compiler_params=_params(("parallel", "arbitrary")),
        name="peer",
    )(x, mod5, p["g_ffn"], p["g_final"], p["w_qT"], p["keys"], p["expert_u"], p["expert_vT"])


def _prep(w_in, g_mix, g_v, w_s, b_s, conv_w, conv_b, dt_bias, a_log, d_skip, g_ssm, w_out, g_ffn, w_q,
          sub_keys, expert_u, expert_v, g_final):
    d = w_in.shape[0]
    o_z = 2 * A_WIDTH
    o_x = o_z + B_WIDTH
    o_dt = o_x + CONV_DIM
    w_dt = jnp.zeros((d, LANES), F32).at[:, :B_HEADS].set(w_in[:, o_dt:o_dt + B_HEADS])
    pad16 = lambda v: jnp.zeros((LANES,), F32).at[:B_HEADS].set(v)
    heads = jnp.arange(LANES)[:, None]
    p = {
        "g_mix": g_mix.reshape(1, d),
        "w_uv": w_in[:, 0:o_z].astype(BF16),
        "w_z": w_in[:, o_z:o_x].astype(BF16),
        "w_xbc": w_in[:, o_x:o_dt].astype(BF16),
        "w_dt": w_dt.astype(BF16),
        "w_dtT": w_dt.T.astype(BF16),
        "dt_bias_row": pad16(dt_bias).reshape(1, LANES),
        "dt_bias_col": pad16(dt_bias).reshape(LANES, 1),
        "a_log_row": pad16(a_log).reshape(1, LANES),
        "a_log_sq": jnp.broadcast_to(pad16(a_log).reshape(LANES, 1), (LANES, LANES)),
        "expand64": (heads == jnp.arange(B_HEADS * B_HEAD_DIM)[None, :] // B_HEAD_DIM).astype(BF16),
        "g_v": g_v.reshape(1, -1),
        "w_s": w_s,
        "b_s_rep": jnp.broadcast_to(b_s[:, :, None], (A_HEADS, CHUNK, CHUNK)),
        "w_s00": jnp.repeat(w_s[:, 0, 0], A_HEAD_DIM).reshape(1, -1),
        "b_s0": jnp.repeat(b_s[:, 0], A_HEAD_DIM).reshape(1, -1),
        "conv_w": conv_w,
        "conv_b": conv_b.reshape(1, -1),
        "d_skip_exp": jnp.repeat(d_skip, B_HEAD_DIM).reshape(1, -1),
        "g_ssm": g_ssm.reshape(1, -1),
        "w_out": w_out.astype(BF16),
        "g_ffn": g_ffn.reshape(1, d),
        "g_final": g_final.reshape(1, d),
        "w_qT": w_q.T.astype(BF16),
        "keys": sub_keys.reshape(R_HEADS * 2, N_KEYS, -1).astype(BF16),
        "expert_u": expert_u.astype(BF16),
        "expert_vT": expert_v.T.astype(BF16),
    }
    return p


def kernel(x_prompt, x_sample, c_prompt, c_sample, state_ssm, state_conv, w_ada, b_ada, g_mix, w_in, g_v, w_s, b_s,
           conv_w, conv_b, dt_bias, a_log, d_skip, g_ssm, w_out, g_ffn, w_q, sub_keys, expert_u, expert_v,
           w_ada_f, b_ada_f, g_final):
    assert w_ada.shape[0] == 1, "single-layer trunk"
    bp, seq, d = x_prompt.shape
    ns = x_sample.shape[0]
    p = _prep(w_in[0], g_mix[0], g_v[0], w_s[0], b_s[0], conv_w[0], conv_b[0], dt_bias[0], a_log[0], d_skip[0],
              g_ssm[0], w_out[0], g_ffn[0], w_q[0], sub_keys[0], expert_u[0], expert_v[0], g_final)

    c_all = jnp.concatenate([c_prompt, c_sample], axis=0)
    mod = _ada_call(c_all, w_ada[0], b_ada[0]).reshape(bp + ns, N_MOD, d)
    modf = _ada_call(c_all, w_ada_f, b_ada_f).reshape(bp + ns, 2, d)
    mod5 = jnp.concatenate([mod[:, 3:6], modf], axis=1)

    x1_p, ssm_p, ctail_p = _mix_call(x_prompt, mod[:bp, 0:3].reshape(bp, 3, 1, d), p)
    y_p = _peer_call(x1_p.reshape(bp * seq, d), mod5[:bp].reshape(bp, 5, 1, d), seq // (PEER_SUBTILES * PEER_TOKENS), p,
                     PEER_TOKENS, PEER_SUBTILES)

    xs_in = x_sample.reshape(ns, d)
    mod1_s = jnp.transpose(mod[bp:, 0:3], (1, 0, 2))
    sconv = state_conv[0].reshape(ns, -1)
    v_s, cnew_s, ya_s, z_s, xc_s, bc_s, xdtT_s, dec_s = _s1_call(xs_in, mod1_s, sconv, p)
    ssm_s, yssd_s = _s2_call(dec_s[:, :B_HEADS], state_ssm[0], xdtT_s, bc_s)
    x1_s = _s3_call(xs_in, mod1_s, yssd_s, xc_s, z_s, ya_s, p)
    mod5_s = jnp.transpose(mod5[bp:], (1, 0, 2)).reshape(1, 5, ns, d)
    y_s = _peer_call(x1_s, mod5_s, 1, p, ns, 1)

    return (
        y_p.reshape(bp, seq, d),
        y_s.reshape(ns, 1, d),
        ssm_p.reshape(1, bp, B_HEADS, B_HEAD_DIM, D_STATE),
        ctail_p[:, SUBLANES - (CONV_W - 1):, :].reshape(1, bp, CONV_W - 1, -1),
        ssm_s.reshape(1, ns, B_HEADS, B_HEAD_DIM, D_STATE),
        cnew_s.reshape(1, ns, CONV_W - 1, -1),
        v_s.reshape(1, ns, 1, -1),
    )
```

```python
import functools
import math

import jax
import jax.numpy as jnp
from jax import lax
from jax.experimental import pallas as pl
from jax.experimental.pallas import tpu as pltpu

F32 = jnp.float32
BF16 = jnp.bfloat16
NEG_INF = float("-inf")

EPS = 1e-6
A_HEADS = 8
A_HEAD_DIM = 128
CHUNK = 128
B_HEADS = 16
B_HEAD_DIM = 64
B_GROUPS = 2
D_STATE = 128
CONV_W = 4
N_KEYS = 128
R_HEADS = 8
TOPK = 16
N_MOD = 6
A_WIDTH = A_HEADS * A_HEAD_DIM
B_WIDTH = B_HEADS * B_HEAD_DIM
BC_WIDTH = B_GROUPS * D_STATE
CONV_DIM = B_WIDTH + 2 * BC_WIDTH
MIX_WIDTH = A_WIDTH + B_WIDTH

LANES = 128
SUBLANES = 8
VMEM_LIMIT = 56 * 1024 * 1024

MIX_ROWS = 512
PEER_TOKENS = 256
PEER_EXPERTS = 2048
PEER_SUBTILES = 2
PEER_SELECT_STRIP = 256
PEER_HEADS_PER_TRIP = 8
LOG2E = 1.4426950408889634
S2_TOKENS = 8
TABLE_ROWS = 1024


def _dot(a, b):
    return jnp.dot(a, b, preferred_element_type=F32)


def _dot_nt(a, b):
    return lax.dot_general(a, b, (((1,), (1,)), ((), ())), preferred_element_type=F32)


def _dot_tn(a, b):
    return lax.dot_general(a, b, (((0,), (0,)), ((), ())), preferred_element_type=F32)


def _split2(x):
    hi = x.astype(BF16)
    lo = (x - hi.astype(F32)).astype(BF16)
    return hi, lo


def _split3(x):
    hi = x.astype(BF16)
    r = x - hi.astype(F32)
    mid = r.astype(BF16)
    lo = (r - mid.astype(F32)).astype(BF16)
    return hi, mid, lo


def _dot_x3(a, b):
    a1, a2 = _split2(a)
    b1, b2 = _split2(b)
    return _dot(a1, b1) + (_dot(a1, b2) + _dot(a2, b1))


def _silu(x):
    return x / (1.0 + jnp.exp(-x))


def _gelu(x):
    return 0.5 * x * (1.0 + lax.erf(x * (1.0 / math.sqrt(2.0))))


def _softplus(x):
    return jnp.maximum(x, 0.0) + jnp.log1p(jnp.exp(-jnp.abs(x)))


def _rmsnorm(x, g):
    return x * lax.rsqrt(jnp.mean(x * x, axis=-1, keepdims=True) + EPS) * g


def _full(shape):
    nd = len(shape)
    return pl.BlockSpec(shape, lambda *_: (0,) * nd, pipeline_mode=pl.Buffered(1))


def _dot2(a, w_ref):
    h = w_ref.shape[1] // 2
    return jnp.concatenate([_dot(a, w_ref[:, :h]), _dot(a, w_ref[:, h:])], axis=1)


def _params(sem):
    return pltpu.CompilerParams(dimension_semantics=sem, vmem_limit_bytes=VMEM_LIMIT)


def _ada_kernel(c_ref, w_ref, b_ref, o_ref):
    o_ref[...] = _dot_x3(_silu(c_ref[...]), w_ref[...]) + b_ref[...]


def _ada_call(c, w, b, tn=512):
    m, k = c.shape
    n = w.shape[1]
    return pl.pallas_call(
        _ada_kernel,
        grid=(n // tn,),
        in_specs=[_full((m, k)), pl.BlockSpec((k, tn), lambda j: (0, j)), pl.BlockSpec((1, tn), lambda j: (0, j))],
        out_specs=pl.BlockSpec((m, tn), lambda j: (0, j)),
        out_shape=jax.ShapeDtypeStruct((m, n), F32),
        compiler_params=_params(("parallel",)),
        name="ada",
    )(c, w, b.reshape(1, n))


def _gate_mlp_chunk(u, v, gv, ws_ref, bs_ref, causal):
    ys, vs = [], []
    for g in range(A_HEADS):
        sl = slice(g * A_HEAD_DIM, (g + 1) * A_HEAD_DIM)
        ug = _gelu(u[:, sl])
        vn = _rmsnorm(_gelu(v[:, sl]), gv[:, sl])
        w = jnp.where(causal, ws_ref[g], 0.0).astype(BF16)
        s = _dot(w, vn.astype(BF16)) + bs_ref[g]
        ys.append((ug * s).astype(BF16))
        vs.append(vn)
    return ys, vs


def _mix_kernel(x_ref, mod_ref, gmix_ref, wuv_ref, wz_ref, wxbc_ref, wdt_ref, wdtT_ref,
                dtb_ref, dtbT_ref, alog_ref, alogT_ref, gv_ref, ws_ref, bs_ref,
                cw_ref, cb_ref, dsk_ref, gssm_ref, wout_ref,
                o_ref, ssm_ref, ctail_ref,
                hst_ref, tail_ref, uv_ref, z_ref, xs_ref, dt_ref, dtT_ref, ycat_ref):
    t = pl.program_id(1)
    tm = x_ref.shape[0]
    n_chunks = tm // CHUNK

    @pl.when(t == 0)
    def _():
        hst_ref[...] = jnp.zeros_like(hst_ref)
        tail_ref[...] = jnp.zeros_like(tail_ref)

    x = x_ref[...]
    h = _rmsnorm(x, gmix_ref[...]) * (1.0 + mod_ref[1]) + mod_ref[0]
    hb = h.astype(BF16)
    uv_ref[...] = _dot2(hb, wuv_ref)
    z_ref[...] = _dot2(hb, wz_ref)
    dt_ref[...] = _softplus(_dot(hb, wdt_ref[...]) + dtb_ref[...])
    dtT_ref[...] = _softplus(_dot_nt(wdtT_ref[...], hb) + dtbT_ref[...])

    cur = _dot2(hb, wxbc_ref)
    prev = tail_ref[...]
    row8 = lax.broadcasted_iota(jnp.int32, (SUBLANES, 1), 0)
    acc = cb_ref[...] + cur * cw_ref[CONV_W - 1:CONV_W, :]
    for k in range(1, CONV_W):
        r = pltpu.roll(cur, k, axis=0)
        head = jnp.where(row8 < k, pltpu.roll(prev, k, axis=0), r[0:SUBLANES])
        shifted = jnp.concatenate([head, r[SUBLANES:]], axis=0)
        acc = acc + shifted * cw_ref[CONV_W - 1 - k:CONV_W - k, :]
    xs_ref[...] = _silu(acc)
    new_tail = cur[tm - SUBLANES:tm]
    tail_ref[...] = new_tail
    ctail_ref[...] = new_tail

    ri = lax.broadcasted_iota(jnp.int32, (CHUNK, CHUNK), 0)
    ci = lax.broadcasted_iota(jnp.int32, (CHUNK, CHUNK), 1)
    causal = ri >= ci
    tril = jnp.where(causal, 1.0, 0.0).astype(BF16)
    triu = jnp.where(ri <= ci, 1.0, 0.0).astype(BF16)
    low_half = ri < B_HEAD_DIM
    low_lane = ci < B_HEAD_DIM
    a_col = -jnp.exp(alog_ref[...])
    a_row = -jnp.exp(alogT_ref[...])

    def chunk_body(c, carry):
        r0 = pl.multiple_of(c * CHUNK, CHUNK)
        rows = pl.ds(r0, CHUNK)

        ys, _ = _gate_mlp_chunk(uv_ref[rows, 0:A_WIDTH], uv_ref[rows, A_WIDTH:2 * A_WIDTH], gv_ref[...], ws_ref, bs_ref, causal)
        for g in range(A_HEADS):
            ycat_ref[rows, g * A_HEAD_DIM:(g + 1) * A_HEAD_DIM] = ys[g]

        dt = dt_ref[rows, :]
        d_a = dt * a_col
        h1, h2, h3 = _split3(d_a)
        acum = _dot(tril, h1) + (_dot(tril, h2) + _dot(tril, h3))
        dt_t = dtT_ref[:, rows]
        g1, g2, g3 = _split3(dt_t * a_row)
        acum_t = _dot(g1, triu) + (_dot(g2, triu) + _dot(g3, triu))

        xs = xs_ref[rows, 0:B_WIDTH]
        zz = z_ref[rows, :]
        y_pairs = []
        for g in range(B_GROUPS):
            bg = xs_ref[rows, B_WIDTH + g * D_STATE:B_WIDTH + (g + 1) * D_STATE]
            cg = xs_ref[rows, B_WIDTH + BC_WIDTH + g * D_STATE:B_WIDTH + BC_WIDTH + (g + 1) * D_STATE]
            cbm = _dot_nt(cg.astype(BF16), bg.astype(BF16))
            for pr in range(B_HEADS // B_GROUPS // 2):
                pair = g * 4 + pr
                xpair = xs[:, pair * 128:(pair + 1) * 128]
                hprev = hst_ref[pair]
                gds, xhs, css, hps, bscs, alasts = [], [], [], [], [], []
                for e in range(2):
                    hd = pair * 2 + e
                    ab = jnp.broadcast_to(acum[:, hd:hd + 1], (CHUNK, CHUNK))
                    seg = ab - acum_t[hd:hd + 1, :]
                    lm = jnp.exp(jnp.where(causal, seg, NEG_INF))
                    gds.append((cbm * lm * dt_t[hd:hd + 1, :]).astype(BF16))
                    sel_l = low_lane if e == 0 else jnp.logical_not(low_lane)
                    sel_r = low_half if e == 0 else jnp.logical_not(low_half)
                    xhs.append(jnp.where(sel_l, xpair, 0.0).astype(BF16))
                    hps.append(jnp.where(sel_r, hprev, 0.0).astype(BF16))
                    css.append((cg * jnp.exp(ab)).astype(BF16))
                    alast = ab[CHUNK - 1:CHUNK, :]
                    dt_col = jnp.broadcast_to(dt[:, hd:hd + 1], (CHUNK, CHUNK))
                    bscs.append((bg * (jnp.exp(alast - ab) * dt_col)).astype(BF16))
                    alasts.append(alast)
                x2 = jnp.concatenate(xhs, axis=0)
                y_pairs.append(_dot(jnp.concatenate(gds, axis=1), x2)
                               + _dot_nt(jnp.concatenate(css, axis=1), jnp.concatenate(hps, axis=1)))
                st = _dot_tn(x2, jnp.concatenate(bscs, axis=0))
                decay = jnp.exp(jnp.where(low_half, alasts[0], alasts[1]))
                hst_ref[pair] = hprev * decay + st
        y = jnp.concatenate(y_pairs, axis=1) + dsk_ref[...] * xs
        y = y * _silu(zz)
        half = B_HEADS * B_HEAD_DIM // B_GROUPS
        for g in range(B_GROUPS):
            sl = slice(g * half, (g + 1) * half)
            ycat_ref[rows, A_WIDTH + g * half:A_WIDTH + (g + 1) * half] = _rmsnorm(y[:, sl], gssm_ref[:, sl]).astype(BF16)
        return carry

    lax.fori_loop(0, n_chunks, chunk_body, 0)

    o_ref[...] = x + mod_ref[2] * _dot2(ycat_ref[...], wout_ref)

    @pl.when(t == pl.num_programs(1) - 1)
    def _():
        ssm_ref[...] = hst_ref[...]


def _mix_call(x, mod1, p, tm=MIX_ROWS):
    b, s, d = x.shape
    n_pairs = B_HEADS // 2
    in_specs = [
        pl.BlockSpec((None, tm, d), lambda i, t: (i, t, 0)),
        pl.BlockSpec((None, 3, 1, d), lambda i, t: (i, 0, 0, 0)),
        _full((1, d)),
        _full(p["w_uv"].shape), _full(p["w_z"].shape), _full(p["w_xbc"].shape),
        _full(p["w_dt"].shape), _full(p["w_dtT"].shape),
        _full((1, LANES)), _full((LANES, 1)), _full((1, LANES)), _full((LANES, LANES)),
        _full((1, A_WIDTH)), _full((A_HEADS, CHUNK, CHUNK)), _full((A_HEADS, CHUNK, CHUNK)),
        _full((CONV_W, CONV_DIM)), _full((1, CONV_DIM)), _full((1, B_WIDTH)), _full((1, B_WIDTH)),
        _full(p["w_out"].shape),
    ]
    out_specs = [
        pl.BlockSpec((None, tm, d), lambda i, t: (i, t, 0)),
        pl.BlockSpec((None, n_pairs, 128, D_STATE), lambda i, t: (i, 0, 0, 0)),
        pl.BlockSpec((None, SUBLANES, CONV_DIM), lambda i, t: (i, 0, 0)),
    ]
    out_shape = [
        jax.ShapeDtypeStruct((b, s, d), F32),
        jax.ShapeDtypeStruct((b, n_pairs, 128, D_STATE), F32),
        jax.ShapeDtypeStruct((b, SUBLANES, CONV_DIM), F32),
    ]
    scratch = [
        pltpu.VMEM((n_pairs, 128, D_STATE), F32),
        pltpu.VMEM((SUBLANES, CONV_DIM), F32),
        pltpu.VMEM((tm, 2 * A_WIDTH), F32),
        pltpu.VMEM((tm, B_WIDTH), F32),
        pltpu.VMEM((tm, CONV_DIM), F32),
        pltpu.VMEM((tm, LANES), F32),
        pltpu.VMEM((LANES, tm), F32),
        pltpu.VMEM((tm, MIX_WIDTH), BF16),
    ]
    return pl.pallas_call(
        _mix_kernel,
        grid=(b, s // tm),
        in_specs=in_specs, out_specs=out_specs, out_shape=out_shape, scratch_shapes=scratch,
        compiler_params=_params(("parallel", "arbitrary")),
        name="mix_prompt",
    )(x, mod1, p["g_mix"], p["w_uv"], p["w_z"], p["w_xbc"], p["w_dt"], p["w_dtT"],
      p["dt_bias_row"], p["dt_bias_col"], p["a_log_row"], p["a_log_sq"],
      p["g_v"], p["w_s"], p["b_s_rep"], p["conv_w"], p["conv_b"], p["d_skip_exp"], p["g_ssm"], p["w_out"])


def _s1_kernel(x_ref, mod_ref, gmix_ref, wuv_ref, wz_ref, wxbc_ref, wdt_ref, dtb_ref, alog_ref, exp64_ref,
               gv_ref, ws0_ref, bs0_ref, cw_ref, cb_ref, sconv_ref,
               v_ref, cnew_ref, ya_ref, z_ref, xs_ref, bc_ref, xdtT_ref, dec_ref):
    x = x_ref[...]
    h = _rmsnorm(x, gmix_ref[...]) * (1.0 + mod_ref[1]) + mod_ref[0]
    hb = h.astype(BF16)
    uv = _dot(hb, wuv_ref[...])
    gv = gv_ref[...]
    for g in range(A_HEADS):
        sl = slice(g * A_HEAD_DIM, (g + 1) * A_HEAD_DIM)
        ug = _gelu(uv[:, sl])
        vn = _rmsnorm(_gelu(uv[:, A_WIDTH + g * A_HEAD_DIM:A_WIDTH + (g + 1) * A_HEAD_DIM]), gv[:, sl])
        v_ref[:, sl] = vn
        s = ws0_ref[:, sl] * vn + bs0_ref[:, sl]
        ya_ref[:, sl] = (ug * s).astype(BF16)
    z_ref[...] = _dot(hb, wz_ref[...])
    raw = _dot(hb, wxbc_ref[...])
    cd = raw.shape[1]
    acc = cb_ref[...] + raw * cw_ref[CONV_W - 1:CONV_W, :]
    for k in range(CONV_W - 1):
        acc = acc + sconv_ref[:, k * cd:(k + 1) * cd] * cw_ref[k:k + 1, :]
    cnew_ref[:, 0:(CONV_W - 2) * cd] = sconv_ref[:, cd:(CONV_W - 1) * cd]
    cnew_ref[:, (CONV_W - 2) * cd:(CONV_W - 1) * cd] = raw
    xbc = _silu(acc)
    xs = xbc[:, 0:B_WIDTH]
    xs_ref[...] = xs
    bc_ref[...] = xbc[:, B_WIDTH:CONV_DIM]
    dt = _softplus(_dot(hb, wdt_ref[...]) + dtb_ref[...])
    dec_ref[...] = jnp.exp(dt * (-jnp.exp(alog_ref[...])))
    f1, f2, f3 = _split3(dt)
    dt64 = _dot(f1, exp64_ref[...]) + (_dot(f2, exp64_ref[...]) + _dot(f3, exp64_ref[...]))
    xdtT_ref[...] = (xs * dt64).T.astype(BF16)


def _s1_call(x, mod1, sconv, p):
    n, d = x.shape
    outs = [
        jax.ShapeDtypeStruct((n, A_WIDTH), F32),
        jax.ShapeDtypeStruct((n, (CONV_W - 1) * CONV_DIM), F32),
        jax.ShapeDtypeStruct((n, A_WIDTH), BF16),
        jax.ShapeDtypeStruct((n, B_WIDTH), F32),
        jax.ShapeDtypeStruct((n, B_WIDTH), F32),
        jax.ShapeDtypeStruct((n, 2 * BC_WIDTH), F32),
        jax.ShapeDtypeStruct((B_WIDTH, n), BF16),
        jax.ShapeDtypeStruct((n, LANES), F32),
    ]
    args = (x, mod1, p["g_mix"], p["w_uv"], p["w_z"], p["w_xbc"], p["w_dt"], p["dt_bias_row"], p["a_log_row"],
            p["expand64"], p["g_v"], p["w_s00"], p["b_s0"], p["conv_w"], p["conv_b"], sconv)
    return pl.pallas_call(
        _s1_kernel,
        grid=(1,),
        in_specs=[_full(a.shape) for a in args],
        out_specs=[_full(o.shape) for o in outs],
        out_shape=outs,
        compiler_params=_params(("arbitrary",)),
        name="mix_sample_in",
    )(*args)


def _s2_kernel(dec_ref, st_ref, xdtT_ref, bc_ref, cblk_ref, o_ref, y_ref):
    i = pl.program_id(0)
    bt = st_ref.shape[0]
    n = bc_ref.shape[0]
    half = B_HEADS * B_HEAD_DIM // B_GROUPS
    rowi = lax.broadcasted_iota(jnp.int32, (n, 1), 0)
    for bb in range(bt):
        b = i * bt + bb
        for g in range(B_GROUPS):
            bm = bc_ref[:, g * D_STATE:(g + 1) * D_STATE]
            rb = jnp.where(rowi == b, bm, 0.0).astype(BF16)
            outer = _dot(xdtT_ref[g * half:(g + 1) * half, :], rb)
            for k in range(B_HEADS // B_GROUPS):
                hd = g * (B_HEADS // B_GROUPS) + k
                sl = slice(k * B_HEAD_DIM, (k + 1) * B_HEAD_DIM)
                o_ref[bb, g, sl, :] = st_ref[bb, g, sl, :] * dec_ref[b, hd] + outer[sl, :]
            crow = cblk_ref[bb:bb + 1, BC_WIDTH + g * D_STATE:BC_WIDTH + (g + 1) * D_STATE]
            c8 = jnp.broadcast_to(crow, (SUBLANES, D_STATE)).astype(BF16)
            yr = _dot_nt(c8, o_ref[bb, g].astype(BF16))
            y_ref[bb:bb + 1, g * half:(g + 1) * half] = yr[0:1]


def _s2_call(dec, state, xdtT, bc, bt=S2_TOKENS):
    n = state.shape[0]
    half = B_HEADS * B_HEAD_DIM // B_GROUPS
    st = state.reshape(n, B_GROUPS, half, D_STATE)
    grid_spec = pltpu.PrefetchScalarGridSpec(
        num_scalar_prefetch=1,
        grid=(n // bt,),
        in_specs=[
            pl.BlockSpec((bt, B_GROUPS, half, D_STATE), lambda i, d: (i, 0, 0, 0)),
            pl.BlockSpec(xdtT.shape, lambda i, d: (0, 0)),
            pl.BlockSpec(bc.shape, lambda i, d: (0, 0)),
            pl.BlockSpec((bt, bc.shape[1]), lambda i, d: (i, 0)),
        ],
        out_specs=[
            pl.BlockSpec((bt, B_GROUPS, half, D_STATE), lambda i, d: (i, 0, 0, 0)),
            pl.BlockSpec((bt, B_WIDTH), lambda i, d: (i, 0)),
        ],
    )
    new_state, y = pl.pallas_call(
        _s2_kernel,
        grid_spec=grid_spec,
        out_shape=[jax.ShapeDtypeStruct(st.shape, F32), jax.ShapeDtypeStruct((n, B_WIDTH), F32)],
        compiler_params=_params(("arbitrary",)),
        name="mix_sample_state",
    )(dec, st, xdtT, bc, bc)
    return new_state.reshape(state.shape), y


def _s3_kernel(x_ref, mod_ref, y_ref, xs_ref, z_ref, ya_ref, dsk_ref, gssm_ref, wout_ref, o_ref):
    y = (y_ref[...] + dsk_ref[...] * xs_ref[...]) * _silu(z_ref[...])
    half = B_HEADS * B_HEAD_DIM // B_GROUPS
    parts = [ya_ref[...]]
    for g in range(B_GROUPS):
        sl = slice(g * half, (g + 1) * half)
        parts.append(_rmsnorm(y[:, sl], gssm_ref[:, sl]).astype(BF16))
    cat = jnp.concatenate(parts, axis=1)
    o_ref[...] = x_ref[...] + mod_ref[2] * _dot(cat, wout_ref[...])


def _s3_call(x, mod1, y, xs, z, ya, p):
    args = (x, mod1, y, xs, z, ya, p["d_skip_exp"], p["g_ssm"], p["w_out"])
    return pl.pallas_call(
        _s3_kernel,
        grid=(1,),
        in_specs=[_full(a.shape) for a in args],
        out_specs=_full(x.shape),
        out_shape=jax.ShapeDtypeStruct(x.shape, F32),
        compiler_params=_params(("arbitrary",)),
        name="mix_sample_out",
    )(*args)


def _cand_blocks(a, b):
    row8 = lax.broadcasted_iota(jnp.int32, (SUBLANES, 1), 0)
    a8 = a[0:SUBLANES]
    blocks = []
    for j in range(SUBLANES):
        blk = a8 + b[j:j + 1]
        cnt = min(SUBLANES, (TOPK + 1) // (j + 1))
        if cnt < SUBLANES:
            blk = jnp.where(row8 < cnt, blk, NEG_INF)
        blocks.append(blk)
    blocks.append(a[0:1] + b[SUBLANES:TOPK])
    blocks.append(a[SUBLANES:TOPK] + b[0:1])
    last = jnp.where(row8 == 0, a[0:1] + b[TOPK:TOPK + 1], jnp.where(row8 == 1, a[TOPK:TOPK + 1] + b[0:1], NEG_INF))
    blocks.append(last)
    return blocks


def _merge_exchange_pairs(n):
    pairs = []
    t = max(1, math.ceil(math.log2(n)))
    p = 1 << (t - 1)
    while p > 0:
        q, r, d = 1 << (t - 1), 0, p
        while d > 0:
            pairs.extend((i, i + d) for i in range(n - d) if (i & p) == r)
            d, q, r = q - p, q >> 1, p
        p >>= 1
    return pairs


def _sort_blocks_desc(blocks):
    blocks = list(blocks)
    for i, j in _merge_exchange_pairs(len(blocks)):
        hi = jnp.maximum(blocks[i], blocks[j])
        blocks[j] = jnp.minimum(blocks[i], blocks[j])
        blocks[i] = hi
    return blocks


def _pop_largest(blocks, k):
    blocks = list(blocks)
    nb = len(blocks)
    rows = []
    for t in range(k):
        m = jnp.max(blocks[0], axis=0, keepdims=True)
        rows.append(m)
        need = min(k - t - 1, nb)
        if need == 0:
            break
        sel = blocks[0] == m
        for j in range(need):
            nxt = blocks[j + 1] if j + 1 < nb else NEG_INF
            blocks[j] = jnp.where(sel, nxt, blocks[j])
    return rows


def _peer_tile(c, n_chunks, x_ref, mod_ref, gffn_ref, gfin_ref, wqT_ref, keys_ref, u_ref, vt_ref, y_ref,
               h2_ref, q_ref, s1_ref, s2_ref, e1_ref, e2_ref, tops_ref, act_ref, coef_ref, outT_ref):
    n_blk = u_ref.shape[0] // N_KEYS

    @pl.when(c == 0)
    def _():
        x = x_ref[...]
        h = _rmsnorm(x, gffn_ref[...]) * (1.0 + mod_ref[1]) + mod_ref[0]
        ht = h.T.astype(BF16)
        h2_ref[...] = ht
        qh = wqT_ref.shape[0] // 2
        q_ref[0:qh, :] = _dot(wqT_ref[0:qh, :], ht)
        q_ref[qh:, :] = _dot(wqT_ref[qh:, :], ht)
        outT_ref[...] = jnp.zeros_like(outT_ref)

        def one_head(hh, top_ref):
            scores = []
            for s in range(2):
                r0 = pl.multiple_of((2 * hh + s) * N_KEYS, N_KEYS)
                qs = q_ref[pl.ds(r0, N_KEYS), :].astype(BF16)
                sc = _dot(keys_ref[2 * hh + s], qs) * LOG2E
                if s == 0:
                    s1_ref[hh] = sc
                else:
                    s2_ref[hh] = sc
                scores.append(sc)
            n_tok = x_ref.shape[0]
            strip = min(n_tok, PEER_SELECT_STRIP)
            for c0 in range(0, n_tok, strip):
                cols = slice(c0, c0 + strip)
                for s in range(2):
                    srt = _sort_blocks_desc([scores[s][j * SUBLANES:(j + 1) * SUBLANES, cols]
                                             for j in range(N_KEYS // SUBLANES)])
                    for k, m in enumerate(_pop_largest(srt, TOPK + 1)):
                        top_ref[s, k:k + 1, cols] = m
                a = top_ref[0, 0:TOPK + 1, cols]
                b = top_ref[1, 0:TOPK + 1, cols]
                best = a[0:1] + b[0:1]
                pops = _pop_largest(_sort_blocks_desc(_cand_blocks(a, b)), TOPK + 1)
                zsum = None
                for m in pops[:TOPK]:
                    e = jnp.exp2(m - best)
                    zsum = e if zsum is None else zsum + e
                thr = jnp.where(pops[TOPK] == NEG_INF, pops[TOPK - 1], 0.5 * (pops[TOPK - 1] + pops[TOPK]))
                s1 = s1_ref[hh, :, cols]
                s1_ref[hh, :, cols] = thr - s1
                e1_ref[hh, :, cols] = 0.5 * jnp.exp2(s1 - (a[0:1] + jnp.log2(zsum)))
                e2_ref[hh, :, cols] = jnp.exp2(scores[1][:, cols] - b[0:1]).astype(BF16)

        def head_body(hp, carry):
            for e in range(PEER_HEADS_PER_TRIP):
                one_head(PEER_HEADS_PER_TRIP * hp + e, tops_ref.at[e])
            return carry

        lax.fori_loop(0, R_HEADS // PEER_HEADS_PER_TRIP, head_body, 0)

    eh = u_ref.shape[0] // 2
    dh = vt_ref.shape[0] // 2

    @pl.when(c < n_chunks)
    def _():
        act_w = act_ref.at[c % 2]
        act_w[0:eh, :] = _dot(u_ref[0:eh, :], h2_ref[...])
        act_w[eh:, :] = _dot(u_ref[eh:, :], h2_ref[...])

    @pl.when(c >= 1)
    def _():
        act_r = act_ref.at[(c - 1) % 2]
        i1 = pl.multiple_of((c - 1) * n_blk, n_blk)
        u_rows = [s1_ref[hh, pl.ds(i1, n_blk), :] for hh in range(R_HEADS)]
        e1_rows = [e1_ref[hh, pl.ds(i1, n_blk), :] for hh in range(R_HEADS)]
        for r in range(n_blk):
            acc = None
            for hh in range(R_HEADS):
                prod = e2_ref[hh] * e1_rows[hh][r:r + 1, :].astype(BF16)
                gate = jnp.where(s2_ref[hh] >= u_rows[hh][r:r + 1, :], prod, jnp.zeros_like(prod))
                acc = gate if acc is None else acc + gate
            blk = slice(r * N_KEYS, (r + 1) * N_KEYS)
            x = act_r[blk, :]
            coef_ref[blk, :] = acc * (x * (1.0 + lax.erf(x * (1.0 / math.sqrt(2.0))))).astype(BF16)
        outT_ref[0:dh, :] += _dot(vt_ref[0:dh, :], coef_ref[...])
        outT_ref[dh:, :] += _dot(vt_ref[dh:, :], coef_ref[...])

    @pl.when(c == n_chunks)
    def _():
        x2 = x_ref[...] + mod_ref[2] * outT_ref[...].T
        y_ref[...] = _rmsnorm(x2, gfin_ref[...]) * (1.0 + mod_ref[4]) + mod_ref[3]


def _peer_kernel(x_ref, mod_ref, gffn_ref, gfin_ref, wqT_ref, keys_ref, u_ref, vt_ref, y_ref,
                 h2_ref, q_ref, s1_ref, s2_ref, e1_ref, e2_ref, tops_ref, act_ref, coef_ref, outT_ref,
                 *, n_chunks, n_sub):
    c = pl.program_id(1)
    tl = x_ref.shape[0] // n_sub

    def sub_tile(j, carry):
        rows = pl.ds(pl.multiple_of(j * tl, tl), tl)
        _peer_tile(c, n_chunks, x_ref.at[rows], mod_ref, gffn_ref, gfin_ref, wqT_ref, keys_ref, u_ref, vt_ref,
                   y_ref.at[rows], h2_ref.at[j], q_ref, s1_ref.at[j], s2_ref.at[j], e1_ref.at[j], e2_ref.at[j],
                   tops_ref, act_ref.at[j], coef_ref, outT_ref.at[j])
        return carry

    lax.fori_loop(0, n_sub, sub_tile, 0)


def _peer_call(x, mod5, tiles_per_row, p, tl, n_sub, ec=PEER_EXPERTS):
    t, d = x.shape
    n_exp = p["expert_u"].shape[0]
    r = mod5.shape[2]
    scratch = [
        pltpu.VMEM((n_sub, d, tl), BF16),
        pltpu.VMEM((R_HEADS * 2 * N_KEYS, tl), F32),
        pltpu.VMEM((n_sub, R_HEADS, N_KEYS, tl), F32),
        pltpu.VMEM((n_sub, R_HEADS, N_KEYS, tl), F32),
        pltpu.VMEM((n_sub, R_HEADS, N_KEYS, tl), F32),
        pltpu.VMEM((n_sub, R_HEADS, N_KEYS, tl), BF16),
        pltpu.VMEM((PEER_HEADS_PER_TRIP, 2, TOPK + SUBLANES, tl), F32),
        pltpu.VMEM((n_sub, 2, ec, tl), F32),
        pltpu.VMEM((ec, tl), BF16),
        pltpu.VMEM((n_sub, d, tl), F32),
    ]
    n_chunks = n_exp // ec
    return pl.pallas_call(
        functools.partial(_peer_kernel, n_chunks=n_chunks, n_sub=n_sub),
        grid=(t // (n_sub * tl), n_chunks + 1),
        in_specs=[
            pl.BlockSpec((n_sub * tl, d), lambda i, c: (i, 0)),
            pl.BlockSpec((None, 5, r, d), lambda i, c: (i // tiles_per_row, 0, 0, 0)),
            _full((1, d)), _full((1, d)),
            _full(p["w_qT"].shape), _full(p["keys"].shape),
            pl.BlockSpec((ec, d), lambda i, c: (jnp.minimum(c, n_chunks - 1), 0)),
            pl.BlockSpec((d, ec), lambda i, c: (0, jnp.maximum(c - 1, 0))),
        ],
        out_specs=pl.BlockSpec((n_sub * tl, d), lambda i, c: (i, 0)),
        out_shape=jax.ShapeDtypeStruct((t, d), F32),
        scratch_shapes=scratch,
        compiler_params=_params(("parallel", "arbitrary")),
        name="peer",
    )(x, mod5, p["g_ffn"], p["g_final"], p["w_qT"], p["keys"], p["expert_u"], p["expert_vT"])


def _table_kernel(x_ref, o_ref, *, transpose):
    x = x_ref[...]
    o_ref[...] = (x.T if transpose else x).astype(BF16)


def _table_call(table, transpose, rows=TABLE_ROWS):
    n, d = table.shape
    out_block = pl.BlockSpec((d, rows), lambda i: (0, i)) if transpose else pl.BlockSpec((rows, d), lambda i: (i, 0))
    return pl.pallas_call(
        functools.partial(_table_kernel, transpose=transpose),
        grid=(n // rows,),
        in_specs=[pl.BlockSpec((rows, d), lambda i: (i, 0))],
        out_specs=out_block,
        out_shape=jax.ShapeDtypeStruct((d, n) if transpose else (n, d), BF16),
        compiler_params=_params(("parallel",)),
        name="table_t" if transpose else "table",
    )(table)


def _prep(w_in, g_mix, g_v, w_s, b_s, conv_w, conv_b, dt_bias, a_log, d_skip, g_ssm, w_out, g_ffn, w_q,
          sub_keys, expert_u, expert_v, g_final):
    d = w_in.shape[0]
    o_z = 2 * A_WIDTH
    o_x = o_z + B_WIDTH
    o_dt = o_x + CONV_DIM
    w_dt = jnp.zeros((d, LANES), F32).at[:, :B_HEADS].set(w_in[:, o_dt:o_dt + B_HEADS])
    pad16 = lambda v: jnp.zeros((LANES,), F32).at[:B_HEADS].set(v)
    heads = jnp.arange(LANES)[:, None]
    p = {
        "g_mix": g_mix.reshape(1, d),
        "w_uv": w_in[:, 0:o_z].astype(BF16),
        "w_z": w_in[:, o_z:o_x].astype(BF16),
        "w_xbc": w_in[:, o_x:o_dt].astype(BF16),
        "w_dt": w_dt.astype(BF16),
        "w_dtT": w_dt.T.astype(BF16),
        "dt_bias_row": pad16(dt_bias).reshape(1, LANES),
        "dt_bias_col": pad16(dt_bias).reshape(LANES, 1),
        "a_log_row": pad16(a_log).reshape(1, LANES),
        "a_log_sq": jnp.broadcast_to(pad16(a_log).reshape(LANES, 1), (LANES, LANES)),
        "expand64": (heads == jnp.arange(B_HEADS * B_HEAD_DIM)[None, :] // B_HEAD_DIM).astype(BF16),
        "g_v": g_v.reshape(1, -1),
        "w_s": w_s,
        "b_s_rep": jnp.broadcast_to(b_s[:, :, None], (A_HEADS, CHUNK, CHUNK)),
        "w_s00": jnp.repeat(w_s[:, 0, 0], A_HEAD_DIM).reshape(1, -1),
        "b_s0": jnp.repeat(b_s[:, 0], A_HEAD_DIM).reshape(1, -1),
        "conv_w": conv_w,
        "conv_b": conv_b.reshape(1, -1),
        "d_skip_exp": jnp.repeat(d_skip, B_HEAD_DIM).reshape(1, -1),
        "g_ssm": g_ssm.reshape(1, -1),
        "w_out": w_out.astype(BF16),
        "g_ffn": g_ffn.reshape(1, d),
        "g_final": g_final.reshape(1, d),
        "w_qT": w_q.T.astype(BF16),
        "keys": sub_keys.reshape(R_HEADS * 2, N_KEYS, -1).astype(BF16),
        "expert_u": _table_call(expert_u, transpose=False),
        "expert_vT": _table_call(expert_v, transpose=True),
    }
    return p


def kernel(x_prompt, x_sample, c_prompt, c_sample, state_ssm, state_conv, w_ada, b_ada, g_mix, w_in, g_v, w_s, b_s,
           conv_w, conv_b, dt_bias, a_log, d_skip, g_ssm, w_out, g_ffn, w_q, sub_keys, expert_u, expert_v,
           w_ada_f, b_ada_f, g_final):
    assert w_ada.shape[0] == 1, "single-layer trunk"
    bp, seq, d = x_prompt.shape
    ns = x_sample.shape[0]
    p = _prep(w_in[0], g_mix[0], g_v[0], w_s[0], b_s[0], conv_w[0], conv_b[0], dt_bias[0], a_log[0], d_skip[0],
              g_ssm[0], w_out[0], g_ffn[0], w_q[0], sub_keys[0], expert_u[0], expert_v[0], g_final)

    c_all = jnp.concatenate([c_prompt, c_sample], axis=0)
    mod = _ada_call(c_all, w_ada[0], b_ada[0]).reshape(bp + ns, N_MOD, d)
    modf = _ada_call(c_all, w_ada_f, b_ada_f).reshape(bp + ns, 2, d)
    mod5 = jnp.concatenate([mod[:, 3:6], modf], axis=1)

    x1_p, ssm_p, ctail_p = _mix_call(x_prompt, mod[:bp, 0:3].reshape(bp, 3, 1, d), p)
    y_p = _peer_call(x1_p.reshape(bp * seq, d), mod5[:bp].reshape(bp, 5, 1, d), seq // (PEER_SUBTILES * PEER_TOKENS), p,
                     PEER_TOKENS, PEER_SUBTILES)

    xs_in = x_sample.reshape(ns, d)
    mod1_s = jnp.transpose(mod[bp:, 0:3], (1, 0, 2))
    sconv = state_conv[0].reshape(ns, -1)
    v_s, cnew_s, ya_s, z_s, xc_s, bc_s, xdtT_s, dec_s = _s1_call(xs_in, mod1_s, sconv, p)
    ssm_s, yssd_s = _s2_call(dec_s[:, :B_HEADS], state_ssm[0], xdtT_s, bc_s)
    x1_s = _s3_call(xs_in, mod1_s, yssd_s, xc_s, z_s, ya_s, p)
    mod5_s = jnp.transpose(mod5[bp:], (1, 0, 2)).reshape(1, 5, ns, d)
    y_s = _peer_call(x1_s, mod5_s, 1, p, ns, 1)

    return (
        y_p.reshape(bp, seq, d),
        y_s.reshape(ns, 1, d),
        ssm_p.reshape(1, bp, B_HEADS, B_HEAD_DIM, D_STATE),
        ctail_p[:, SUBLANES - (CONV_W - 1):, :].reshape(1, bp, CONV_W - 1, -1),
        ssm_s.reshape(1, ns, B_HEADS, B_HEAD_DIM, D_STATE),
        cnew_s.reshape(1, ns, CONV_W - 1, -1),
        v_s.reshape(1, ns, 1, -1),
    )
```

```python
import functools
import math

import jax
import jax.numpy as jnp
from jax import lax
from jax.experimental import pallas as pl
from jax.experimental.pallas import tpu as pltpu

F32 = jnp.float32
BF16 = jnp.bfloat16
NEG_INF = float("-inf")

EPS = 1e-6
A_HEADS = 8
A_HEAD_DIM = 128
CHUNK = 128
B_HEADS = 16
B_HEAD_DIM = 64
B_GROUPS = 2
D_STATE = 128
CONV_W = 4
N_KEYS = 128
R_HEADS = 8
TOPK = 16
N_MOD = 6
A_WIDTH = A_HEADS * A_HEAD_DIM
B_WIDTH = B_HEADS * B_HEAD_DIM
BC_WIDTH = B_GROUPS * D_STATE
CONV_DIM = B_WIDTH + 2 * BC_WIDTH
MIX_WIDTH = A_WIDTH + B_WIDTH

LANES = 128
SUBLANES = 8
VMEM_LIMIT = 56 * 1024 * 1024

MIX_ROWS = 512
PEER_TOKENS = 256
PEER_EXPERTS = 2048
PEER_SUBTILES = 2
PEER_SELECT_STRIP = 256
PEER_HEADS_PER_TRIP = 8
LOG2E = 1.4426950408889634
S2_TOKENS = 8
TABLE_ROWS = 1024


def _dot(a, b):
    return jnp.dot(a, b, preferred_element_type=F32)


def _dot_nt(a, b):
    return lax.dot_general(a, b, (((1,), (1,)), ((), ())), preferred_element_type=F32)


def _dot_tn(a, b):
    return lax.dot_general(a, b, (((0,), (0,)), ((), ())), preferred_element_type=F32)


def _split2(x):
    hi = x.astype(BF16)
    lo = (x - hi.astype(F32)).astype(BF16)
    return hi, lo


def _split3(x):
    hi = x.astype(BF16)
    r = x - hi.astype(F32)
    mid = r.astype(BF16)
    lo = (r - mid.astype(F32)).astype(BF16)
    return hi, mid, lo


def _dot_x3(a, b):
    a1, a2 = _split2(a)
    b1, b2 = _split2(b)
    return _dot(a1, b1) + (_dot(a1, b2) + _dot(a2, b1))


def _silu(x):
    return x / (1.0 + jnp.exp(-x))


def _gelu(x):
    return 0.5 * x * (1.0 + lax.erf(x * (1.0 / math.sqrt(2.0))))


def _softplus(x):
    return jnp.maximum(x, 0.0) + jnp.log1p(jnp.exp(-jnp.abs(x)))


def _rmsnorm(x, g):
    return x * lax.rsqrt(jnp.mean(x * x, axis=-1, keepdims=True) + EPS) * g


def _full(shape):
    nd = len(shape)
    return pl.BlockSpec(shape, lambda *_: (0,) * nd, pipeline_mode=pl.Buffered(1))


def _dot2(a, w_ref):
    h = w_ref.shape[1] // 2
    return jnp.concatenate([_dot(a, w_ref[:, :h]), _dot(a, w_ref[:, h:])], axis=1)


def _params(sem):
    return pltpu.CompilerParams(dimension_semantics=sem, vmem_limit_bytes=VMEM_LIMIT)


def _ada_kernel(c_ref, w_ref, b_ref, o_ref):
    o_ref[...] = _dot_x3(_silu(c_ref[...]), w_ref[...]) + b_ref[...]


def _ada_call(c, w, b, tn=512):
    m, k = c.shape
    n = w.shape[1]
    return pl.pallas_call(
        _ada_kernel,
        grid=(n // tn,),
        in_specs=[_full((m, k)), pl.BlockSpec((k, tn), lambda j: (0, j)), pl.BlockSpec((1, tn), lambda j: (0, j))],
        out_specs=pl.BlockSpec((m, tn), lambda j: (0, j)),
        out_shape=jax.ShapeDtypeStruct((m, n), F32),
        compiler_params=_params(("parallel",)),
        name="ada",
    )(c, w, b.reshape(1, n))


def _gate_mlp_chunk(u, v, gv, ws_ref, bs_ref, causal):
    ys, vs = [], []
    for g in range(A_HEADS):
        sl = slice(g * A_HEAD_DIM, (g + 1) * A_HEAD_DIM)
        ug = _gelu(u[:, sl])
        vn = _rmsnorm(_gelu(v[:, sl]), gv[:, sl])
        w = jnp.where(causal, ws_ref[g], 0.0).astype(BF16)
        s = _dot(w, vn.astype(BF16)) + bs_ref[g]
        ys.append((ug * s).astype(BF16))
        vs.append(vn)
    return ys, vs


def _mix_kernel(x_ref, mod_ref, gmix_ref, wuv_ref, wz_ref, wxbc_ref, wdt_ref, wdtT_ref,
                dtb_ref, dtbT_ref, alog_ref, alogT_ref, gv_ref, ws_ref, bs_ref,
                cw_ref, cb_ref, dsk_ref, gssm_ref, wout_ref,
                o_ref, ssm_ref, ctail_ref,
                hst_ref, tail_ref, uv_ref, z_ref, xs_ref, dt_ref, dtT_ref, ycat_ref):
    t = pl.program_id(1)
    tm = x_ref.shape[0]
    n_chunks = tm // CHUNK

    @pl.when(t == 0)
    def _():
        hst_ref[...] = jnp.zeros_like(hst_ref)
        tail_ref[...] = jnp.zeros_like(tail_ref)

    x = x_ref[...]
    h = _rmsnorm(x, gmix_ref[...]) * (1.0 + mod_ref[1]) + mod_ref[0]
    hb = h.astype(BF16)
    uv_ref[...] = _dot2(hb, wuv_ref)
    z_ref[...] = _dot2(hb, wz_ref)
    dt_ref[...] = _softplus(_dot(hb, wdt_ref[...]) + dtb_ref[...])
    dtT_ref[...] = _softplus(_dot_nt(wdtT_ref[...], hb) + dtbT_ref[...])

    cur = _dot2(hb, wxbc_ref)
    prev = tail_ref[...]
    row8 = lax.broadcasted_iota(jnp.int32, (SUBLANES, 1), 0)
    acc = cb_ref[...] + cur * cw_ref[CONV_W - 1:CONV_W, :]
    for k in range(1, CONV_W):
        r = pltpu.roll(cur, k, axis=0)
        head = jnp.where(row8 < k, pltpu.roll(prev, k, axis=0), r[0:SUBLANES])
        shifted = jnp.concatenate([head, r[SUBLANES:]], axis=0)
        acc = acc + shifted * cw_ref[CONV_W - 1 - k:CONV_W - k, :]
    xs_ref[...] = _silu(acc)
    new_tail = cur[tm - SUBLANES:tm]
    tail_ref[...] = new_tail
    ctail_ref[...] = new_tail

    ri = lax.broadcasted_iota(jnp.int32, (CHUNK, CHUNK), 0)
    ci = lax.broadcasted_iota(jnp.int32, (CHUNK, CHUNK), 1)
    causal = ri >= ci
    tril = jnp.where(causal, 1.0, 0.0).astype(BF16)
    triu = jnp.where(ri <= ci, 1.0, 0.0).astype(BF16)
    low_half = ri < B_HEAD_DIM
    low_lane = ci < B_HEAD_DIM
    a_col = -jnp.exp(alog_ref[...])
    a_row = -jnp.exp(alogT_ref[...])

    def chunk_body(c, carry):
        r0 = pl.multiple_of(c * CHUNK, CHUNK)
        rows = pl.ds(r0, CHUNK)

        ys, _ = _gate_mlp_chunk(uv_ref[rows, 0:A_WIDTH], uv_ref[rows, A_WIDTH:2 * A_WIDTH], gv_ref[...], ws_ref, bs_ref, causal)
        for g in range(A_HEADS):
            ycat_ref[rows, g * A_HEAD_DIM:(g + 1) * A_HEAD_DIM] = ys[g]

        dt = dt_ref[rows, :]
        d_a = dt * a_col
        h1, h2, h3 = _split3(d_a)
        acum = _dot(tril, h1) + (_dot(tril, h2) + _dot(tril, h3))
        dt_t = dtT_ref[:, rows]
        g1, g2, g3 = _split3(dt_t * a_row)
        acum_t = _dot(g1, triu) + (_dot(g2, triu) + _dot(g3, triu))

        xs = xs_ref[rows, 0:B_WIDTH]
        zz = z_ref[rows, :]
        y_pairs = []
        for g in range(B_GROUPS):
            bg = xs_ref[rows, B_WIDTH + g * D_STATE:B_WIDTH + (g + 1) * D_STATE]
            cg = xs_ref[rows, B_WIDTH + BC_WIDTH + g * D_STATE:B_WIDTH + BC_WIDTH + (g + 1) * D_STATE]
            cbm = _dot_nt(cg.astype(BF16), bg.astype(BF16))
            for pr in range(B_HEADS // B_GROUPS // 2):
                pair = g * 4 + pr
                xpair = xs[:, pair * 128:(pair + 1) * 128]
                hprev = hst_ref[pair]
                gds, xhs, css, hps, bscs, alasts = [], [], [], [], [], []
                for e in range(2):
                    hd = pair * 2 + e
                    ab = jnp.broadcast_to(acum[:, hd:hd + 1], (CHUNK, CHUNK))
                    seg = ab - acum_t[hd:hd + 1, :]
                    lm = jnp.exp(jnp.where(causal, seg, NEG_INF))
                    gds.append((cbm * lm * dt_t[hd:hd + 1, :]).astype(BF16))
                    sel_l = low_lane if e == 0 else jnp.logical_not(low_lane)
                    sel_r = low_half if e == 0 else jnp.logical_not(low_half)
                    xhs.append(jnp.where(sel_l, xpair, 0.0).astype(BF16))
                    hps.append(jnp.where(sel_r, hprev, 0.0).astype(BF16))
                    css.append((cg * jnp.exp(ab)).astype(BF16))
                    alast = ab[CHUNK - 1:CHUNK, :]
                    dt_col = jnp.broadcast_to(dt[:, hd:hd + 1], (CHUNK, CHUNK))
                    bscs.append((bg * (jnp.exp(alast - ab) * dt_col)).astype(BF16))
                    alasts.append(alast)
                x2 = jnp.concatenate(xhs, axis=0)
                y_pairs.append(_dot(jnp.concatenate(gds, axis=1), x2)
                               + _dot_nt(jnp.concatenate(css, axis=1), jnp.concatenate(hps, axis=1)))
                st = _dot_tn(x2, jnp.concatenate(bscs, axis=0))
                decay = jnp.exp(jnp.where(low_half, alasts[0], alasts[1]))
                hst_ref[pair] = hprev * decay + st
        y = jnp.concatenate(y_pairs, axis=1) + dsk_ref[...] * xs
        y = y * _silu(zz)
        half = B_HEADS * B_HEAD_DIM // B_GROUPS
        for g in range(B_GROUPS):
            sl = slice(g * half, (g + 1) * half)
            ycat_ref[rows, A_WIDTH + g * half:A_WIDTH + (g + 1) * half] = _rmsnorm(y[:, sl], gssm_ref[:, sl]).astype(BF16)
        return carry

    lax.fori_loop(0, n_chunks, chunk_body, 0)

    o_ref[...] = x + mod_ref[2] * _dot2(ycat_ref[...], wout_ref)

    @pl.when(t == pl.num_programs(1) - 1)
    def _():
        ssm_ref[...] = hst_ref[...]


def _mix_call(x, mod1, p, tm=MIX_ROWS):
    b, s, d = x.shape
    n_pairs = B_HEADS // 2
    in_specs = [
        pl.BlockSpec((None, tm, d), lambda i, t: (i, t, 0)),
        pl.BlockSpec((None, 3, 1, d), lambda i, t: (i, 0, 0, 0)),
        _full((1, d)),
        _full(p["w_uv"].shape), _full(p["w_z"].shape), _full(p["w_xbc"].shape),
        _full(p["w_dt"].shape), _full(p["w_dtT"].shape),
        _full((1, LANES)), _full((LANES, 1)), _full((1, LANES)), _full((LANES, LANES)),
        _full((1, A_WIDTH)), _full((A_HEADS, CHUNK, CHUNK)), _full((A_HEADS, CHUNK, CHUNK)),
        _full((CONV_W, CONV_DIM)), _full((1, CONV_DIM)), _full((1, B_WIDTH)), _full((1, B_WIDTH)),
        _full(p["w_out"].shape),
    ]
    out_specs = [
        pl.BlockSpec((None, tm, d), lambda i, t: (i, t, 0)),
        pl.BlockSpec((None, n_pairs, 128, D_STATE), lambda i, t: (i, 0, 0, 0)),
        pl.BlockSpec((None, SUBLANES, CONV_DIM), lambda i, t: (i, 0, 0)),
    ]
    out_shape = [
        jax.ShapeDtypeStruct((b, s, d), F32),
        jax.ShapeDtypeStruct((b, n_pairs, 128, D_STATE), F32),
        jax.ShapeDtypeStruct((b, SUBLANES, CONV_DIM), F32),
    ]
    scratch = [
        pltpu.VMEM((n_pairs, 128, D_STATE), F32),
        pltpu.VMEM((SUBLANES, CONV_DIM), F32),
        pltpu.VMEM((tm, 2 * A_WIDTH), F32),
        pltpu.VMEM((tm, B_WIDTH), F32),
        pltpu.VMEM((tm, CONV_DIM), F32),
        pltpu.VMEM((tm, LANES), F32),
        pltpu.VMEM((LANES, tm), F32),
        pltpu.VMEM((tm, MIX_WIDTH), BF16),
    ]
    return pl.pallas_call(
        _mix_kernel,
        grid=(b, s // tm),
        in_specs=in_specs, out_specs=out_specs, out_shape=out_shape, scratch_shapes=scratch,
        compiler_params=_params(("parallel", "arbitrary")),
        name="mix_prompt",
    )(x, mod1, p["g_mix"], p["w_uv"], p["w_z"], p["w_xbc"], p["w_dt"], p["w_dtT"],
      p["dt_bias_row"], p["dt_bias_col"], p["a_log_row"], p["a_log_sq"],
      p["g_v"], p["w_s"], p["b_s_rep"], p["conv_w"], p["conv_b"], p["d_skip_exp"], p["g_ssm"], p["w_out"])


def _s1_kernel(x_ref, mod_ref, gmix_ref, wuv_ref, wz_ref, wxbc_ref, wdt_ref, dtb_ref, alog_ref, exp64_ref,
               gv_ref, ws0_ref, bs0_ref, cw_ref, cb_ref, sconv_ref,
               v_ref, cnew_ref, ya_ref, z_ref, xs_ref, bc_ref, xdtT_ref, dec_ref):
    x = x_ref[...]
    h = _rmsnorm(x, gmix_ref[...]) * (1.0 + mod_ref[1]) + mod_ref[0]
    hb = h.astype(BF16)
    uv = _dot(hb, wuv_ref[...])
    gv = gv_ref[...]
    for g in range(A_HEADS):
        sl = slice(g * A_HEAD_DIM, (g + 1) * A_HEAD_DIM)
        ug = _gelu(uv[:, sl])
        vn = _rmsnorm(_gelu(uv[:, A_WIDTH + g * A_HEAD_DIM:A_WIDTH + (g + 1) * A_HEAD_DIM]), gv[:, sl])
        v_ref[:, sl] = vn
        s = ws0_ref[:, sl] * vn + bs0_ref[:, sl]
        ya_ref[:, sl] = (ug * s).astype(BF16)
    z_ref[...] = _dot(hb, wz_ref[...])
    raw = _dot(hb, wxbc_ref[...])
    cd = raw.shape[1]
    acc = cb_ref[...] + raw * cw_ref[CONV_W - 1:CONV_W, :]
    for k in range(CONV_W - 1):
        acc = acc + sconv_ref[:, k * cd:(k + 1) * cd] * cw_ref[k:k + 1, :]
    cnew_ref[:, 0:(CONV_W - 2) * cd] = sconv_ref[:, cd:(CONV_W - 1) * cd]
    cnew_ref[:, (CONV_W - 2) * cd:(CONV_W - 1) * cd] = raw
    xbc = _silu(acc)
    xs = xbc[:, 0:B_WIDTH]
    xs_ref[...] = xs
    bc_ref[...] = xbc[:, B_WIDTH:CONV_DIM]
    dt = _softplus(_dot(hb, wdt_ref[...]) + dtb_ref[...])
    dec_ref[...] = jnp.exp(dt * (-jnp.exp(alog_ref[...])))
    f1, f2, f3 = _split3(dt)
    dt64 = _dot(f1, exp64_ref[...]) + (_dot(f2, exp64_ref[...]) + _dot(f3, exp64_ref[...]))
    xdtT_ref[...] = (xs * dt64).T.astype(BF16)


def _s1_call(x, mod1, sconv, p):
    n, d = x.shape
    outs = [
        jax.ShapeDtypeStruct((n, A_WIDTH), F32),
        jax.ShapeDtypeStruct((n, (CONV_W - 1) * CONV_DIM), F32),
        jax.ShapeDtypeStruct((n, A_WIDTH), BF16),
        jax.ShapeDtypeStruct((n, B_WIDTH), F32),
        jax.ShapeDtypeStruct((n, B_WIDTH), F32),
        jax.ShapeDtypeStruct((n, 2 * BC_WIDTH), F32),
        jax.ShapeDtypeStruct((B_WIDTH, n), BF16),
        jax.ShapeDtypeStruct((n, LANES), F32),
    ]
    args = (x, mod1, p["g_mix"], p["w_uv"], p["w_z"], p["w_xbc"], p["w_dt"], p["dt_bias_row"], p["a_log_row"],
            p["expand64"], p["g_v"], p["w_s00"], p["b_s0"], p["conv_w"], p["conv_b"], sconv)
    return pl.pallas_call(
        _s1_kernel,
        grid=(1,),
        in_specs=[_full(a.shape) for a in args],
        out_specs=[_full(o.shape) for o in outs],
        out_shape=outs,
        compiler_params=_params(("arbitrary",)),
        name="mix_sample_in",
    )(*args)


def _s2_kernel(dec_ref, st_ref, xdtT_ref, bc_ref, cblk_ref, o_ref, y_ref):
    i = pl.program_id(0)
    bt = st_ref.shape[0]
    n = bc_ref.shape[0]
    half = B_HEADS * B_HEAD_DIM // B_GROUPS
    rowi = lax.broadcasted_iota(jnp.int32, (n, 1), 0)
    for bb in range(bt):
        b = i * bt + bb
        for g in range(B_GROUPS):
            bm = bc_ref[:, g * D_STATE:(g + 1) * D_STATE]
            rb = jnp.where(rowi == b, bm, 0.0).astype(BF16)
            outer = _dot(xdtT_ref[g * half:(g + 1) * half, :], rb)
            for k in range(B_HEADS // B_GROUPS):
                hd = g * (B_HEADS // B_GROUPS) + k
                sl = slice(k * B_HEAD_DIM, (k + 1) * B_HEAD_DIM)
                o_ref[bb, g, sl, :] = st_ref[bb, g, sl, :] * dec_ref[b, hd] + outer[sl, :]
            crow = cblk_ref[bb:bb + 1, BC_WIDTH + g * D_STATE:BC_WIDTH + (g + 1) * D_STATE]
            c8 = jnp.broadcast_to(crow, (SUBLANES, D_STATE)).astype(BF16)
            yr = _dot_nt(c8, o_ref[bb, g].astype(BF16))
            y_ref[bb:bb + 1, g * half:(g + 1) * half] = yr[0:1]


def _s2_call(dec, state, xdtT, bc, bt=S2_TOKENS):
    n = state.shape[0]
    half = B_HEADS * B_HEAD_DIM // B_GROUPS
    st = state.reshape(n, B_GROUPS, half, D_STATE)
    grid_spec = pltpu.PrefetchScalarGridSpec(
        num_scalar_prefetch=1,
        grid=(n // bt,),
        in_specs=[
            pl.BlockSpec((bt, B_GROUPS, half, D_STATE), lambda i, d: (i, 0, 0, 0)),
            pl.BlockSpec(xdtT.shape, lambda i, d: (0, 0)),
            pl.BlockSpec(bc.shape, lambda i, d: (0, 0)),
            pl.BlockSpec((bt, bc.shape[1]), lambda i, d: (i, 0)),
        ],
        out_specs=[
            pl.BlockSpec((bt, B_GROUPS, half, D_STATE), lambda i, d: (i, 0, 0, 0)),
            pl.BlockSpec((bt, B_WIDTH), lambda i, d: (i, 0)),
        ],
    )
    new_state, y = pl.pallas_call(
        _s2_kernel,
        grid_spec=grid_spec,
        out_shape=[jax.ShapeDtypeStruct(st.shape, F32), jax.ShapeDtypeStruct((n, B_WIDTH), F32)],
        compiler_params=_params(("arbitrary",)),
        name="mix_sample_state",
    )(dec, st, xdtT, bc, bc)
    return new_state.reshape(state.shape), y


def _s3_kernel(x_ref, mod_ref, y_ref, xs_ref, z_ref, ya_ref, dsk_ref, gssm_ref, wout_ref, o_ref):
    y = (y_ref[...] + dsk_ref[...] * xs_ref[...]) * _silu(z_ref[...])
    half = B_HEADS * B_HEAD_DIM // B_GROUPS
    parts = [ya_ref[...]]
    for g in range(B_GROUPS):
        sl = slice(g * half, (g + 1) * half)
        parts.append(_rmsnorm(y[:, sl], gssm_ref[:, sl]).astype(BF16))
    cat = jnp.concatenate(parts, axis=1)
    o_ref[...] = x_ref[...] + mod_ref[2] * _dot(cat, wout_ref[...])


def _s3_call(x, mod1, y, xs, z, ya, p):
    args = (x, mod1, y, xs, z, ya, p["d_skip_exp"], p["g_ssm"], p["w_out"])
    return pl.pallas_call(
        _s3_kernel,
        grid=(1,),
        in_specs=[_full(a.shape) for a in args],
        out_specs=_full(x.shape),
        out_shape=jax.ShapeDtypeStruct(x.shape, F32),
        compiler_params=_params(("arbitrary",)),
        name="mix_sample_out",
    )(*args)


def _cand_blocks(a, b):
    row8 = lax.broadcasted_iota(jnp.int32, (SUBLANES, 1), 0)
    a8 = a[0:SUBLANES]
    blocks = []
    for j in range(SUBLANES):
        blk = a8 + b[j:j + 1]
        cnt = min(SUBLANES, (TOPK + 1) // (j + 1))
        if cnt < SUBLANES:
            blk = jnp.where(row8 < cnt, blk, NEG_INF)
        blocks.append(blk)
    blocks.append(a[0:1] + b[SUBLANES:TOPK])
    blocks.append(a[SUBLANES:TOPK] + b[0:1])
    last = jnp.where(row8 == 0, a[0:1] + b[TOPK:TOPK + 1], jnp.where(row8 == 1, a[TOPK:TOPK + 1] + b[0:1], NEG_INF))
    blocks.append(last)
    return blocks


def _merge_exchange_pairs(n):
    pairs = []
    t = max(1, math.ceil(math.log2(n)))
    p = 1 << (t - 1)
    while p > 0:
        q, r, d = 1 << (t - 1), 0, p
        while d > 0:
            pairs.extend((i, i + d) for i in range(n - d) if (i & p) == r)
            d, q, r = q - p, q >> 1, p
        p >>= 1
    return pairs


def _sort_blocks_desc(blocks):
    blocks = list(blocks)
    for i, j in _merge_exchange_pairs(len(blocks)):
        hi = jnp.maximum(blocks[i], blocks[j])
        blocks[j] = jnp.minimum(blocks[i], blocks[j])
        blocks[i] = hi
    return blocks


def _pop_largest(blocks, k):
    blocks = list(blocks)
    nb = len(blocks)
    rows = []
    for t in range(k):
        m = jnp.max(blocks[0], axis=0, keepdims=True)
        rows.append(m)
        need = min(k - t - 1, nb)
        if need == 0:
            break
        sel = blocks[0] == m
        for j in range(need):
            nxt = blocks[j + 1] if j + 1 < nb else NEG_INF
            blocks[j] = jnp.where(sel, nxt, blocks[j])
    return rows


def _peer_tile(c, n_chunks, x_ref, mod_ref, gffn_ref, gfin_ref, wqT_ref, keys_ref, u_ref, vt_ref, y_ref,
               h2_ref, q_ref, s1_ref, s2_ref, e1_ref, e2_ref, tops_ref, act_ref, coef_ref, outT_ref):
    n_blk = u_ref.shape[0] // N_KEYS

    @pl.when(c == 0)
    def _():
        x = x_ref[...]
        h = _rmsnorm(x, gffn_ref[...]) * (1.0 + mod_ref[1]) + mod_ref[0]
        ht = h.T.astype(BF16)
        h2_ref[...] = ht
        qh = wqT_ref.shape[0] // 2
        q_ref[0:qh, :] = _dot(wqT_ref[0:qh, :], ht)
        q_ref[qh:, :] = _dot(wqT_ref[qh:, :], ht)
        outT_ref[...] = jnp.zeros_like(outT_ref)

        def one_head(hh, top_ref):
            scores = []
            for s in range(2):
                r0 = pl.multiple_of((2 * hh + s) * N_KEYS, N_KEYS)
                qs = q_ref[pl.ds(r0, N_KEYS), :].astype(BF16)
                sc = _dot(keys_ref[2 * hh + s], qs) * LOG2E
                if s == 0:
                    s1_ref[hh] = sc
                else:
                    s2_ref[hh] = sc
                scores.append(sc)
            n_tok = x_ref.shape[0]
            strip = min(n_tok, PEER_SELECT_STRIP)
            for c0 in range(0, n_tok, strip):
                cols = slice(c0, c0 + strip)
                for s in range(2):
                    srt = _sort_blocks_desc([scores[s][j * SUBLANES:(j + 1) * SUBLANES, cols]
                                             for j in range(N_KEYS // SUBLANES)])
                    for k, m in enumerate(_pop_largest(srt, TOPK + 1)):
                        top_ref[s, k:k + 1, cols] = m
                a = top_ref[0, 0:TOPK + 1, cols]
                b = top_ref[1, 0:TOPK + 1, cols]
                best = a[0:1] + b[0:1]
                pops = _pop_largest(_sort_blocks_desc(_cand_blocks(a, b)), TOPK + 1)
                zsum = None
                for m in pops[:TOPK]:
                    e = jnp.exp2(m - best)
                    zsum = e if zsum is None else zsum + e
                thr = jnp.where(pops[TOPK] == NEG_INF, pops[TOPK - 1], 0.5 * (pops[TOPK - 1] + pops[TOPK]))
                s1 = s1_ref[hh, :, cols]
                s1_ref[hh, :, cols] = thr - s1
                e1_ref[hh, :, cols] = 0.5 * jnp.exp2(s1 - (a[0:1] + jnp.log2(zsum)))
                e2_ref[hh, :, cols] = jnp.exp2(scores[1][:, cols] - b[0:1]).astype(BF16)

        def head_body(hp, carry):
            for e in range(PEER_HEADS_PER_TRIP):
                one_head(PEER_HEADS_PER_TRIP * hp + e, tops_ref.at[e])
            return carry

        lax.fori_loop(0, R_HEADS // PEER_HEADS_PER_TRIP, head_body, 0)

    eh = u_ref.shape[0] // 2
    dh = vt_ref.shape[0] // 2

    @pl.when(c < n_chunks)
    def _():
        act_w = act_ref.at[c % 2]
        act_w[0:eh, :] = _dot(u_ref[0:eh, :], h2_ref[...])
        act_w[eh:, :] = _dot(u_ref[eh:, :], h2_ref[...])

    @pl.when(c >= 1)
    def _():
        act_r = act_ref.at[(c - 1) % 2]
        i1 = pl.multiple_of((c - 1) * n_blk, n_blk)
        u_rows = [s1_ref[hh, pl.ds(i1, n_blk), :] for hh in range(R_HEADS)]
        e1_rows = [e1_ref[hh, pl.ds(i1, n_blk), :] for hh in range(R_HEADS)]
        for r in range(n_blk):
            acc = None
            for hh in range(R_HEADS):
                prod = e2_ref[hh] * e1_rows[hh][r:r + 1, :].astype(BF16)
                gate = jnp.where(s2_ref[hh] >= u_rows[hh][r:r + 1, :], prod, jnp.zeros_like(prod))
                acc = gate if acc is None else acc + gate
            blk = slice(r * N_KEYS, (r + 1) * N_KEYS)
            x = act_r[blk, :].astype(BF16)
            coef_ref[blk, :] = acc * (x * (1.0 + lax.erf(x * (1.0 / math.sqrt(2.0)))))
        outT_ref[0:dh, :] += _dot(vt_ref[0:dh, :], coef_ref[...])
        outT_ref[dh:, :] += _dot(vt_ref[dh:, :], coef_ref[...])

    @pl.when(c == n_chunks)
    def _():
        x2 = x_ref[...] + mod_ref[2] * outT_ref[...].T
        y_ref[...] = _rmsnorm(x2, gfin_ref[...]) * (1.0 + mod_ref[4]) + mod_ref[3]


def _peer_kernel(x_ref, mod_ref, gffn_ref, gfin_ref, wqT_ref, keys_ref, u_ref, vt_ref, y_ref,
                 h2_ref, q_ref, s1_ref, s2_ref, e1_ref, e2_ref, tops_ref, act_ref, coef_ref, outT_ref,
                 *, n_chunks, n_sub):
    c = pl.program_id(1)
    tl = x_ref.shape[0] // n_sub

    def sub_tile(j, carry):
        rows = pl.ds(pl.multiple_of(j * tl, tl), tl)
        _peer_tile(c, n_chunks, x_ref.at[rows], mod_ref, gffn_ref, gfin_ref, wqT_ref, keys_ref, u_ref, vt_ref,
                   y_ref.at[rows], h2_ref.at[j], q_ref, s1_ref.at[j], s2_ref.at[j], e1_ref.at[j], e2_ref.at[j],
                   tops_ref, act_ref.at[j], coef_ref, outT_ref.at[j])
        return carry

    lax.fori_loop(0, n_sub, sub_tile, 0)


def _peer_call(x, mod5, tiles_per_row, p, tl, n_sub, ec=PEER_EXPERTS):
    t, d = x.shape
    n_exp = p["expert_u"].shape[0]
    r = mod5.shape[2]
    scratch = [
        pltpu.VMEM((n_sub, d, tl), BF16),
        pltpu.VMEM((R_HEADS * 2 * N_KEYS, tl), F32),
        pltpu.VMEM((n_sub, R_HEADS, N_KEYS, tl), F32),
        pltpu.VMEM((n_sub, R_HEADS, N_KEYS, tl), F32),
        pltpu.VMEM((n_sub, R_HEADS, N_KEYS, tl), F32),
        pltpu.VMEM((n_sub, R_HEADS, N_KEYS, tl), BF16),
        pltpu.VMEM((PEER_HEADS_PER_TRIP, 2, TOPK + SUBLANES, tl), F32),
        pltpu.VMEM((n_sub, 2, ec, tl), F32),
        pltpu.VMEM((ec, tl), BF16),
        pltpu.VMEM((n_sub, d, tl), F32),
    ]
    n_chunks = n_exp // ec
    return pl.pallas_call(
        functools.partial(_peer_kernel, n_chunks=n_chunks, n_sub=n_sub),
        grid=(t // (n_sub * tl), n_chunks + 1),
        in_specs=[
            pl.BlockSpec((n_sub * tl, d), lambda i, c: (i, 0)),
            pl.BlockSpec((None, 5, r, d), lambda i, c: (i // tiles_per_row, 0, 0, 0)),
            _full((1, d)), _full((1, d)),
            _full(p["w_qT"].shape), _full(p["keys"].shape),
            pl.BlockSpec((ec, d), lambda i, c: (jnp.minimum(c, n_chunks - 1), 0)),
            pl.BlockSpec((d, ec), lambda i, c: (0, jnp.maximum(c - 1, 0))),
        ],
        out_specs=pl.BlockSpec((n_sub * tl, d), lambda i, c: (i, 0)),
        out_shape=jax.ShapeDtypeStruct((t, d), F32),
        scratch_shapes=scratch,
        compiler_params=_params(("parallel", "arbitrary")),
        name="peer",
    )(x, mod5, p["g_ffn"], p["g_final"], p["w_qT"], p["keys"], p["expert_u"], p["expert_vT"])


def _table_kernel(x_ref, o_ref, *, transpose):
    x = x_ref[...]
    o_ref[...] = (x.T if transpose else x).astype(BF16)


def _table_call(table, transpose, rows=TABLE_ROWS):
    n, d = table.shape
    out_block = pl.BlockSpec((d, rows), lambda i: (0, i)) if transpose else pl.BlockSpec((rows, d), lambda i: (i, 0))
    return pl.pallas_call(
        functools.partial(_table_kernel, transpose=transpose),
        grid=(n // rows,),
        in_specs=[pl.BlockSpec((rows, d), lambda i: (i, 0))],
        out_specs=out_block,
        out_shape=jax.ShapeDtypeStruct((d, n) if transpose else (n, d), BF16),
        compiler_params=_params(("parallel",)),
        name="table_t" if transpose else "table",
    )(table)


def _prep(w_in, g_mix, g_v, w_s, b_s, conv_w, conv_b, dt_bias, a_log, d_skip, g_ssm, w_out, g_ffn, w_q,
          sub_keys, expert_u, expert_v, g_final):
    d = w_in.shape[0]
    o_z = 2 * A_WIDTH
    o_x = o_z + B_WIDTH
    o_dt = o_x + CONV_DIM
    w_dt = jnp.zeros((d, LANES), F32).at[:, :B_HEADS].set(w_in[:, o_dt:o_dt + B_HEADS])
    pad16 = lambda v: jnp.zeros((LANES,), F32).at[:B_HEADS].set(v)
    heads = jnp.arange(LANES)[:, None]
    p = {
        "g_mix": g_mix.reshape(1, d),
        "w_uv": w_in[:, 0:o_z].astype(BF16),
        "w_z": w_in[:, o_z:o_x].astype(BF16),
        "w_xbc": w_in[:, o_x:o_dt].astype(BF16),
        "w_dt": w_dt.astype(BF16),
        "w_dtT": w_dt.T.astype(BF16),
        "dt_bias_row": pad16(dt_bias).reshape(1, LANES),
        "dt_bias_col": pad16(dt_bias).reshape(LANES, 1),
        "a_log_row": pad16(a_log).reshape(1, LANES),
        "a_log_sq": jnp.broadcast_to(pad16(a_log).reshape(LANES, 1), (LANES, LANES)),
        "expand64": (heads == jnp.arange(B_HEADS * B_HEAD_DIM)[None, :] // B_HEAD_DIM).astype(BF16),
        "g_v": g_v.reshape(1, -1),
        "w_s": w_s,
        "b_s_rep": jnp.broadcast_to(b_s[:, :, None], (A_HEADS, CHUNK, CHUNK)),
        "w_s00": jnp.repeat(w_s[:, 0, 0], A_HEAD_DIM).reshape(1, -1),
        "b_s0": jnp.repeat(b_s[:, 0], A_HEAD_DIM).reshape(1, -1),
        "conv_w": conv_w,
        "conv_b": conv_b.reshape(1, -1),
        "d_skip_exp": jnp.repeat(d_skip, B_HEAD_DIM).reshape(1, -1),
        "g_ssm": g_ssm.reshape(1, -1),
        "w_out": w_out.astype(BF16),
        "g_ffn": g_ffn.reshape(1, d),
        "g_final": g_final.reshape(1, d),
        "w_qT": w_q.T.astype(BF16),
        "keys": sub_keys.reshape(R_HEADS * 2, N_KEYS, -1).astype(BF16),
        "expert_u": _table_call(expert_u, transpose=False),
        "expert_vT": _table_call(expert_v, transpose=True),
    }
    return p


def kernel(x_prompt, x_sample, c_prompt, c_sample, state_ssm, state_conv, w_ada, b_ada, g_mix, w_in, g_v, w_s, b_s,
           conv_w, conv_b, dt_bias, a_log, d_skip, g_ssm, w_out, g_ffn, w_q, sub_keys, expert_u, expert_v,
           w_ada_f, b_ada_f, g_final):
    assert w_ada.shape[0] == 1, "single-layer trunk"
    bp, seq, d = x_prompt.shape
    ns = x_sample.shape[0]
    p = _prep(w_in[0], g_mix[0], g_v[0], w_s[0], b_s[0], conv_w[0], conv_b[0], dt_bias[0], a_log[0], d_skip[0],
              g_ssm[0], w_out[0], g_ffn[0], w_q[0], sub_keys[0], expert_u[0], expert_v[0], g_final)

    c_all = jnp.concatenate([c_prompt, c_sample], axis=0)
    mod = _ada_call(c_all, w_ada[0], b_ada[0]).reshape(bp + ns, N_MOD, d)
    modf = _ada_call(c_all, w_ada_f, b_ada_f).reshape(bp + ns, 2, d)
    mod5 = jnp.concatenate([mod[:, 3:6], modf], axis=1)

    x1_p, ssm_p, ctail_p = _mix_call(x_prompt, mod[:bp, 0:3].reshape(bp, 3, 1, d), p)
    y_p = _peer_call(x1_p.reshape(bp * seq, d), mod5[:bp].reshape(bp, 5, 1, d), seq // (PEER_SUBTILES * PEER_TOKENS), p,
                     PEER_TOKENS, PEER_SUBTILES)

    xs_in = x_sample.reshape(ns, d)
    mod1_s = jnp.transpose(mod[bp:, 0:3], (1, 0, 2))
    sconv = state_conv[0].reshape(ns, -1)
    v_s, cnew_s, ya_s, z_s, xc_s, bc_s, xdtT_s, dec_s = _s1_call(xs_in, mod1_s, sconv, p)
    ssm_s, yssd_s = _s2_call(dec_s[:, :B_HEADS], state_ssm[0], xdtT_s, bc_s)
    x1_s = _s3_call(xs_in, mod1_s, yssd_s, xc_s, z_s, ya_s, p)
    mod5_s = jnp.transpose(mod5[bp:], (1, 0, 2)).reshape(1, 5, ns, d)
    y_s = _peer_call(x1_s, mod5_s, 1, p, ns, 1)

    return (
        y_p.reshape(bp, seq, d),
        y_s.reshape(ns, 1, d),
        ssm_p.reshape(1, bp, B_HEADS, B_HEAD_DIM, D_STATE),
        ctail_p[:, SUBLANES - (CONV_W - 1):, :].reshape(1, bp, CONV_W - 1, -1),
        ssm_s.reshape(1, ns, B_HEADS, B_HEAD_DIM, D_STATE),
        cnew_s.reshape(1, ns, CONV_W - 1, -1),
        v_s.reshape(1, ns, 1, -1),
    )
```

```python
import functools
import math

import jax
import jax.numpy as jnp
from jax import lax
from jax.experimental import pallas as pl
from jax.experimental.pallas import tpu as pltpu

F32 = jnp.float32
BF16 = jnp.bfloat16
NEG_INF = float("-inf")

EPS = 1e-6
A_HEADS = 8
A_HEAD_DIM = 128
CHUNK = 128
B_HEADS = 16
B_HEAD_DIM = 64
B_GROUPS = 2
D_STATE = 128
CONV_W = 4
N_KEYS = 128
R_HEADS = 8
TOPK = 16
N_MOD = 6
A_WIDTH = A_HEADS * A_HEAD_DIM
B_WIDTH = B_HEADS * B_HEAD_DIM
BC_WIDTH = B_GROUPS * D_STATE
CONV_DIM = B_WIDTH + 2 * BC_WIDTH
MIX_WIDTH = A_WIDTH + B_WIDTH

LANES = 128
SUBLANES = 8
VMEM_LIMIT = 56 * 1024 * 1024

MIX_ROWS = 512
PEER_TOKENS = 256
PEER_EXPERTS = 2048
PEER_SUBTILES = 2
PEER_SELECT_STRIP = 256
PEER_HEADS_PER_TRIP = 8
LOG2E = 1.4426950408889634
S2_TOKENS = 8
TABLE_ROWS = 1024


def _dot(a, b):
    return jnp.dot(a, b, preferred_element_type=F32)


def _dot_nt(a, b):
    return lax.dot_general(a, b, (((1,), (1,)), ((), ())), preferred_element_type=F32)


def _dot_tn(a, b):
    return lax.dot_general(a, b, (((0,), (0,)), ((), ())), preferred_element_type=F32)


def _split2(x):
    hi = x.astype(BF16)
    lo = (x - hi.astype(F32)).astype(BF16)
    return hi, lo


def _split3(x):
    hi = x.astype(BF16)
    r = x - hi.astype(F32)
    mid = r.astype(BF16)
    lo = (r - mid.astype(F32)).astype(BF16)
    return hi, mid, lo


def _dot_x3(a, b):
    a1, a2 = _split2(a)
    b1, b2 = _split2(b)
    return _dot(a1, b1) + (_dot(a1, b2) + _dot(a2, b1))


def _silu(x):
    return x / (1.0 + jnp.exp(-x))


def _gelu(x):
    return 0.5 * x * (1.0 + lax.erf(x * (1.0 / math.sqrt(2.0))))


def _softplus(x):
    return jnp.maximum(x, 0.0) + jnp.log1p(jnp.exp(-jnp.abs(x)))


def _rmsnorm(x, g):
    return x * lax.rsqrt(jnp.mean(x * x, axis=-1, keepdims=True) + EPS) * g


def _full(shape):
    nd = len(shape)
    return pl.BlockSpec(shape, lambda *_: (0,) * nd, pipeline_mode=pl.Buffered(1))


def _dot2(a, w_ref):
    h = w_ref.shape[1] // 2
    return jnp.concatenate([_dot(a, w_ref[:, :h]), _dot(a, w_ref[:, h:])], axis=1)


def _params(sem):
    return pltpu.CompilerParams(dimension_semantics=sem, vmem_limit_bytes=VMEM_LIMIT)


def _ada_kernel(c_ref, w_ref, b_ref, o_ref):
    o_ref[...] = _dot_x3(_silu(c_ref[...]), w_ref[...]) + b_ref[...]


def _ada_call(c, w, b, tn=512):
    m, k = c.shape
    n = w.shape[1]
    return pl.pallas_call(
        _ada_kernel,
        grid=(n // tn,),
        in_specs=[_full((m, k)), pl.BlockSpec((k, tn), lambda j: (0, j)), pl.BlockSpec((1, tn), lambda j: (0, j))],
        out_specs=pl.BlockSpec((m, tn), lambda j: (0, j)),
        out_shape=jax.ShapeDtypeStruct((m, n), F32),
        compiler_params=_params(("parallel",)),
        name="ada",
    )(c, w, b.reshape(1, n))


def _gate_mlp_chunk(u, v, gv, ws_ref, bs_ref, causal):
    ys, vs = [], []
    for g in range(A_HEADS):
        sl = slice(g * A_HEAD_DIM, (g + 1) * A_HEAD_DIM)
        ug = _gelu(u[:, sl])
        vn = _rmsnorm(_gelu(v[:, sl]), gv[:, sl])
        w = jnp.where(causal, ws_ref[g], 0.0).astype(BF16)
        s = _dot(w, vn.astype(BF16)) + bs_ref[g]
        ys.append((ug * s).astype(BF16))
        vs.append(vn)
    return ys, vs


def _mix_kernel(x_ref, mod_ref, gmix_ref, wuv_ref, wz_ref, wxbc_ref, wdt_ref, wdtT_ref,
                dtb_ref, dtbT_ref, alog_ref, alogT_ref, gv_ref, ws_ref, bs_ref,
                cw_ref, cb_ref, dsk_ref, gssm_ref, wout_ref,
                o_ref, ssm_ref, ctail_ref,
                hst_ref, tail_ref, uv_ref, z_ref, xs_ref, dt_ref, dtT_ref, ycat_ref):
    t = pl.program_id(1)
    tm = x_ref.shape[0]
    n_chunks = tm // CHUNK

    @pl.when(t == 0)
    def _():
        hst_ref[...] = jnp.zeros_like(hst_ref)
        tail_ref[...] = jnp.zeros_like(tail_ref)

    x = x_ref[...]
    h = _rmsnorm(x, gmix_ref[...]) * (1.0 + mod_ref[1]) + mod_ref[0]
    hb = h.astype(BF16)
    uv_ref[...] = _dot2(hb, wuv_ref)
    z_ref[...] = _dot2(hb, wz_ref)
    dt_ref[...] = _softplus(_dot(hb, wdt_ref[...]) + dtb_ref[...])
    dtT_ref[...] = _softplus(_dot_nt(wdtT_ref[...], hb) + dtbT_ref[...])

    cur = _dot2(hb, wxbc_ref)
    prev = tail_ref[...]
    row8 = lax.broadcasted_iota(jnp.int32, (SUBLANES, 1), 0)
    acc = cb_ref[...] + cur * cw_ref[CONV_W - 1:CONV_W, :]
    for k in range(1, CONV_W):
        r = pltpu.roll(cur, k, axis=0)
        head = jnp.where(row8 < k, pltpu.roll(prev, k, axis=0), r[0:SUBLANES])
        shifted = jnp.concatenate([head, r[SUBLANES:]], axis=0)
        acc = acc + shifted * cw_ref[CONV_W - 1 - k:CONV_W - k, :]
    xs_ref[...] = _silu(acc)
    new_tail = cur[tm - SUBLANES:tm]
    tail_ref[...] = new_tail
    ctail_ref[...] = new_tail

    ri = lax.broadcasted_iota(jnp.int32, (CHUNK, CHUNK), 0)
    ci = lax.broadcasted_iota(jnp.int32, (CHUNK, CHUNK), 1)
    causal = ri >= ci
    tril = jnp.where(causal, 1.0, 0.0).astype(BF16)
    triu = jnp.where(ri <= ci, 1.0, 0.0).astype(BF16)
    low_half = ri < B_HEAD_DIM
    low_lane = ci < B_HEAD_DIM
    a_col = -jnp.exp(alog_ref[...])
    a_row = -jnp.exp(alogT_ref[...])

    def chunk_body(c, carry):
        r0 = pl.multiple_of(c * CHUNK, CHUNK)
        rows = pl.ds(r0, CHUNK)

        ys, _ = _gate_mlp_chunk(uv_ref[rows, 0:A_WIDTH], uv_ref[rows, A_WIDTH:2 * A_WIDTH], gv_ref[...], ws_ref, bs_ref, causal)
        for g in range(A_HEADS):
            ycat_ref[rows, g * A_HEAD_DIM:(g + 1) * A_HEAD_DIM] = ys[g]

        dt = dt_ref[rows, :]
        d_a = dt * a_col
        h1, h2, h3 = _split3(d_a)
        acum = _dot(tril, h1) + (_dot(tril, h2) + _dot(tril, h3))
        dt_t = dtT_ref[:, rows]
        g1, g2, g3 = _split3(dt_t * a_row)
        acum_t = _dot(g1, triu) + (_dot(g2, triu) + _dot(g3, triu))

        xs = xs_ref[rows, 0:B_WIDTH]
        zz = z_ref[rows, :]
        y_pairs = []
        for g in range(B_GROUPS):
            bg = xs_ref[rows, B_WIDTH + g * D_STATE:B_WIDTH + (g + 1) * D_STATE]
            cg = xs_ref[rows, B_WIDTH + BC_WIDTH + g * D_STATE:B_WIDTH + BC_WIDTH + (g + 1) * D_STATE]
            cbm = _dot_nt(cg.astype(BF16), bg.astype(BF16))
            for pr in range(B_HEADS // B_GROUPS // 2):
                pair = g * 4 + pr
                xpair = xs[:, pair * 128:(pair + 1) * 128]
                hprev = hst_ref[pair]
                gds, xhs, css, hps, bscs, alasts = [], [], [], [], [], []
                for e in range(2):
                    hd = pair * 2 + e
                    ab = jnp.broadcast_to(acum[:, hd:hd + 1], (CHUNK, CHUNK))
                    seg = ab - acum_t[hd:hd + 1, :]
                    lm = jnp.exp(jnp.where(causal, seg, NEG_INF))
                    gds.append((cbm * lm * dt_t[hd:hd + 1, :]).astype(BF16))
                    sel_l = low_lane if e == 0 else jnp.logical_not(low_lane)
                    sel_r = low_half if e == 0 else jnp.logical_not(low_half)
                    xhs.append(jnp.where(sel_l, xpair, 0.0).astype(BF16))
                    hps.append(jnp.where(sel_r, hprev, 0.0).astype(BF16))
                    css.append((cg * jnp.exp(ab)).astype(BF16))
                    alast = ab[CHUNK - 1:CHUNK, :]
                    dt_col = jnp.broadcast_to(dt[:, hd:hd + 1], (CHUNK, CHUNK))
                    bscs.append((bg * (jnp.exp(alast - ab) * dt_col)).astype(BF16))
                    alasts.append(alast)
                x2 = jnp.concatenate(xhs, axis=0)
                y_pairs.append(_dot(jnp.concatenate(gds, axis=1), x2)
                               + _dot_nt(jnp.concatenate(css, axis=1), jnp.concatenate(hps, axis=1)))
                st = _dot_tn(x2, jnp.concatenate(bscs, axis=0))
                decay = jnp.exp(jnp.where(low_half, alasts[0], alasts[1]))
                hst_ref[pair] = hprev * decay + st
        y = jnp.concatenate(y_pairs, axis=1) + dsk_ref[...] * xs
        y = y * _silu(zz)
        half = B_HEADS * B_HEAD_DIM // B_GROUPS
        for g in range(B_GROUPS):
            sl = slice(g * half, (g + 1) * half)
            ycat_ref[rows, A_WIDTH + g * half:A_WIDTH + (g + 1) * half] = _rmsnorm(y[:, sl], gssm_ref[:, sl]).astype(BF16)
        return carry

    lax.fori_loop(0, n_chunks, chunk_body, 0)

    o_ref[...] = x + mod_ref[2] * _dot2(ycat_ref[...], wout_ref)

    @pl.when(t == pl.num_programs(1) - 1)
    def _():
        ssm_ref[...] = hst_ref[...]


def _mix_call(x, mod1, p, tm=MIX_ROWS):
    b, s, d = x.shape
    n_pairs = B_HEADS // 2
    in_specs = [
        pl.BlockSpec((None, tm, d), lambda i, t: (i, t, 0)),
        pl.BlockSpec((None, 3, 1, d), lambda i, t: (i, 0, 0, 0)),
        _full((1, d)),
        _full(p["w_uv"].shape), _full(p["w_z"].shape), _full(p["w_xbc"].shape),
        _full(p["w_dt"].shape), _full(p["w_dtT"].shape),
        _full((1, LANES)), _full((LANES, 1)), _full((1, LANES)), _full((LANES, LANES)),
        _full((1, A_WIDTH)), _full((A_HEADS, CHUNK, CHUNK)), _full((A_HEADS, CHUNK, CHUNK)),
        _full((CONV_W, CONV_DIM)), _full((1, CONV_DIM)), _full((1, B_WIDTH)), _full((1, B_WIDTH)),
        _full(p["w_out"].shape),
    ]
    out_specs = [
        pl.BlockSpec((None, tm, d), lambda i, t: (i, t, 0)),
        pl.BlockSpec((None, n_pairs, 128, D_STATE), lambda i, t: (i, 0, 0, 0)),
        pl.BlockSpec((None, SUBLANES, CONV_DIM), lambda i, t: (i, 0, 0)),
    ]
    out_shape = [
        jax.ShapeDtypeStruct((b, s, d), F32),
        jax.ShapeDtypeStruct((b, n_pairs, 128, D_STATE), F32),
        jax.ShapeDtypeStruct((b, SUBLANES, CONV_DIM), F32),
    ]
    scratch = [
        pltpu.VMEM((n_pairs, 128, D_STATE), F32),
        pltpu.VMEM((SUBLANES, CONV_DIM), F32),
        pltpu.VMEM((tm, 2 * A_WIDTH), F32),
        pltpu.VMEM((tm, B_WIDTH), F32),
        pltpu.VMEM((tm, CONV_DIM), F32),
        pltpu.VMEM((tm, LANES), F32),
        pltpu.VMEM((LANES, tm), F32),
        pltpu.VMEM((tm, MIX_WIDTH), BF16),
    ]
    return pl.pallas_call(
        _mix_kernel,
        grid=(b, s // tm),
        in_specs=in_specs, out_specs=out_specs, out_shape=out_shape, scratch_shapes=scratch,
        compiler_params=_params(("parallel", "arbitrary")),
        name="mix_prompt",
    )(x, mod1, p["g_mix"], p["w_uv"], p["w_z"], p["w_xbc"], p["w_dt"], p["w_dtT"],
      p["dt_bias_row"], p["dt_bias_col"], p["a_log_row"], p["a_log_sq"],
      p["g_v"], p["w_s"], p["b_s_rep"], p["conv_w"], p["conv_b"], p["d_skip_exp"], p["g_ssm"], p["w_out"])


def _s1_kernel(x_ref, mod_ref, gmix_ref, wuv_ref, wz_ref, wxbc_ref, wdt_ref, dtb_ref, alog_ref, exp64_ref,
               gv_ref, ws0_ref, bs0_ref, cw_ref, cb_ref, sconv_ref,
               v_ref, cnew_ref, ya_ref, z_ref, xs_ref, bc_ref, xdtT_ref, dec_ref):
    x = x_ref[...]
    h = _rmsnorm(x, gmix_ref[...]) * (1.0 + mod_ref[1]) + mod_ref[0]
    hb = h.astype(BF16)
    uv = _dot(hb, wuv_ref[...])
    gv = gv_ref[...]
    for g in range(A_HEADS):
        sl = slice(g * A_HEAD_DIM, (g + 1) * A_HEAD_DIM)
        ug = _gelu(uv[:, sl])
        vn = _rmsnorm(_gelu(uv[:, A_WIDTH + g * A_HEAD_DIM:A_WIDTH + (g + 1) * A_HEAD_DIM]), gv[:, sl])
        v_ref[:, sl] = vn
        s = ws0_ref[:, sl] * vn + bs0_ref[:, sl]
        ya_ref[:, sl] = (ug * s).astype(BF16)
    z_ref[...] = _dot(hb, wz_ref[...])
    raw = _dot(hb, wxbc_ref[...])
    cd = raw.shape[1]
    acc = cb_ref[...] + raw * cw_ref[CONV_W - 1:CONV_W, :]
    for k in range(CONV_W - 1):
        acc = acc + sconv_ref[:, k * cd:(k + 1) * cd] * cw_ref[k:k + 1, :]
    cnew_ref[:, 0:(CONV_W - 2) * cd] = sconv_ref[:, cd:(CONV_W - 1) * cd]
    cnew_ref[:, (CONV_W - 2) * cd:(CONV_W - 1) * cd] = raw
    xbc = _silu(acc)
    xs = xbc[:, 0:B_WIDTH]
    xs_ref[...] = xs
    bc_ref[...] = xbc[:, B_WIDTH:CONV_DIM]
    dt = _softplus(_dot(hb, wdt_ref[...]) + dtb_ref[...])
    dec_ref[...] = jnp.exp(dt * (-jnp.exp(alog_ref[...])))
    f1, f2, f3 = _split3(dt)
    dt64 = _dot(f1, exp64_ref[...]) + (_dot(f2, exp64_ref[...]) + _dot(f3, exp64_ref[...]))
    xdtT_ref[...] = (xs * dt64).T.astype(BF16)


def _s1_call(x, mod1, sconv, p):
    n, d = x.shape
    outs = [
        jax.ShapeDtypeStruct((n, A_WIDTH), F32),
        jax.ShapeDtypeStruct((n, (CONV_W - 1) * CONV_DIM), F32),
        jax.ShapeDtypeStruct((n, A_WIDTH), BF16),
        jax.ShapeDtypeStruct((n, B_WIDTH), F32),
        jax.ShapeDtypeStruct((n, B_WIDTH), F32),
        jax.ShapeDtypeStruct((n, 2 * BC_WIDTH), F32),
        jax.ShapeDtypeStruct((B_WIDTH, n), BF16),
        jax.ShapeDtypeStruct((n, LANES), F32),
    ]
    args = (x, mod1, p["g_mix"], p["w_uv"], p["w_z"], p["w_xbc"], p["w_dt"], p["dt_bias_row"], p["a_log_row"],
            p["expand64"], p["g_v"], p["w_s00"], p["b_s0"], p["conv_w"], p["conv_b"], sconv)
    return pl.pallas_call(
        _s1_kernel,
        grid=(1,),
        in_specs=[_full(a.shape) for a in args],
        out_specs=[_full(o.shape) for o in outs],
        out_shape=outs,
        compiler_params=_params(("arbitrary",)),
        name="mix_sample_in",
    )(*args)


def _s2_kernel(dec_ref, st_ref, xdtT_ref, bc_ref, cblk_ref, o_ref, y_ref):
    i = pl.program_id(0)
    bt = st_ref.shape[0]
    n = bc_ref.shape[0]
    half = B_HEADS * B_HEAD_DIM // B_GROUPS
    rowi = lax.broadcasted_iota(jnp.int32, (n, 1), 0)
    for bb in range(bt):
        b = i * bt + bb
        for g in range(B_GROUPS):
            bm = bc_ref[:, g * D_STATE:(g + 1) * D_STATE]
            rb = jnp.where(rowi == b, bm, 0.0).astype(BF16)
            outer = _dot(xdtT_ref[g * half:(g + 1) * half, :], rb)
            for k in range(B_HEADS // B_GROUPS):
                hd = g * (B_HEADS // B_GROUPS) + k
                sl = slice(k * B_HEAD_DIM, (k + 1) * B_HEAD_DIM)
                o_ref[bb, g, sl, :] = st_ref[bb, g, sl, :] * dec_ref[b, hd] + outer[sl, :]
            crow = cblk_ref[bb:bb + 1, BC_WIDTH + g * D_STATE:BC_WIDTH + (g + 1) * D_STATE]
            c8 = jnp.broadcast_to(crow, (SUBLANES, D_STATE)).astype(BF16)
            yr = _dot_nt(c8, o_ref[bb, g].astype(BF16))
            y_ref[bb:bb + 1, g * half:(g + 1) * half] = yr[0:1]


def _s2_call(dec, state, xdtT, bc, bt=S2_TOKENS):
    n = state.shape[0]
    half = B_HEADS * B_HEAD_DIM // B_GROUPS
    st = state.reshape(n, B_GROUPS, half, D_STATE)
    grid_spec = pltpu.PrefetchScalarGridSpec(
        num_scalar_prefetch=1,
        grid=(n // bt,),
        in_specs=[
            pl.BlockSpec((bt, B_GROUPS, half, D_STATE), lambda i, d: (i, 0, 0, 0)),
            pl.BlockSpec(xdtT.shape, lambda i, d: (0, 0)),
            pl.BlockSpec(bc.shape, lambda i, d: (0, 0)),
            pl.BlockSpec((bt, bc.shape[1]), lambda i, d: (i, 0)),
        ],
        out_specs=[
            pl.BlockSpec((bt, B_GROUPS, half, D_STATE), lambda i, d: (i, 0, 0, 0)),
            pl.BlockSpec((bt, B_WIDTH), lambda i, d: (i, 0)),
        ],
    )
    new_state, y = pl.pallas_call(
        _s2_kernel,
        grid_spec=grid_spec,
        out_shape=[jax.ShapeDtypeStruct(st.shape, F32), jax.ShapeDtypeStruct((n, B_WIDTH), F32)],
        compiler_params=_params(("arbitrary",)),
        name="mix_sample_state",
    )(dec, st, xdtT, bc, bc)
    return new_state.reshape(state.shape), y


def _s3_kernel(x_ref, mod_ref, y_ref, xs_ref, z_ref, ya_ref, dsk_ref, gssm_ref, wout_ref, o_ref):
    y = (y_ref[...] + dsk_ref[...] * xs_ref[...]) * _silu(z_ref[...])
    half = B_HEADS * B_HEAD_DIM // B_GROUPS
    parts = [ya_ref[...]]
    for g in range(B_GROUPS):
        sl = slice(g * half, (g + 1) * half)
        parts.append(_rmsnorm(y[:, sl], gssm_ref[:, sl]).astype(BF16))
    cat = jnp.concatenate(parts, axis=1)
    o_ref[...] = x_ref[...] + mod_ref[2] * _dot(cat, wout_ref[...])


def _s3_call(x, mod1, y, xs, z, ya, p):
    args = (x, mod1, y, xs, z, ya, p["d_skip_exp"], p["g_ssm"], p["w_out"])
    return pl.pallas_call(
        _s3_kernel,
        grid=(1,),
        in_specs=[_full(a.shape) for a in args],
        out_specs=_full(x.shape),
        out_shape=jax.ShapeDtypeStruct(x.shape, F32),
        compiler_params=_params(("arbitrary",)),
        name="mix_sample_out",
    )(*args)


def _cand_blocks(a, b):
    row8 = lax.broadcasted_iota(jnp.int32, (SUBLANES, 1), 0)
    a8 = a[0:SUBLANES]
    blocks = []
    for j in range(SUBLANES):
        blk = a8 + b[j:j + 1]
        cnt = min(SUBLANES, (TOPK + 1) // (j + 1))
        if cnt < SUBLANES:
            blk = jnp.where(row8 < cnt, blk, NEG_INF)
        blocks.append(blk)
    blocks.append(a[0:1] + b[SUBLANES:TOPK])
    blocks.append(a[SUBLANES:TOPK] + b[0:1])
    last = jnp.where(row8 == 0, a[0:1] + b[TOPK:TOPK + 1], jnp.where(row8 == 1, a[TOPK:TOPK + 1] + b[0:1], NEG_INF))
    blocks.append(last)
    return blocks


def _merge_exchange_pairs(n):
    pairs = []
    t = max(1, math.ceil(math.log2(n)))
    p = 1 << (t - 1)
    while p > 0:
        q, r, d = 1 << (t - 1), 0, p
        while d > 0:
            pairs.extend((i, i + d) for i in range(n - d) if (i & p) == r)
            d, q, r = q - p, q >> 1, p
        p >>= 1
    return pairs


def _sort_blocks_desc(blocks):
    blocks = list(blocks)
    for i, j in _merge_exchange_pairs(len(blocks)):
        hi = jnp.maximum(blocks[i], blocks[j])
        blocks[j] = jnp.minimum(blocks[i], blocks[j])
        blocks[i] = hi
    return blocks


def _pop_largest(blocks, k):
    blocks = list(blocks)
    nb = len(blocks)
    rows = []
    for t in range(k):
        m = jnp.max(blocks[0], axis=0, keepdims=True)
        rows.append(m)
        need = min(k - t - 1, nb)
        if need == 0:
            break
        sel = blocks[0] == m
        for j in range(need):
            nxt = blocks[j + 1] if j + 1 < nb else NEG_INF
            blocks[j] = jnp.where(sel, nxt, blocks[j])
    return rows


def _peer_tile(c, n_chunks, x_ref, mod_ref, gffn_ref, gfin_ref, wqT_ref, keys_ref, u_ref, vt_ref, y_ref,
               h2_ref, q_ref, s1_ref, s2_ref, e1_ref, e2_ref, tops_ref, act_ref, coef_ref, outT_ref):
    n_blk = u_ref.shape[0] // N_KEYS

    @pl.when(c == 0)
    def _():
        x = x_ref[...]
        h = _rmsnorm(x, gffn_ref[...]) * (1.0 + mod_ref[1]) + mod_ref[0]
        ht = h.T.astype(BF16)
        h2_ref[...] = ht
        qh = wqT_ref.shape[0] // 2
        q_ref[0:qh, :] = _dot(wqT_ref[0:qh, :], ht)
        q_ref[qh:, :] = _dot(wqT_ref[qh:, :], ht)
        outT_ref[...] = jnp.zeros_like(outT_ref)

        def one_head(hh, top_ref):
            scores = []
            for s in range(2):
                r0 = pl.multiple_of((2 * hh + s) * N_KEYS, N_KEYS)
                qs = q_ref[pl.ds(r0, N_KEYS), :].astype(BF16)
                sc = _dot(keys_ref[2 * hh + s], qs) * LOG2E
                if s == 0:
                    s1_ref[hh] = sc
                else:
                    s2_ref[hh] = sc
                scores.append(sc)
            n_tok = x_ref.shape[0]
            strip = min(n_tok, PEER_SELECT_STRIP)
            for c0 in range(0, n_tok, strip):
                cols = slice(c0, c0 + strip)
                for s in range(2):
                    srt = _sort_blocks_desc([scores[s][j * SUBLANES:(j + 1) * SUBLANES, cols]
                                             for j in range(N_KEYS // SUBLANES)])
                    for k, m in enumerate(_pop_largest(srt, TOPK + 1)):
                        top_ref[s, k:k + 1, cols] = m
                a = top_ref[0, 0:TOPK + 1, cols]
                b = top_ref[1, 0:TOPK + 1, cols]
                best = a[0:1] + b[0:1]
                pops = _pop_largest(_sort_blocks_desc(_cand_blocks(a, b)), TOPK + 1)
                zsum = None
                for m in pops[:TOPK]:
                    e = jnp.exp2(m - best)
                    zsum = e if zsum is None else zsum + e
                thr = jnp.where(pops[TOPK] == NEG_INF, pops[TOPK - 1], 0.5 * (pops[TOPK - 1] + pops[TOPK]))
                s1 = s1_ref[hh, :, cols]
                s1_ref[hh, :, cols] = thr - s1
                e1_ref[hh, :, cols] = 0.5 * jnp.exp2(s1 - (a[0:1] + jnp.log2(zsum)))
                e2_ref[hh, :, cols] = jnp.exp2(scores[1][:, cols] - b[0:1]).astype(BF16)

        def head_body(hp, carry):
            for e in range(PEER_HEADS_PER_TRIP):
                one_head(PEER_HEADS_PER_TRIP * hp + e, tops_ref.at[e])
            return carry

        lax.fori_loop(0, R_HEADS // PEER_HEADS_PER_TRIP, head_body, 0)

    eh = u_ref.shape[0] // 2
    dh = vt_ref.shape[0] // 2

    @pl.when(c < n_chunks)
    def _():
        act_w = act_ref.at[c % 2]
        act_w[0:eh, :] = _dot(u_ref[0:eh, :], h2_ref[...])
        act_w[eh:, :] = _dot(u_ref[eh:, :], h2_ref[...])

    @pl.when(c >= 1)
    def _():
        act_r = act_ref.at[(c - 1) % 2]
        i1 = pl.multiple_of((c - 1) * n_blk, n_blk)
        for r in range(n_blk):
            if r % SUBLANES == 0:
                g0 = pl.multiple_of(i1 + r, SUBLANES)
                u_rows = [s1_ref[hh, pl.ds(g0, SUBLANES), :] for hh in range(R_HEADS)]
                e1_rows = [e1_ref[hh, pl.ds(g0, SUBLANES), :] for hh in range(R_HEADS)]
            k = r % SUBLANES
            acc = None
            for hh in range(R_HEADS):
                prod = e2_ref[hh] * e1_rows[hh][k:k + 1, :].astype(BF16)
                gate = jnp.where(s2_ref[hh] >= u_rows[hh][k:k + 1, :], prod, jnp.zeros_like(prod))
                acc = gate if acc is None else acc + gate
            blk = slice(r * N_KEYS, (r + 1) * N_KEYS)
            x = act_r[blk, :].astype(BF16)
            coef_ref[blk, :] = acc * (x * (1.0 + lax.erf(x * (1.0 / math.sqrt(2.0)))))
        outT_ref[0:dh, :] += _dot(vt_ref[0:dh, :], coef_ref[...])
        outT_ref[dh:, :] += _dot(vt_ref[dh:, :], coef_ref[...])

    @pl.when(c == n_chunks)
    def _():
        x2 = x_ref[...] + mod_ref[2] * outT_ref[...].T
        y_ref[...] = _rmsnorm(x2, gfin_ref[...]) * (1.0 + mod_ref[4]) + mod_ref[3]


def _peer_kernel(x_ref, mod_ref, gffn_ref, gfin_ref, wqT_ref, keys_ref, u_ref, vt_ref, y_ref,
                 h2_ref, q_ref, s1_ref, s2_ref, e1_ref, e2_ref, tops_ref, act_ref, coef_ref, outT_ref,
                 *, n_chunks, n_sub):
    c = pl.program_id(1)
    tl = x_ref.shape[0] // n_sub

    def sub_tile(j, carry):
        rows = pl.ds(pl.multiple_of(j * tl, tl), tl)
        _peer_tile(c, n_chunks, x_ref.at[rows], mod_ref, gffn_ref, gfin_ref, wqT_ref, keys_ref, u_ref, vt_ref,
                   y_ref.at[rows], h2_ref.at[j], q_ref, s1_ref.at[j], s2_ref.at[j], e1_ref.at[j], e2_ref.at[j],
                   tops_ref, act_ref.at[j], coef_ref, outT_ref.at[j])
        return carry

    lax.fori_loop(0, n_sub, sub_tile, 0)


def _peer_call(x, mod5, tiles_per_row, p, tl, n_sub, ec=PEER_EXPERTS):
    t, d = x.shape
    n_exp = p["expert_u"].shape[0]
    r = mod5.shape[2]
    scratch = [
        pltpu.VMEM((n_sub, d, tl), BF16),
        pltpu.VMEM((R_HEADS * 2 * N_KEYS, tl), F32),
        pltpu.VMEM((n_sub, R_HEADS, N_KEYS, tl), F32),
        pltpu.VMEM((n_sub, R_HEADS, N_KEYS, tl), F32),
        pltpu.VMEM((n_sub, R_HEADS, N_KEYS, tl), F32),
        pltpu.VMEM((n_sub, R_HEADS, N_KEYS, tl), BF16),
        pltpu.VMEM((PEER_HEADS_PER_TRIP, 2, TOPK + SUBLANES, tl), F32),
        pltpu.VMEM((n_sub, 2, ec, tl), F32),
        pltpu.VMEM((ec, tl), BF16),
        pltpu.VMEM((n_sub, d, tl), F32),
    ]
    n_chunks = n_exp // ec
    return pl.pallas_call(
        functools.partial(_peer_kernel, n_chunks=n_chunks, n_sub=n_sub),
        grid=(t // (n_sub * tl), n_chunks + 1),
        in_specs=[
            pl.BlockSpec((n_sub * tl, d), lambda i, c: (i, 0)),
            pl.BlockSpec((None, 5, r, d), lambda i, c: (i // tiles_per_row, 0, 0, 0)),
            _full((1, d)), _full((1, d)),
            _full(p["w_qT"].shape), _full(p["keys"].shape),
            pl.BlockSpec((ec, d), lambda i, c: (jnp.minimum(c, n_chunks - 1), 0)),
            pl.BlockSpec((d, ec), lambda i, c: (0, jnp.maximum(c - 1, 0))),
        ],
        out_specs=pl.BlockSpec((n_sub * tl, d), lambda i, c: (i, 0)),
        out_shape=jax.ShapeDtypeStruct((t, d), F32),
        scratch_shapes=scratch,
        compiler_params=_params(("parallel", "arbitrary")),
        name="peer",
    )(x, mod5, p["g_ffn"], p["g_final"], p["w_qT"], p["keys"], p["expert_u"], p["expert_vT"])


def _table_kernel(x_ref, o_ref, *, transpose):
    x = x_ref[...]
    o_ref[...] = (x.T if transpose else x).astype(BF16)


def _table_call(table, transpose, rows=TABLE_ROWS):
    n, d = table.shape
    out_block = pl.BlockSpec((d, rows), lambda i: (0, i)) if transpose else pl.BlockSpec((rows, d), lambda i: (i, 0))
    return pl.pallas_call(
        functools.partial(_table_kernel, transpose=transpose),
        grid=(n // rows,),
        in_specs=[pl.BlockSpec((rows, d), lambda i: (i, 0))],
        out_specs=out_block,
        out_shape=jax.ShapeDtypeStruct((d, n) if transpose else (n, d), BF16),
        compiler_params=_params(("parallel",)),
        name="table_t" if transpose else "table",
    )(table)


def _prep(w_in, g_mix, g_v, w_s, b_s, conv_w, conv_b, dt_bias, a_log, d_skip, g_ssm, w_out, g_ffn, w_q,
          sub_keys, expert_u, expert_v, g_final):
    d = w_in.shape[0]
    o_z = 2 * A_WIDTH
    o_x = o_z + B_WIDTH
    o_dt = o_x + CONV_DIM
    w_dt = jnp.zeros((d, LANES), F32).at[:, :B_HEADS].set(w_in[:, o_dt:o_dt + B_HEADS])
    pad16 = lambda v: jnp.zeros((LANES,), F32).at[:B_HEADS].set(v)
    heads = jnp.arange(LANES)[:, None]
    p = {
        "g_mix": g_mix.reshape(1, d),
        "w_uv": w_in[:, 0:o_z].astype(BF16),
        "w_z": w_in[:, o_z:o_x].astype(BF16),
        "w_xbc": w_in[:, o_x:o_dt].astype(BF16),
        "w_dt": w_dt.astype(BF16),
        "w_dtT": w_dt.T.astype(BF16),
        "dt_bias_row": pad16(dt_bias).reshape(1, LANES),
        "dt_bias_col": pad16(dt_bias).reshape(LANES, 1),
        "a_log_row": pad16(a_log).reshape(1, LANES),
        "a_log_sq": jnp.broadcast_to(pad16(a_log).reshape(LANES, 1), (LANES, LANES)),
        "expand64": (heads == jnp.arange(B_HEADS * B_HEAD_DIM)[None, :] // B_HEAD_DIM).astype(BF16),
        "g_v": g_v.reshape(1, -1),
        "w_s": w_s,
        "b_s_rep": jnp.broadcast_to(b_s[:, :, None], (A_HEADS, CHUNK, CHUNK)),
        "w_s00": jnp.repeat(w_s[:, 0, 0], A_HEAD_DIM).reshape(1, -1),
        "b_s0": jnp.repeat(b_s[:, 0], A_HEAD_DIM).reshape(1, -1),
        "conv_w": conv_w,
        "conv_b": conv_b.reshape(1, -1),
        "d_skip_exp": jnp.repeat(d_skip, B_HEAD_DIM).reshape(1, -1),
        "g_ssm": g_ssm.reshape(1, -1),
        "w_out": w_out.astype(BF16),
        "g_ffn": g_ffn.reshape(1, d),
        "g_final": g_final.reshape(1, d),
        "w_qT": w_q.T.astype(BF16),
        "keys": sub_keys.reshape(R_HEADS * 2, N_KEYS, -1).astype(BF16),
        "expert_u": _table_call(expert_u, transpose=False),
        "expert_vT": _table_call(expert_v, transpose=True),
    }
    return p


def kernel(x_prompt, x_sample, c_prompt, c_sample, state_ssm, state_conv, w_ada, b_ada, g_mix, w_in, g_v, w_s, b_s,
           conv_w, conv_b, dt_bias, a_log, d_skip, g_ssm, w_out, g_ffn, w_q, sub_keys, expert_u, expert_v,
           w_ada_f, b_ada_f, g_final):
    assert w_ada.shape[0] == 1, "single-layer trunk"
    bp, seq, d = x_prompt.shape
    ns = x_sample.shape[0]
    p = _prep(w_in[0], g_mix[0], g_v[0], w_s[0], b_s[0], conv_w[0], conv_b[0], dt_bias[0], a_log[0], d_skip[0],
              g_ssm[0], w_out[0], g_ffn[0], w_q[0], sub_keys[0], expert_u[0], expert_v[0], g_final)

    c_all = jnp.concatenate([c_prompt, c_sample], axis=0)
    mod = _ada_call(c_all, w_ada[0], b_ada[0]).reshape(bp + ns, N_MOD, d)
    modf = _ada_call(c_all, w_ada_f, b_ada_f).reshape(bp + ns, 2, d)
    mod5 = jnp.concatenate([mod[:, 3:6], modf], axis=1)

    x1_p, ssm_p, ctail_p = _mix_call(x_prompt, mod[:bp, 0:3].reshape(bp, 3, 1, d), p)
    y_p = _peer_call(x1_p.reshape(bp * seq, d), mod5[:bp].reshape(bp, 5, 1, d), seq // (PEER_SUBTILES * PEER_TOKENS), p,
                     PEER_TOKENS, PEER_SUBTILES)

    xs_in = x_sample.reshape(ns, d)
    mod1_s = jnp.transpose(mod[bp:, 0:3], (1, 0, 2))
    sconv = state_conv[0].reshape(ns, -1)
    v_s, cnew_s, ya_s, z_s, xc_s, bc_s, xdtT_s, dec_s = _s1_call(xs_in, mod1_s, sconv, p)
    ssm_s, yssd_s = _s2_call(dec_s[:, :B_HEADS], state_ssm[0], xdtT_s, bc_s)
    x1_s = _s3_call(xs_in, mod1_s, yssd_s, xc_s, z_s, ya_s, p)
    mod5_s = jnp.transpose(mod5[bp:], (1, 0, 2)).reshape(1, 5, ns, d)
    y_s = _peer_call(x1_s, mod5_s, 1, p, ns, 1)

    return (
        y_p.reshape(bp, seq, d),
        y_s.reshape(ns, 1, d),
        ssm_p.reshape(1, bp, B_HEADS, B_HEAD_DIM, D_STATE),
        ctail_p[:, SUBLANES - (CONV_W - 1):, :].reshape(1, bp, CONV_W - 1, -1),
        ssm_s.reshape(1, ns, B_HEADS, B_HEAD_DIM, D_STATE),
        cnew_s.reshape(1, ns, CONV_W - 1, -1),
        v_s.reshape(1, ns, 1, -1),
    )
```
